```python
import math
import jax, jax.numpy as jnp
from jax import lax
import numpy as np

D_MODEL = 1024
BATCH = 8
SEQ = 2048
DEPTH = 1
DEC_BATCH = 128
DEC_SEQ = 4
PAST_LEN = 16384
PAGE_SIZE = 128

MIX_WIDTH = D_MODEL
RET_WIDTH = MIX_WIDTH // 2
CONV_WIDTH = MIX_WIDTH - RET_WIDTH
RET_HEADS = 4
RET_DK = RET_WIDTH // RET_HEADS
RET_DV = RET_WIDTH // RET_HEADS
RET_QK = RET_HEADS * RET_DK
RET_CHUNK = 128
ROPE_BASE = 10000.0
CONV_K = 31
CONV_BUF = CONV_K - 1
D_FF = ((8 * D_MODEL // 3 + 127) // 128) * 128
N_MOD = 9
EPS = 1e-6
IN_COLS = 2 * RET_QK + 2 * RET_WIDTH + 2 * CONV_WIDTH
SPLITS = [RET_QK, 2 * RET_QK, 2 * RET_QK + RET_WIDTH, 2 * RET_QK + 2 * RET_WIDTH,
          2 * RET_QK + 2 * RET_WIDTH + CONV_WIDTH]

kernel_name = 'hybrid_retention_conformer_conv_macaron_adaln_step'


def rmsnorm(x, g):
    xf = x.astype(jnp.float32)
    y = xf * lax.rsqrt(jnp.mean(xf * xf, axis=-1, keepdims=True) + EPS)
    return (y * g.astype(jnp.float32)).astype(x.dtype)


def swiglu(h, w_gate, w_up, w_down):
    return (jax.nn.silu(h @ w_gate) * (h @ w_up)) @ w_down


def rotary(x, pos):
    half = x.shape[-1] // 2
    inv = ROPE_BASE ** (-jnp.arange(half, dtype=jnp.float32) / half)
    ang = pos[:, None] * inv[None, :]
    cos = jnp.cos(ang)[None, :, None, :]
    sin = jnp.sin(ang)[None, :, None, :]
    x1, x2 = x[..., :half], x[..., half:]
    return jnp.concatenate([x1 * cos - x2 * sin, x1 * sin + x2 * cos], axis=-1)


def log_gammas():
    return jnp.log(1.0 - 2.0 ** (-5.0 - jnp.arange(RET_HEADS, dtype=jnp.float32)))


def retention_chunk(S, qkv):
    q, k, v = qkv
    C = q.shape[1]
    lg = log_gammas()
    idx = jnp.arange(C, dtype=jnp.float32)
    diff = idx[:, None] - idx[None, :]
    dmask = jnp.where(diff[None] >= 0, jnp.exp(lg[:, None, None] * jnp.maximum(diff, 0.0)[None]), 0.0)
    scores = jnp.einsum('bihd,bjhd->bhij', q, k) * dmask[None]
    inner = jnp.einsum('bhij,bjhe->bihe', scores, v)
    decay_q = jnp.exp(lg[None, :] * (idx[:, None] + 1.0))
    cross = jnp.einsum('bihd,bhde->bihe', q, S) * decay_q[None, :, :, None]
    decay_k = jnp.exp(lg[None, :] * (C - 1.0 - idx[:, None]))
    S_new = jnp.exp(lg * C)[None, :, None, None] * S + jnp.einsum(
        'bjhd,bjhe->bhde', k * decay_k[None, :, :, None], v)
    return S_new, inner + cross


def retention(q, k, v, S0):
    B, T, H, _ = q.shape
    C = RET_CHUNK if T % RET_CHUNK == 0 else T
    n = T // C
    to_chunks = lambda a: a.reshape(B, n, C, H, a.shape[-1]).transpose(1, 0, 2, 3, 4)
    S, out = lax.scan(retention_chunk, S0, (to_chunks(q), to_chunks(k), to_chunks(v)))
    return S, out.transpose(1, 0, 2, 3, 4).reshape(B, T, H, RET_DV)


def conv_module(a, b, buf, dw_w, dw_b, ln_g, ln_b):
    u = a * jax.nn.sigmoid(b)
    up = jnp.concatenate([buf.astype(u.dtype), u], axis=1)
    y = lax.conv_general_dilated(up, dw_w[:, None, :].astype(u.dtype), window_strides=(1,),
                                 padding='VALID', dimension_numbers=('NWC', 'WIO', 'NWC'),
                                 feature_group_count=CONV_WIDTH) + dw_b
    yf = y.astype(jnp.float32)
    mu = jnp.mean(yf, axis=-1, keepdims=True)
    var = jnp.mean(jnp.square(yf - mu), axis=-1, keepdims=True)
    yn = (yf - mu) * lax.rsqrt(var + EPS) * ln_g.astype(jnp.float32) + ln_b.astype(jnp.float32)
    return jax.nn.silu(yn).astype(a.dtype), up[:, -CONV_BUF:]


def decoder_layer(x, c, pos0, S0, buf0, norm_ffn1, ffn1_w_gate, ffn1_w_up, ffn1_w_down,
                  norm_mix, w_in, ret_gn_gain, dw_w, dw_b, conv_ln_gain, conv_ln_bias, w_out,
                  norm_ffn2, ffn2_w_gate, ffn2_w_up, ffn2_w_down, w_ada, b_ada):
    B, T, _ = x.shape
    ada = (jax.nn.silu(c) @ w_ada + b_ada).reshape(B, N_MOD, D_MODEL)[:, :, None, :]
    sh1, sc1, g1, sh2, sc2, g2, sh3, sc3, g3 = [ada[:, i] for i in range(N_MOD)]
    h = rmsnorm(x, norm_ffn1) * (1.0 + sc1) + sh1
    x = x + 0.5 * g1 * swiglu(h, ffn1_w_gate, ffn1_w_up, ffn1_w_down)
    h = rmsnorm(x, norm_mix) * (1.0 + sc2) + sh2
    q, k, v, g, a, b = jnp.split(h @ w_in, SPLITS, axis=-1)
    pos = (pos0 + jnp.arange(T, dtype=jnp.int32)).astype(jnp.float32)
    qh = rotary(q.reshape(B, T, RET_HEADS, RET_DK).astype(jnp.float32), pos)
    kh = rotary(k.reshape(B, T, RET_HEADS, RET_DK).astype(jnp.float32), pos) * (RET_DK ** -0.5)
    vh = v.reshape(B, T, RET_HEADS, RET_DV).astype(jnp.float32)
    S_new, o = retention(qh, kh, vh, S0.astype(jnp.float32))
    mu = jnp.mean(o, axis=-1, keepdims=True)
    var = jnp.mean(jnp.square(o - mu), axis=-1, keepdims=True)
    o = ((o - mu) * lax.rsqrt(var + EPS)).reshape(B, T, RET_WIDTH) * ret_gn_gain.astype(jnp.float32)
    ret_out = (jax.nn.silu(g.astype(jnp.float32)) * o).astype(x.dtype)
    conv_out, buf_new = conv_module(a, b, buf0, dw_w, dw_b, conv_ln_gain, conv_ln_bias)
    x = x + g2 * (jnp.concatenate([ret_out, conv_out], axis=-1) @ w_out)
    h = rmsnorm(x, norm_ffn2) * (1.0 + sc3) + sh3
    x = x + 0.5 * g3 * swiglu(h, ffn2_w_gate, ffn2_w_up, ffn2_w_down)
    return x, S_new, buf_new


def setup_inputs(seed: int = 0) -> dict:
    key = jax.random.key(seed)
    ks = jax.random.split(key, 26)
    f32 = jnp.float32
    nrm = lambda k, shape, s: jax.random.normal(k, shape, f32) * s
    gain = lambda k, shape: 1.0 + 0.02 * jax.random.normal(k, shape, f32)
    L = DEPTH
    return {
        'x_prompt': nrm(ks[0], (BATCH, SEQ, D_MODEL), 1.0),
        'x_sample': nrm(ks[1], (DEC_BATCH, DEC_SEQ, D_MODEL), 1.0),
        'c_prompt': nrm(ks[2], (BATCH, D_MODEL), 1.0),
        'c_sample': nrm(ks[3], (DEC_BATCH, D_MODEL), 1.0),
        'state_ret': nrm(ks[4], (L, DEC_BATCH, RET_HEADS, RET_DK, RET_DV), 0.5),
        'state_conv': nrm(ks[5], (L, DEC_BATCH, CONV_BUF, CONV_WIDTH), 0.5),
        'norm_ffn1': gain(ks[6], (L, D_MODEL)),
        'ffn1_w_gate': nrm(ks[7], (L, D_MODEL, D_FF), D_MODEL ** -0.5),
        'ffn1_w_up': nrm(ks[8], (L, D_MODEL, D_FF), D_MODEL ** -0.5),
        'ffn1_w_down': nrm(ks[9], (L, D_FF, D_MODEL), D_FF ** -0.5),
        'norm_mix': gain(ks[10], (L, D_MODEL)),
        'w_in': nrm(ks[11], (L, D_MODEL, IN_COLS), D_MODEL ** -0.5),
        'ret_gn_gain': gain(ks[12], (L, RET_WIDTH)),
        'dw_w': nrm(ks[13], (L, CONV_K, CONV_WIDTH), CONV_K ** -0.5),
        'dw_b': nrm(ks[14], (L, CONV_WIDTH), 0.02),
        'conv_ln_gain': gain(ks[15], (L, CONV_WIDTH)),
        'conv_ln_bias': nrm(ks[16], (L, CONV_WIDTH), 0.02),
        'w_out': nrm(ks[17], (L, MIX_WIDTH, D_MODEL), MIX_WIDTH ** -0.5),
        'norm_ffn2': gain(ks[18], (L, D_MODEL)),
        'ffn2_w_gate': nrm(ks[19], (L, D_MODEL, D_FF), D_MODEL ** -0.5),
        'ffn2_w_up': nrm(ks[20], (L, D_MODEL, D_FF), D_MODEL ** -0.5),
        'ffn2_w_down': nrm(ks[21], (L, D_FF, D_MODEL), D_FF ** -0.5),
        'w_ada': nrm(ks[22], (L, D_MODEL, N_MOD * D_MODEL), 0.5 * D_MODEL ** -0.5),
        'b_ada': nrm(ks[23], (L, N_MOD * D_MODEL), 0.01),
        'norm_final': gain(ks[24], (D_MODEL,)),
    }


def reference(x_prompt, x_sample, c_prompt, c_sample, state_ret, state_conv,
              norm_ffn1, ffn1_w_gate, ffn1_w_up, ffn1_w_down, norm_mix, w_in, ret_gn_gain,
              dw_w, dw_b, conv_ln_gain, conv_ln_bias, w_out, norm_ffn2, ffn2_w_gate,
              ffn2_w_up, ffn2_w_down, w_ada, b_ada, norm_final):
    yp, ys = x_prompt, x_sample
    Bp = x_prompt.shape[0]
    ret_p, conv_p, ret_s, conv_s = [], [], [], []
    for l in range(DEPTH):
        lw = (norm_ffn1[l], ffn1_w_gate[l], ffn1_w_up[l], ffn1_w_down[l], norm_mix[l], w_in[l],
              ret_gn_gain[l], dw_w[l], dw_b[l], conv_ln_gain[l], conv_ln_bias[l], w_out[l],
              norm_ffn2[l], ffn2_w_gate[l], ffn2_w_up[l], ffn2_w_down[l], w_ada[l], b_ada[l])
        S0p = jnp.zeros((Bp, RET_HEADS, RET_DK, RET_DV), jnp.float32)
        buf0p = jnp.zeros((Bp, CONV_BUF, CONV_WIDTH), x_prompt.dtype)
        yp, Sp, bp = decoder_layer(yp, c_prompt, 0, S0p, buf0p, *lw)
        ys, Ss, bs = decoder_layer(ys, c_sample, PAST_LEN, state_ret[l], state_conv[l], *lw)
        ret_p.append(Sp)
        conv_p.append(bp)
        ret_s.append(Ss)
        conv_s.append(bs)
    y_prompt = rmsnorm(yp, norm_final)
    y_sample = rmsnorm(ys, norm_final)
    ret_state_prompt = jnp.stack(ret_p).astype(state_ret.dtype)
    conv_state_prompt = jnp.stack(conv_p).astype(state_conv.dtype)
    ret_state_sample = jnp.stack(ret_s).astype(state_ret.dtype)
    conv_state_sample = jnp.stack(conv_s).astype(state_conv.dtype)
    return (y_prompt, y_sample, ret_state_prompt, conv_state_prompt, ret_state_sample, conv_state_sample)
```

```python
import functools
import math

import jax
import jax.numpy as jnp
from jax import lax
from jax.experimental import pallas as pl
from jax.experimental.pallas import tpu as pltpu

F32 = jnp.float32
BF16 = jnp.bfloat16

PAST_LEN = 16384
RET_CHUNK = 128
ROPE_BASE = 10000.0
EPS = 1e-6
N_MOD = 9

SUBLANES = 8
LANES = 128
VMEM_LIMIT_BYTES = 56 * 1024 * 1024

PROMPT_TOKEN_TILE = 512
SAMPLE_SEQ_TILE = 64
SAMPLE_MIX_SEQ_TILE = 16
FF_CHUNK = 512


def _resident(shape):
    n = len(shape)
    return pl.BlockSpec(shape, lambda *_: (0,) * n, pipeline_mode=pl.Buffered(1))


def _rmsnorm(x, gain):
    ms = jnp.mean(x * x, axis=-1, keepdims=True)
    return x * lax.rsqrt(ms + EPS) * gain


def _dot(a, b):
    return jnp.dot(a, b, preferred_element_type=F32)


def _ada_kernel(c_ref, w_ref, b_ref, o_ref):
    c = c_ref[...]
    h = (c * jax.nn.sigmoid(c)).astype(BF16)
    o_ref[0] = _dot(h, w_ref[...].astype(BF16)) + b_ref[...]


def _ada(c_all, w_ada, b_ada):
    nb, d = c_all.shape
    return pl.pallas_call(
        _ada_kernel,
        grid=(N_MOD,),
        in_specs=[
            pl.BlockSpec((nb, d), lambda j: (0, 0)),
            pl.BlockSpec((d, d), lambda j: (0, j)),
            pl.BlockSpec((1, d), lambda j: (0, j)),
        ],
        out_specs=pl.BlockSpec((1, nb, d), lambda j: (j, 0, 0)),
        out_shape=jax.ShapeDtypeStruct((N_MOD, nb, d), F32),
        compiler_params=pltpu.CompilerParams(dimension_semantics=("arbitrary",),
                                             vmem_limit_bytes=VMEM_LIMIT_BYTES),
        name="ada",
    )(c_all, w_ada, b_ada.reshape(1, -1))


def _ffn_kernel(x_ref, mod_ref, ng_ref, wg_ref, wu_ref, wd_ref, *rest, final_norm):
    if final_norm:
        fg_ref, o_ref, act_ref = rest
    else:
        o_ref, act_ref = rest
    x = x_ref[...]
    g_, r_, d_ = x.shape
    shift, scale, gate = mod_ref[0], mod_ref[1], mod_ref[2]
    h = _rmsnorm(x, ng_ref[...]) * (1.0 + scale) + shift
    hb = h.astype(BF16).reshape(g_ * r_, d_)
    d_ff = wg_ref.shape[1]
    for c0 in range(0, d_ff, FF_CHUNK):
        sl = slice(c0, min(c0 + FF_CHUNK, d_ff))
        g = _dot(hb, wg_ref[:, sl])
        u = _dot(hb, wu_ref[:, sl])
        act_ref[:, sl] = (g * jax.nn.sigmoid(g) * u).astype(BF16)
    y = _dot(act_ref[...], wd_ref[...]).reshape(g_, r_, d_)
    out = x + 0.5 * gate * y
    if final_norm:
        out = _rmsnorm(out, fg_ref[...])
    o_ref[...] = out


def _ffn(x, mods, norm_gain, wg, wu, wd, final_gain, *, block, mod_block, x_index, mod_index, grid):
    d = x.shape[-1]
    d_ff = wg.shape[1]
    rows = block[0] * block[1]
    final_norm = final_gain is not None
    in_specs = [
        pl.BlockSpec(block, x_index),
        pl.BlockSpec(mod_block, mod_index),
        _resident((1, d)),
        _resident((d, d_ff)),
        _resident((d, d_ff)),
        _resident((d_ff, d)),
    ]
    args = [x, mods, norm_gain.reshape(1, d), wg, wu, wd]
    if final_norm:
        in_specs.append(_resident((1, d)))
        args.append(final_gain.reshape(1, d))
    return pl.pallas_call(
        functools.partial(_ffn_kernel, final_norm=final_norm),
        grid=grid,
        in_specs=in_specs,
        out_specs=pl.BlockSpec(block, x_index),
        out_shape=jax.ShapeDtypeStruct(x.shape, F32),
        scratch_shapes=[pltpu.VMEM((rows, d_ff), BF16)],
        compiler_params=pltpu.CompilerParams(dimension_semantics=("arbitrary",) * len(grid),
                                             vmem_limit_bytes=VMEM_LIMIT_BYTES),
        name="ffn_final" if final_norm else "ffn",
    )(*args)


def _rotate(xh, cos2, sin2):
    return xh * cos2 + pltpu.roll(xh, xh.shape[-1] // 2, axis=xh.ndim - 1) * sin2


def _group_norm_gate(o, gate_pre, gain):
    mu = jnp.mean(o, axis=-1, keepdims=True)
    ctr = o - mu
    var = jnp.mean(ctr * ctr, axis=-1, keepdims=True)
    return gate_pre * jax.nn.sigmoid(gate_pre) * (ctr * lax.rsqrt(var + EPS) * gain)


def _layer_norm_silu(y, gain, bias):
    mu = jnp.mean(y, axis=-1, keepdims=True)
    ctr = y - mu
    var = jnp.mean(ctr * ctr, axis=-1, keepdims=True)
    yn = ctr * lax.rsqrt(var + EPS) * gain + bias
    return yn * jax.nn.sigmoid(yn)


def _conv_taps(win_ref, dww_ref, n_out, lead):
    n_taps = dww_ref.shape[0]
    ndim = len(win_ref.shape)
    pre = (slice(None),) * (ndim - 2)
    acc = None
    for phase in range(SUBLANES):
        offsets = [o for o in range(lead, lead + n_taps) if o % SUBLANES == phase]
        if not offsets:
            continue
        span = offsets[-1] - phase + n_out
        shifted = win_ref[pre + (slice(phase, phase + span), slice(None))]
        for o in offsets:
            a = o - phase
            w_row = dww_ref[o - lead:o - lead + 1, :]
            term = shifted[pre + (slice(a, a + n_out), slice(None))] * w_row
            acc = term if acc is None else acc + term
    return acc


def _mix_prompt_kernel(x_ref, mod_ref, ng_ref, win_ref, gn_ref, dww_ref, dwb_ref, lng_ref, lnb_ref,
                       wout_ref, cos_ref, sin_ref, dmask_ref, dq_ref, dk_ref, gc_ref,
                       o_ref, sout_ref, cout_ref, s_scr, u_scr, mix_scr, *, heads, hist):
    t_idx = pl.program_id(1)
    n_t = pl.num_programs(1)
    tt, d_ = x_ref.shape[1], x_ref.shape[2]
    ret_w = gn_ref.shape[1]
    dh = ret_w // heads
    conv_w = dww_ref.shape[1]
    n_taps = dww_ref.shape[0]

    @pl.when(t_idx == 0)
    def _():
        s_scr[...] = jnp.zeros_like(s_scr)
        u_scr[0:hist, :] = jnp.zeros((hist, conv_w), F32)

    x = x_ref[0]
    shift, scale, gate = mod_ref[0, 0], mod_ref[1, 0], mod_ref[2, 0]
    h = _rmsnorm(x, ng_ref[...]) * (1.0 + scale) + shift
    proj = _dot(h.astype(BF16), win_ref[...])

    cos2, sin2 = cos_ref[...], sin_ref[...]
    k_scale = dh ** -0.5
    o_q, o_k, o_v, o_g = 0, ret_w, 2 * ret_w, 3 * ret_w
    o_a, o_b = 4 * ret_w, 4 * ret_w + conv_w

    for hd in range(heads):
        lo = hd * dh
        q = _rotate(proj[:, o_q + lo:o_q + lo + dh], cos2, sin2)
        k = _rotate(proj[:, o_k + lo:o_k + lo + dh], cos2, sin2) * k_scale
        v = proj[:, o_v + lo:o_v + lo + dh]
        gate_pre = proj[:, o_g + lo:o_g + lo + dh]
        dmask, dq, dk, gc = dmask_ref[hd], dq_ref[hd], dk_ref[hd], gc_ref[hd]
        state = s_scr[hd]
        outs = []
        for c0 in range(0, tt, RET_CHUNK):
            qc, kc = q[c0:c0 + RET_CHUNK], k[c0:c0 + RET_CHUNK]
            vb = v[c0:c0 + RET_CHUNK].astype(BF16)
            scores = lax.dot_general(qc.astype(BF16), kc.astype(BF16), (((1,), (1,)), ((), ())),
                                     preferred_element_type=F32)
            inner = _dot((scores * dmask).astype(BF16), vb)
            cross = _dot((qc * dq).astype(BF16), state.astype(BF16))
            outs.append(inner + cross)
            state = gc * state + lax.dot_general((kc * dk).astype(BF16), vb, (((0,), (0,)), ((), ())),
                                                 preferred_element_type=F32)
        s_scr[hd] = state
        o = jnp.concatenate(outs, axis=0) if len(outs) > 1 else outs[0]
        mix_scr[:, lo:lo + dh] = _group_norm_gate(o, gate_pre, gn_ref[:, lo:lo + dh]).astype(BF16)

    a = proj[:, o_a:o_a + conv_w]
    b = proj[:, o_b:o_b + conv_w]
    u_scr[hist:hist + tt, :] = a * jax.nn.sigmoid(b)
    y = _conv_taps(u_scr, dww_ref, tt, hist - (n_taps - 1)) + dwb_ref[...]
    mix_scr[:, ret_w:ret_w + conv_w] = _layer_norm_silu(y, lng_ref[...], lnb_ref[...]).astype(BF16)
    u_scr[0:hist, :] = u_scr[tt:tt + hist, :]

    o_ref[0] = x + gate * _dot(mix_scr[...], wout_ref[...])

    @pl.when(t_idx == n_t - 1)
    def _():
        sout_ref[0] = s_scr[...]
        cout_ref[0] = u_scr[hist - (n_taps - 1):hist, :]


def _decay_tables(heads, chunk):
    lg = jnp.log(1.0 - 2.0 ** (-5.0 - jnp.arange(heads, dtype=F32)))
    idx = jnp.arange(chunk, dtype=F32)
    diff = idx[:, None] - idx[None, :]
    dmask = jnp.where(diff[None] >= 0, jnp.exp(lg[:, None, None] * jnp.maximum(diff, 0.0)[None]), 0.0)
    dq = jnp.exp(lg[:, None] * (idx[None, :] + 1.0))
    dk = jnp.exp(lg[:, None] * (chunk - 1.0 - idx[None, :]))
    gc = jnp.exp(lg * chunk)
    return dmask, dq, dk, gc


def _rotary_tables(pos0, n_pos, half):
    inv = ROPE_BASE ** (-jnp.arange(half, dtype=F32) / half)
    pos = (pos0 + jnp.arange(n_pos, dtype=jnp.int32)).astype(F32)
    ang = pos[:, None] * inv[None, :]
    cos, sin = jnp.cos(ang), jnp.sin(ang)
    return jnp.concatenate([cos, cos], axis=-1), jnp.concatenate([-sin, sin], axis=-1)


def _mix_prompt(x, mods, norm_gain, w_in, gn_gain, dw_w, dw_b, ln_g, ln_b, w_out, heads):
    nb, seq, d = x.shape
    tt = PROMPT_TOKEN_TILE
    ret_w = gn_gain.shape[0]
    dh = ret_w // heads
    n_taps, conv_w = dw_w.shape
    hist = -(-(n_taps - 1) // SUBLANES) * SUBLANES
    chunk = RET_CHUNK
    cos2, sin2 = _rotary_tables(0, seq, dh // 2)
    dmask, dq, dk, gc = _decay_tables(heads, chunk)
    dq_b = jnp.broadcast_to(dq[:, :, None], (heads, chunk, dh))
    dk_b = jnp.broadcast_to(dk[:, :, None], (heads, chunk, dh))
    gc_b = jnp.broadcast_to(gc[:, None, None], (heads, 1, dh))
    kern = functools.partial(_mix_prompt_kernel, heads=heads, hist=hist)
    return pl.pallas_call(
        kern,
        grid=(nb, seq // tt),
        in_specs=[
            pl.BlockSpec((1, tt, d), lambda b, t: (b, t, 0)),
            pl.BlockSpec((3, 1, 1, d), lambda b, t: (0, b, 0, 0)),
            _resident((1, d)),
            _resident(w_in.shape),
            _resident((1, ret_w)),
            _resident((n_taps, conv_w)),
            _resident((1, conv_w)),
            _resident((1, conv_w)),
            _resident((1, conv_w)),
            _resident(w_out.shape),
            pl.BlockSpec((tt, dh), lambda b, t: (t, 0)),
            pl.BlockSpec((tt, dh), lambda b, t: (t, 0)),
            _resident((heads, chunk, chunk)),
            _resident((heads, chunk, dh)),
            _resident((heads, chunk, dh)),
            _resident((heads, 1, dh)),
        ],
        out_specs=[
            pl.BlockSpec((1, tt, d), lambda b, t: (b, t, 0)),
            pl.BlockSpec((1, heads, dh, dh), lambda b, t: (b, 0, 0, 0)),
            pl.BlockSpec((1, n_taps - 1, conv_w), lambda b, t: (b, 0, 0)),
        ],
        out_shape=[
            jax.ShapeDtypeStruct(x.shape, F32),
            jax.ShapeDtypeStruct((nb, heads, dh, dh), F32),
            jax.ShapeDtypeStruct((nb, n_taps - 1, conv_w), F32),
        ],
        scratch_shapes=[
            pltpu.VMEM((heads, dh, dh), F32),
            pltpu.VMEM((hist + tt, conv_w), F32),
            pltpu.VMEM((tt, ret_w + conv_w), BF16),
        ],
        compiler_params=pltpu.CompilerParams(dimension_semantics=("arbitrary", "arbitrary"),
                                             vmem_limit_bytes=VMEM_LIMIT_BYTES),
        name="mix_prompt",
    )(x, mods, norm_gain.reshape(1, d), w_in, gn_gain.reshape(1, -1), dw_w, dw_b.reshape(1, -1),
      ln_g.reshape(1, -1), ln_b.reshape(1, -1), w_out, cos2, sin2, dmask, dq_b, dk_b, gc_b)


def _mix_sample_kernel(x_ref, mod_ref, ng_ref, win_ref, gn_ref, dww_ref, dwb_ref, lng_ref, lnb_ref,
                       wout_ref, cos_ref, sin_ref, dmask_ref, dq_ref, dk_ref, gc_ref, sin_state_ref,
                       cin_ref, o_ref, sout_ref, cout_ref, qd_scr, kd_scr, v_scr, cross_scr, win_scr,
                       *, heads, n_tok):
    tb, rows, d_ = x_ref.shape
    ret_w = gn_ref.shape[1]
    dh = ret_w // heads
    n_taps, conv_w = dww_ref.shape
    n_buf = n_taps - 1

    x = x_ref[...]
    shift, scale, gate = mod_ref[0], mod_ref[1], mod_ref[2]
    h = _rmsnorm(x, ng_ref[...]) * (1.0 + scale) + shift
    proj = _dot(h.astype(BF16).reshape(tb * rows, d_), win_ref[...]).reshape(tb, rows, -1)

    cos2, sin2 = cos_ref[...], sin_ref[...]
    k_scale = dh ** -0.5
    o_q, o_k, o_v, o_g = 0, ret_w, 2 * ret_w, 3 * ret_w
    o_a, o_b = 4 * ret_w, 4 * ret_w + conv_w

    inner_heads = []
    for hd in range(heads):
        lo = hd * dh
        q = _rotate(proj[:, :, o_q + lo:o_q + lo + dh], cos2, sin2)
        k = _rotate(proj[:, :, o_k + lo:o_k + lo + dh], cos2, sin2) * k_scale
        v = proj[:, :, o_v + lo:o_v + lo + dh]
        qd_scr[:, :, lo:lo + dh] = q * dq_ref[hd]
        kd_scr[:, :, lo:lo + dh] = k * dk_ref[hd]
        v_scr[:, :, lo:lo + dh] = v
        row_id = lax.broadcasted_iota(jnp.int32, (tb, rows, dh), 1)
        inner = jnp.zeros((tb, rows, dh), F32)
        for i in range(n_tok):
            s_i = jnp.sum(q[:, i:i + 1, :] * k, axis=-1, keepdims=True)
            p_i = s_i * dmask_ref[hd, i]
            o_i = jnp.sum(p_i * v, axis=1, keepdims=True)
            inner = jnp.where(row_id == i, o_i, inner)
        inner_heads.append(inner)

    zeros_pad = jnp.zeros((dh - rows, dh), F32)

    def per_sequence(b, carry):
        for hd in range(heads):
            lo = hd * dh
            state = sin_state_ref[b, hd]
            cross_scr[b, :, lo:lo + dh] = _dot(qd_scr[b, :, lo:lo + dh].astype(BF16), state.astype(BF16))
            k_sq = jnp.concatenate([kd_scr[b, :, lo:lo + dh], zeros_pad], axis=0).astype(BF16)
            v_sq = jnp.concatenate([v_scr[b, :, lo:lo + dh], zeros_pad], axis=0).astype(BF16)
            upd = lax.dot_general(k_sq, v_sq, (((0,), (0,)), ((), ())), preferred_element_type=F32)
            sout_ref[b, hd] = gc_ref[hd] * state + upd
        return carry

    lax.fori_loop(0, tb, per_sequence, 0)

    mixed = []
    for hd in range(heads):
        lo = hd * dh
        o = inner_heads[hd] + cross_scr[:, :, lo:lo + dh]
        gate_pre = proj[:, :, o_g + lo:o_g + lo + dh]
        mixed.append(_group_norm_gate(o, gate_pre, gn_ref[:, lo:lo + dh]).astype(BF16))

    a = proj[:, :, o_a:o_a + conv_w]
    b_ = proj[:, :, o_b:o_b + conv_w]
    win_scr[:, 0:n_buf, :] = cin_ref[...]
    win_scr[:, n_buf:n_buf + rows, :] = a * jax.nn.sigmoid(b_)
    y = _conv_taps(win_scr, dww_ref, rows, 0) + dwb_ref[...]
    mixed.append(_layer_norm_silu(y, lng_ref[...], lnb_ref[...]).astype(BF16))
    cout_ref[...] = win_scr[:, n_tok:n_tok + n_buf, :]

    mixed = jnp.concatenate(mixed, axis=-1).reshape(tb * rows, ret_w + conv_w)
    o_ref[...] = x + gate * _dot(mixed, wout_ref[...]).reshape(tb, rows, d_)


def _mix_sample(x, mods, norm_gain, w_in, gn_gain, dw_w, dw_b, ln_g, ln_b, w_out, state_ret, state_conv,
                n_tok):
    nb, rows, d = x.shape
    tb = SAMPLE_MIX_SEQ_TILE
    heads, dh = state_ret.shape[1], state_ret.shape[2]
    ret_w = heads * dh
    n_taps, conv_w = dw_w.shape
    n_buf = n_taps - 1
    win_rows = -(-(n_buf + rows + SUBLANES) // SUBLANES) * SUBLANES
    cos2, sin2 = _rotary_tables(PAST_LEN, rows, dh // 2)
    dmask, dq, dk, gc = _decay_tables(heads, n_tok)
    pad = rows - n_tok
    dmask_b = jnp.broadcast_to(jnp.pad(dmask, ((0, 0), (0, 0), (0, pad)))[:, :, :, None],
                               (heads, n_tok, rows, dh))
    dq_b = jnp.broadcast_to(jnp.pad(dq, ((0, 0), (0, pad)))[:, :, None], (heads, rows, dh))
    dk_b = jnp.broadcast_to(jnp.pad(dk, ((0, 0), (0, pad)))[:, :, None], (heads, rows, dh))
    gc_b = jnp.broadcast_to(gc[:, None, None], (heads, 1, dh))
    kern = functools.partial(_mix_sample_kernel, heads=heads, n_tok=n_tok)
    return pl.pallas_call(
        kern,
        grid=(nb // tb,),
        in_specs=[
            pl.BlockSpec((tb, rows, d), lambda i: (i, 0, 0)),
            pl.BlockSpec((3, tb, 1, d), lambda i: (0, i, 0, 0)),
            _resident((1, d)),
            _resident(w_in.shape),
            _resident((1, ret_w)),
            _resident((n_taps, conv_w)),
            _resident((1, conv_w)),
            _resident((1, conv_w)),
            _resident((1, conv_w)),
            _resident(w_out.shape),
            _resident((rows, dh)),
            _resident((rows, dh)),
            _resident((heads, n_tok, rows, dh)),
            _resident((heads, rows, dh)),
            _resident((heads, rows, dh)),
            _resident((heads, 1, dh)),
            pl.BlockSpec((tb, heads, dh, dh), lambda i: (i, 0, 0, 0)),
            pl.BlockSpec((tb, n_buf, conv_w), lambda i: (i, 0, 0)),
        ],
        out_specs=[
            pl.BlockSpec((tb, rows, d), lambda i: (i, 0, 0)),
            pl.BlockSpec((tb, heads, dh, dh), lambda i: (i, 0, 0, 0)),
            pl.BlockSpec((tb, n_buf, conv_w), lambda i: (i, 0, 0)),
        ],
        out_shape=[
            jax.ShapeDtypeStruct(x.shape, F32),
            jax.ShapeDtypeStruct(state_ret.shape, F32),
            jax.ShapeDtypeStruct(state_conv.shape, F32),
        ],
        scratch_shapes=[
            pltpu.VMEM((tb, rows, ret_w), F32),
            pltpu.VMEM((tb, rows, ret_w), F32),
            pltpu.VMEM((tb, rows, ret_w), F32),
            pltpu.VMEM((tb, rows, ret_w), F32),
            pltpu.VMEM((tb, win_rows, conv_w), F32),
        ],
        compiler_params=pltpu.CompilerParams(dimension_semantics=("arbitrary",),
                                             vmem_limit_bytes=VMEM_LIMIT_BYTES),
        name="mix_sample",
    )(x, mods, norm_gain.reshape(1, d), w_in, gn_gain.reshape(1, -1), dw_w, dw_b.reshape(1, -1),
      ln_g.reshape(1, -1), ln_b.reshape(1, -1), w_out, cos2, sin2, dmask_b, dq_b, dk_b, gc_b,
      state_ret, state_conv)


def _layer(xp, xs, mods_p, mods_s, sret, sconv, lw, final_gain, n_tok):
    (norm_ffn1, w1g, w1u, w1d, norm_mix, w_in, gn_gain, dw_w, dw_b, ln_g, ln_b, w_out,
     norm_ffn2, w2g, w2u, w2d) = lw
    nb, seq, d = xp.shape
    ns, rows, _ = xs.shape
    heads = sret.shape[1]
    tm, ts = PROMPT_TOKEN_TILE, SAMPLE_SEQ_TILE

    ffn_p = functools.partial(_ffn, block=(1, tm, d), mod_block=(3, 1, 1, d), grid=(nb, seq // tm),
                              x_index=lambda b, t: (b, t, 0), mod_index=lambda b, t: (0, b, 0, 0))
    ffn_s = functools.partial(_ffn, block=(ts, rows, d), mod_block=(3, ts, 1, d), grid=(ns // ts,),
                              x_index=lambda i: (i, 0, 0), mod_index=lambda i: (0, i, 0, 0))

    xp = ffn_p(xp, mods_p[0:3], norm_ffn1, w1g, w1u, w1d, None)
    xs = ffn_s(xs, mods_s[0:3], norm_ffn1, w1g, w1u, w1d, None)
    xp, ret_p, conv_p = _mix_prompt(xp, mods_p[3:6], norm_mix, w_in, gn_gain, dw_w, dw_b, ln_g, ln_b,
                                    w_out, heads)
    xs, ret_s, conv_s = _mix_sample(xs, mods_s[3:6], norm_mix, w_in, gn_gain, dw_w, dw_b, ln_g, ln_b,
                                    w_out, sret, sconv, n_tok)
    xp = ffn_p(xp, mods_p[6:9], norm_ffn2, w2g, w2u, w2d, final_gain)
    xs = ffn_s(xs, mods_s[6:9], norm_ffn2, w2g, w2u, w2d, final_gain)
    return xp, xs, ret_p, conv_p, ret_s, conv_s


def kernel(x_prompt, x_sample, c_prompt, c_sample, state_ret, state_conv, norm_ffn1, ffn1_w_gate,
           ffn1_w_up, ffn1_w_down, norm_mix, w_in, ret_gn_gain, dw_w, dw_b, conv_ln_gain, conv_ln_bias,
           w_out, norm_ffn2, ffn2_w_gate, ffn2_w_up, ffn2_w_down, w_ada, b_ada, norm_final):
    depth = w_in.shape[0]
    nb = x_prompt.shape[0]
    ns, n_tok, d = x_sample.shape
    assert n_tok <= SUBLANES and x_prompt.shape[1] % RET_CHUNK == 0

    xp = x_prompt
    xs = jnp.pad(x_sample, ((0, 0), (0, SUBLANES - n_tok), (0, 0)))
    c_all = jnp.concatenate([c_prompt, c_sample], axis=0)

    ret_p, conv_p, ret_s, conv_s = [], [], [], []
    for l in range(depth):
        ada = _ada(c_all, w_ada[l], b_ada[l])
        mods_p = ada[:, :nb].reshape(N_MOD, nb, 1, d)
        mods_s = ada[:, nb:].reshape(N_MOD, ns, 1, d)
        lw = (norm_ffn1[l], ffn1_w_gate[l].astype(BF16), ffn1_w_up[l].astype(BF16),
              ffn1_w_down[l].astype(BF16), norm_mix[l], w_in[l].astype(BF16), ret_gn_gain[l], dw_w[l],
              dw_b[l], conv_ln_gain[l], conv_ln_bias[l], w_out[l].astype(BF16), norm_ffn2[l],
              ffn2_w_gate[l].astype(BF16), ffn2_w_up[l].astype(BF16), ffn2_w_down[l].astype(BF16))
        final_gain = norm_final if l == depth - 1 else None
        xp, xs, rp, cp, rs, cs = _layer(xp, xs, mods_p, mods_s, state_ret[l], state_conv[l], lw,
                                        final_gain, n_tok)
        ret_p.append(rp)
        conv_p.append(cp)
        ret_s.append(rs)
        conv_s.append(cs)

    return (xp, xs[:, :n_tok], jnp.stack(ret_p), jnp.stack(conv_p), jnp.stack(ret_s), jnp.stack(conv_s))
```

```python
import functools
import math

import jax
import jax.numpy as jnp
from jax import lax
from jax.experimental import pallas as pl
from jax.experimental.pallas import tpu as pltpu

F32 = jnp.float32
BF16 = jnp.bfloat16

PAST_LEN = 16384
RET_CHUNK = 128
ROPE_BASE = 10000.0
EPS = 1e-6
N_MOD = 9

SUBLANES = 8
LANES = 128
VMEM_LIMIT_BYTES = 56 * 1024 * 1024

PROMPT_TOKEN_TILE = 512
SAMPLE_SEQ_TILE = 64
SAMPLE_MIX_SEQ_TILE = 16
FF_CHUNK = 512


def _resident(shape):
    n = len(shape)
    return pl.BlockSpec(shape, lambda *_: (0,) * n, pipeline_mode=pl.Buffered(1))


def _rmsnorm(x, gain):
    ms = jnp.mean(x * x, axis=-1, keepdims=True)
    return x * lax.rsqrt(ms + EPS) * gain


def _dot(a, b):
    return jnp.dot(a, b, preferred_element_type=F32)


def _ada_kernel(c_ref, w_ref, b_ref, o_ref):
    c = c_ref[...]
    h = (c * jax.nn.sigmoid(c)).astype(BF16)
    o_ref[0] = _dot(h, w_ref[...].astype(BF16)) + b_ref[...]


def _ada(c_all, w_ada, b_ada):
    nb, d = c_all.shape
    return pl.pallas_call(
        _ada_kernel,
        grid=(N_MOD,),
        in_specs=[
            pl.BlockSpec((nb, d), lambda j: (0, 0)),
            pl.BlockSpec((d, d), lambda j: (0, j)),
            pl.BlockSpec((1, d), lambda j: (0, j)),
        ],
        out_specs=pl.BlockSpec((1, nb, d), lambda j: (j, 0, 0)),
        out_shape=jax.ShapeDtypeStruct((N_MOD, nb, d), F32),
        compiler_params=pltpu.CompilerParams(dimension_semantics=("arbitrary",),
                                             vmem_limit_bytes=VMEM_LIMIT_BYTES),
        name="ada",
    )(c_all, w_ada, b_ada.reshape(1, -1))


def _ffn_kernel(x_ref, mod_ref, ng_ref, wg_ref, wu_ref, wd_ref, *rest, final_norm):
    if final_norm:
        fg_ref, o_ref, act_ref = rest
    else:
        o_ref, act_ref = rest
    x = x_ref[...]
    g_, r_, d_ = x.shape
    shift, scale, gate = mod_ref[0], mod_ref[1], mod_ref[2]
    h = _rmsnorm(x, ng_ref[...]) * (1.0 + scale) + shift
    hb = h.astype(BF16).reshape(g_ * r_, d_)
    d_ff = wg_ref.shape[1]
    for c0 in range(0, d_ff, FF_CHUNK):
        sl = slice(c0, min(c0 + FF_CHUNK, d_ff))
        g = _dot(hb, wg_ref[:, sl])
        u = _dot(hb, wu_ref[:, sl])
        act_ref[:, sl] = (g * jax.nn.sigmoid(g) * u).astype(BF16)
    y = _dot(act_ref[...], wd_ref[...]).reshape(g_, r_, d_)
    out = x + 0.5 * gate * y
    if final_norm:
        out = _rmsnorm(out, fg_ref[...])
    o_ref[...] = out


def _ffn(x, mods, norm_gain, wg, wu, wd, final_gain, *, block, mod_block, x_index, mod_index, grid):
    d = x.shape[-1]
    d_ff = wg.shape[1]
    rows = block[0] * block[1]
    final_norm = final_gain is not None
    in_specs = [
        pl.BlockSpec(block, x_index),
        pl.BlockSpec(mod_block, mod_index),
        _resident((1, d)),
        _resident((d, d_ff)),
        _resident((d, d_ff)),
        _resident((d_ff, d)),
    ]
    args = [x, mods, norm_gain.reshape(1, d), wg, wu, wd]
    if final_norm:
        in_specs.append(_resident((1, d)))
        args.append(final_gain.reshape(1, d))
    return pl.pallas_call(
        functools.partial(_ffn_kernel, final_norm=final_norm),
        grid=grid,
        in_specs=in_specs,
        out_specs=pl.BlockSpec(block, x_index),
        out_shape=jax.ShapeDtypeStruct(x.shape, F32),
        scratch_shapes=[pltpu.VMEM((rows, d_ff), BF16)],
        compiler_params=pltpu.CompilerParams(dimension_semantics=("arbitrary",) * len(grid),
                                             vmem_limit_bytes=VMEM_LIMIT_BYTES),
        name="ffn_final" if final_norm else "ffn",
    )(*args)


def _rotate(xh, cos2, sin2):
    return xh * cos2 + pltpu.roll(xh, xh.shape[-1] // 2, axis=xh.ndim - 1) * sin2


def _group_norm_gate(o, gate_pre, gain):
    mu = jnp.mean(o, axis=-1, keepdims=True)
    ctr = o - mu
    var = jnp.mean(ctr * ctr, axis=-1, keepdims=True)
    return gate_pre * jax.nn.sigmoid(gate_pre) * (ctr * lax.rsqrt(var + EPS) * gain)


def _layer_norm_silu(y, gain, bias):
    mu = jnp.mean(y, axis=-1, keepdims=True)
    ctr = y - mu
    var = jnp.mean(ctr * ctr, axis=-1, keepdims=True)
    yn = ctr * lax.rsqrt(var + EPS) * gain + bias
    return yn * jax.nn.sigmoid(yn)


def _conv_taps(win_ref, dww_ref, n_out, lead):
    n_taps = dww_ref.shape[0]
    pre = (slice(None),) * (len(win_ref.shape) - 2)
    total = None
    for phase in range(SUBLANES):
        offsets = [o for o in range(lead, lead + n_taps) if o % SUBLANES == phase]
        if not offsets:
            continue
        rows = n_out if phase == 0 else n_out + SUBLANES
        phase_sum = None
        for o in offsets:
            a = o - phase
            term = win_ref[pre + (slice(a, a + rows), slice(None))] * dww_ref[o - lead:o - lead + 1, :]
            phase_sum = term if phase_sum is None else phase_sum + term
        if phase:
            phase_sum = phase_sum[pre + (slice(phase, phase + n_out), slice(None))]
        total = phase_sum if total is None else total + phase_sum
    return total


def _mix_prompt_kernel(x_ref, mod_ref, ng_ref, win_ref, gn_ref, dww_ref, dwb_ref, lng_ref, lnb_ref,
                       wout_ref, cos_ref, sin_ref, dmask_ref, dq_ref, dk_ref, gc_ref,
                       o_ref, sout_ref, cout_ref, s_scr, u_scr, mix_scr, *, heads, hist):
    t_idx = pl.program_id(1)
    n_t = pl.num_programs(1)
    tt, d_ = x_ref.shape[1], x_ref.shape[2]
    ret_w = gn_ref.shape[1]
    dh = ret_w // heads
    conv_w = dww_ref.shape[1]
    n_taps = dww_ref.shape[0]

    @pl.when(t_idx == 0)
    def _():
        s_scr[...] = jnp.zeros_like(s_scr)
        u_scr[0:hist, :] = jnp.zeros((hist, conv_w), F32)

    x = x_ref[0]
    shift, scale, gate = mod_ref[0, 0], mod_ref[1, 0], mod_ref[2, 0]
    h = _rmsnorm(x, ng_ref[...]) * (1.0 + scale) + shift
    hb = h.astype(BF16)

    cos2, sin2 = cos_ref[...], sin_ref[...]
    k_scale = dh ** -0.5
    o_q, o_k, o_v, o_g = 0, ret_w, 2 * ret_w, 3 * ret_w
    o_a, o_b = 4 * ret_w, 4 * ret_w + conv_w

    ab = _dot(hb, win_ref[:, o_a:o_b + conv_w])
    u_scr[hist:hist + tt, :] = ab[:, :conv_w] * jax.nn.sigmoid(ab[:, conv_w:])
    y = _conv_taps(u_scr, dww_ref, tt, hist - (n_taps - 1)) + dwb_ref[...]
    conv_out = _layer_norm_silu(y, lng_ref[...], lnb_ref[...]).astype(BF16)
    u_scr[0:hist, :] = u_scr[tt:tt + hist, :]

    proj = _dot(hb, win_ref[:, o_q:o_a])
    for hd in range(heads):
        lo = hd * dh
        q = _rotate(proj[:, o_q + lo:o_q + lo + dh], cos2, sin2)
        k = _rotate(proj[:, o_k + lo:o_k + lo + dh], cos2, sin2) * k_scale
        v = proj[:, o_v + lo:o_v + lo + dh]
        gate_pre = proj[:, o_g + lo:o_g + lo + dh]
        dmask, dq, dk, gc = dmask_ref[hd], dq_ref[hd], dk_ref[hd], gc_ref[hd]
        state = s_scr[hd]
        outs = []
        for c0 in range(0, tt, RET_CHUNK):
            qc, kc = q[c0:c0 + RET_CHUNK], k[c0:c0 + RET_CHUNK]
            vb = v[c0:c0 + RET_CHUNK].astype(BF16)
            scores = lax.dot_general(qc.astype(BF16), kc.astype(BF16), (((1,), (1,)), ((), ())),
                                     preferred_element_type=F32)
            inner = _dot((scores * dmask).astype(BF16), vb)
            cross = _dot((qc * dq).astype(BF16), state.astype(BF16))
            outs.append(inner + cross)
            state = gc * state + lax.dot_general((kc * dk).astype(BF16), vb, (((0,), (0,)), ((), ())),
                                                 preferred_element_type=F32)
        s_scr[hd] = state
        o = jnp.concatenate(outs, axis=0) if len(outs) > 1 else outs[0]
        mix_scr[:, lo:lo + dh] = _group_norm_gate(o, gate_pre, gn_ref[:, lo:lo + dh]).astype(BF16)

    mixed = _dot(mix_scr[...], wout_ref[0:ret_w, :]) + _dot(conv_out, wout_ref[ret_w:ret_w + conv_w, :])
    o_ref[0] = x + gate * mixed

    @pl.when(t_idx == n_t - 1)
    def _():
        sout_ref[0] = s_scr[...]
        cout_ref[0] = u_scr[hist - (n_taps - 1):hist, :]


def _decay_tables(heads, chunk):
    lg = jnp.log(1.0 - 2.0 ** (-5.0 - jnp.arange(heads, dtype=F32)))
    idx = jnp.arange(chunk, dtype=F32)
    diff = idx[:, None] - idx[None, :]
    dmask = jnp.where(diff[None] >= 0, jnp.exp(lg[:, None, None] * jnp.maximum(diff, 0.0)[None]), 0.0)
    dq = jnp.exp(lg[:, None] * (idx[None, :] + 1.0))
    dk = jnp.exp(lg[:, None] * (chunk - 1.0 - idx[None, :]))
    gc = jnp.exp(lg * chunk)
    return dmask, dq, dk, gc


def _rotary_tables(pos0, n_pos, half):
    inv = ROPE_BASE ** (-jnp.arange(half, dtype=F32) / half)
    pos = (pos0 + jnp.arange(n_pos, dtype=jnp.int32)).astype(F32)
    ang = pos[:, None] * inv[None, :]
    cos, sin = jnp.cos(ang), jnp.sin(ang)
    return jnp.concatenate([cos, cos], axis=-1), jnp.concatenate([-sin, sin], axis=-1)


def _mix_prompt(x, mods, norm_gain, w_in, gn_gain, dw_w, dw_b, ln_g, ln_b, w_out, heads):
    nb, seq, d = x.shape
    tt = PROMPT_TOKEN_TILE
    ret_w = gn_gain.shape[0]
    dh = ret_w // heads
    n_taps, conv_w = dw_w.shape
    hist = -(-(n_taps - 1) // SUBLANES) * SUBLANES
    chunk = RET_CHUNK
    cos2, sin2 = _rotary_tables(0, seq, dh // 2)
    dmask, dq, dk, gc = _decay_tables(heads, chunk)
    dq_b = jnp.broadcast_to(dq[:, :, None], (heads, chunk, dh))
    dk_b = jnp.broadcast_to(dk[:, :, None], (heads, chunk, dh))
    gc_b = jnp.broadcast_to(gc[:, None, None], (heads, 1, dh))
    kern = functools.partial(_mix_prompt_kernel, heads=heads, hist=hist)
    return pl.pallas_call(
        kern,
        grid=(nb, seq // tt),
        in_specs=[
            pl.BlockSpec((1, tt, d), lambda b, t: (b, t, 0)),
            pl.BlockSpec((3, 1, 1, d), lambda b, t: (0, b, 0, 0)),
            _resident((1, d)),
            _resident(w_in.shape),
            _resident((1, ret_w)),
            _resident((n_taps, conv_w)),
            _resident((1, conv_w)),
            _resident((1, conv_w)),
            _resident((1, conv_w)),
            _resident(w_out.shape),
            pl.BlockSpec((tt, dh), lambda b, t: (t, 0)),
            pl.BlockSpec((tt, dh), lambda b, t: (t, 0)),
            _resident((heads, chunk, chunk)),
            _resident((heads, chunk, dh)),
            _resident((heads, chunk, dh)),
            _resident((heads, 1, dh)),
        ],
        out_specs=[
            pl.BlockSpec((1, tt, d), lambda b, t: (b, t, 0)),
            pl.BlockSpec((1, heads, dh, dh), lambda b, t: (b, 0, 0, 0)),
            pl.BlockSpec((1, n_taps - 1, conv_w), lambda b, t: (b, 0, 0)),
        ],
        out_shape=[
            jax.ShapeDtypeStruct(x.shape, F32),
            jax.ShapeDtypeStruct((nb, heads, dh, dh), F32),
            jax.ShapeDtypeStruct((nb, n_taps - 1, conv_w), F32),
        ],
        scratch_shapes=[
            pltpu.VMEM((heads, dh, dh), F32),
            pltpu.VMEM((hist + tt, conv_w), F32),
            pltpu.VMEM((tt, ret_w), BF16),
        ],
        compiler_params=pltpu.CompilerParams(dimension_semantics=("arbitrary", "arbitrary"),
                                             vmem_limit_bytes=VMEM_LIMIT_BYTES),
        name="mix_prompt",
    )(x, mods, norm_gain.reshape(1, d), w_in, gn_gain.reshape(1, -1), dw_w, dw_b.reshape(1, -1),
      ln_g.reshape(1, -1), ln_b.reshape(1, -1), w_out, cos2, sin2, dmask, dq_b, dk_b, gc_b)


def _mix_sample_kernel(x_ref, mod_ref, ng_ref, win_ref, gn_ref, dww_ref, dwb_ref, lng_ref, lnb_ref,
                       wout_ref, cos_ref, sin_ref, dmask_ref, dq_ref, dk_ref, gc_ref, sin_state_ref,
                       cin_ref, o_ref, sout_ref, cout_ref, qd_scr, kd_scr, v_scr, cross_scr, win_scr,
                       *, heads, n_tok):
    tb, rows, d_ = x_ref.shape
    ret_w = gn_ref.shape[1]
    dh = ret_w // heads
    n_taps, conv_w = dww_ref.shape
    n_buf = n_taps - 1

    x = x_ref[...]
    shift, scale, gate = mod_ref[0], mod_ref[1], mod_ref[2]
    h = _rmsnorm(x, ng_ref[...]) * (1.0 + scale) + shift
    proj = _dot(h.astype(BF16).reshape(tb * rows, d_), win_ref[...]).reshape(tb, rows, -1)

    cos2, sin2 = cos_ref[...], sin_ref[...]
    k_scale = dh ** -0.5
    o_q, o_k, o_v, o_g = 0, ret_w, 2 * ret_w, 3 * ret_w
    o_a, o_b = 4 * ret_w, 4 * ret_w + conv_w

    inner_heads = []
    for hd in range(heads):
        lo = hd * dh
        q = _rotate(proj[:, :, o_q + lo:o_q + lo + dh], cos2, sin2)
        k = _rotate(proj[:, :, o_k + lo:o_k + lo + dh], cos2, sin2) * k_scale
        v = proj[:, :, o_v + lo:o_v + lo + dh]
        qd_scr[:, :, lo:lo + dh] = q * dq_ref[hd]
        kd_scr[:, :, lo:lo + dh] = k * dk_ref[hd]
        v_scr[:, :, lo:lo + dh] = v
        row_id = lax.broadcasted_iota(jnp.int32, (tb, rows, dh), 1)
        inner = jnp.zeros((tb, rows, dh), F32)
        for i in range(n_tok):
            s_i = jnp.sum(q[:, i:i + 1, :] * k, axis=-1, keepdims=True)
            p_i = s_i * dmask_ref[hd, i]
            o_i = jnp.sum(p_i * v, axis=1, keepdims=True)
            inner = jnp.where(row_id == i, o_i, inner)
        inner_heads.append(inner)

    zeros_pad = jnp.zeros((dh - rows, dh), F32)

    def per_sequence(b, carry):
        for hd in range(heads):
            lo = hd * dh
            state = sin_state_ref[b, hd]
            cross_scr[b, :, lo:lo + dh] = _dot(qd_scr[b, :, lo:lo + dh].astype(BF16), state.astype(BF16))
            k_sq = jnp.concatenate([kd_scr[b, :, lo:lo + dh], zeros_pad], axis=0).astype(BF16)
            v_sq = jnp.concatenate([v_scr[b, :, lo:lo + dh], zeros_pad], axis=0).astype(BF16)
            upd = lax.dot_general(k_sq, v_sq, (((0,), (0,)), ((), ())), preferred_element_type=F32)
            sout_ref[b, hd] = gc_ref[hd] * state + upd
        return carry

    lax.fori_loop(0, tb, per_sequence, 0)

    mixed = []
    for hd in range(heads):
        lo = hd * dh
        o = inner_heads[hd] + cross_scr[:, :, lo:lo + dh]
        gate_pre = proj[:, :, o_g + lo:o_g + lo + dh]
        mixed.append(_group_norm_gate(o, gate_pre, gn_ref[:, lo:lo + dh]).astype(BF16))

    a = proj[:, :, o_a:o_a + conv_w]
    b_ = proj[:, :, o_b:o_b + conv_w]
    tail0 = (n_buf // SUBLANES) * SUBLANES
    win_scr[:, tail0:, :] = jnp.zeros((tb, win_scr.shape[1] - tail0, conv_w), F32)
    win_scr[:, 0:n_buf, :] = cin_ref[...]
    win_scr[:, n_buf:n_buf + rows, :] = a * jax.nn.sigmoid(b_)
    y = _conv_taps(win_scr, dww_ref, rows, 0) + dwb_ref[...]
    mixed.append(_layer_norm_silu(y, lng_ref[...], lnb_ref[...]).astype(BF16))
    cout_ref[...] = win_scr[:, n_tok:n_tok + n_buf, :]

    mixed = jnp.concatenate(mixed, axis=-1).reshape(tb * rows, ret_w + conv_w)
    o_ref[...] = x + gate * _dot(mixed, wout_ref[...]).reshape(tb, rows, d_)


def _mix_sample(x, mods, norm_gain, w_in, gn_gain, dw_w, dw_b, ln_g, ln_b, w_out, state_ret, state_conv,
                n_tok):
    nb, rows, d = x.shape
    tb = SAMPLE_MIX_SEQ_TILE
    heads, dh = state_ret.shape[1], state_ret.shape[2]
    ret_w = heads * dh
    n_taps, conv_w = dw_w.shape
    n_buf = n_taps - 1
    win_rows = -(-(n_buf + rows + SUBLANES) // SUBLANES) * SUBLANES
    cos2, sin2 = _rotary_tables(PAST_LEN, rows, dh // 2)
    dmask, dq, dk, gc = _decay_tables(heads, n_tok)
    pad = rows - n_tok
    dmask_b = jnp.broadcast_to(jnp.pad(dmask, ((0, 0), (0, 0), (0, pad)))[:, :, :, None],
                               (heads, n_tok, rows, dh))
    dq_b = jnp.broadcast_to(jnp.pad(dq, ((0, 0), (0, pad)))[:, :, None], (heads, rows, dh))
    dk_b = jnp.broadcast_to(jnp.pad(dk, ((0, 0), (0, pad)))[:, :, None], (heads, rows, dh))
    gc_b = jnp.broadcast_to(gc[:, None, None], (heads, 1, dh))
    kern = functools.partial(_mix_sample_kernel, heads=heads, n_tok=n_tok)
    return pl.pallas_call(
        kern,
        grid=(nb // tb,),
        in_specs=[
            pl.BlockSpec((tb, rows, d), lambda i: (i, 0, 0)),
            pl.BlockSpec((3, tb, 1, d), lambda i: (0, i, 0, 0)),
            _resident((1, d)),
            _resident(w_in.shape),
            _resident((1, ret_w)),
            _resident((n_taps, conv_w)),
            _resident((1, conv_w)),
            _resident((1, conv_w)),
            _resident((1, conv_w)),
            _resident(w_out.shape),
            _resident((rows, dh)),
            _resident((rows, dh)),
            _resident((heads, n_tok, rows, dh)),
            _resident((heads, rows, dh)),
            _resident((heads, rows, dh)),
            _resident((heads, 1, dh)),
            pl.BlockSpec((tb, heads, dh, dh), lambda i: (i, 0, 0, 0)),
            pl.BlockSpec((tb, n_buf, conv_w), lambda i: (i, 0, 0)),
        ],
        out_specs=[
            pl.BlockSpec((tb, rows, d), lambda i: (i, 0, 0)),
            pl.BlockSpec((tb, heads, dh, dh), lambda i: (i, 0, 0, 0)),
            pl.BlockSpec((tb, n_buf, conv_w), lambda i: (i, 0, 0)),
        ],
        out_shape=[
            jax.ShapeDtypeStruct(x.shape, F32),
            jax.ShapeDtypeStruct(state_ret.shape, F32),
            jax.ShapeDtypeStruct(state_conv.shape, F32),
        ],
        scratch_shapes=[
            pltpu.VMEM((tb, rows, ret_w), F32),
            pltpu.VMEM((tb, rows, ret_w), F32),
            pltpu.VMEM((tb, rows, ret_w), F32),
            pltpu.VMEM((tb, rows, ret_w), F32),
            pltpu.VMEM((tb, win_rows, conv_w), F32),
        ],
        compiler_params=pltpu.CompilerParams(dimension_semantics=("arbitrary",),
                                             vmem_limit_bytes=VMEM_LIMIT_BYTES),
        name="mix_sample",
    )(x, mods, norm_gain.reshape(1, d), w_in, gn_gain.reshape(1, -1), dw_w, dw_b.reshape(1, -1),
      ln_g.reshape(1, -1), ln_b.reshape(1, -1), w_out, cos2, sin2, dmask_b, dq_b, dk_b, gc_b,
      state_ret, state_conv)


def _layer(xp, xs, mods_p, mods_s, sret, sconv, lw, final_gain, n_tok):
    (norm_ffn1, w1g, w1u, w1d, norm_mix, w_in, gn_gain, dw_w, dw_b, ln_g, ln_b, w_out,
     norm_ffn2, w2g, w2u, w2d) = lw
    nb, seq, d = xp.shape
    ns, rows, _ = xs.shape
    heads = sret.shape[1]
    tm, ts = PROMPT_TOKEN_TILE, SAMPLE_SEQ_TILE

    ffn_p = functools.partial(_ffn, block=(1, tm, d), mod_block=(3, 1, 1, d), grid=(nb, seq // tm),
                              x_index=lambda b, t: (b, t, 0), mod_index=lambda b, t: (0, b, 0, 0))
    ffn_s = functools.partial(_ffn, block=(ts, rows, d), mod_block=(3, ts, 1, d), grid=(ns // ts,),
                              x_index=lambda i: (i, 0, 0), mod_index=lambda i: (0, i, 0, 0))

    xp = ffn_p(xp, mods_p[0:3], norm_ffn1, w1g, w1u, w1d, None)
    xs = ffn_s(xs, mods_s[0:3], norm_ffn1, w1g, w1u, w1d, None)
    xp, ret_p, conv_p = _mix_prompt(xp, mods_p[3:6], norm_mix, w_in, gn_gain, dw_w, dw_b, ln_g, ln_b,
                                    w_out, heads)
    xs, ret_s, conv_s = _mix_sample(xs, mods_s[3:6], norm_mix, w_in, gn_gain, dw_w, dw_b, ln_g, ln_b,
                                    w_out, sret, sconv, n_tok)
    xp = ffn_p(xp, mods_p[6:9], norm_ffn2, w2g, w2u, w2d, final_gain)
    xs = ffn_s(xs, mods_s[6:9], norm_ffn2, w2g, w2u, w2d, final_gain)
    return xp, xs, ret_p, conv_p, ret_s, conv_s


def kernel(x_prompt, x_sample, c_prompt, c_sample, state_ret, state_conv, norm_ffn1, ffn1_w_gate,
           ffn1_w_up, ffn1_w_down, norm_mix, w_in, ret_gn_gain, dw_w, dw_b, conv_ln_gain, conv_ln_bias,
           w_out, norm_ffn2, ffn2_w_gate, ffn2_w_up, ffn2_w_down, w_ada, b_ada, norm_final):
    depth = w_in.shape[0]
    nb = x_prompt.shape[0]
    ns, n_tok, d = x_sample.shape
    assert n_tok <= SUBLANES and x_prompt.shape[1] % RET_CHUNK == 0

    xp = x_prompt
    xs = jnp.pad(x_sample, ((0, 0), (0, SUBLANES - n_tok), (0, 0)))
    c_all = jnp.concatenate([c_prompt, c_sample], axis=0)

    ret_p, conv_p, ret_s, conv_s = [], [], [], []
    for l in range(depth):
        ada = _ada(c_all, w_ada[l], b_ada[l])
        mods_p = ada[:, :nb].reshape(N_MOD, nb, 1, d)
        mods_s = ada[:, nb:].reshape(N_MOD, ns, 1, d)
        lw = (norm_ffn1[l], ffn1_w_gate[l].astype(BF16), ffn1_w_up[l].astype(BF16),
              ffn1_w_down[l].astype(BF16), norm_mix[l], w_in[l].astype(BF16), ret_gn_gain[l], dw_w[l],
              dw_b[l], conv_ln_gain[l], conv_ln_bias[l], w_out[l].astype(BF16), norm_ffn2[l],
              ffn2_w_gate[l].astype(BF16), ffn2_w_up[l].astype(BF16), ffn2_w_down[l].astype(BF16))
        final_gain = norm_final if l == depth - 1 else None
        xp, xs, rp, cp, rs, cs = _layer(xp, xs, mods_p, mods_s, state_ret[l], state_conv[l], lw,
                                        final_gain, n_tok)
        ret_p.append(rp)
        conv_p.append(cp)
        ret_s.append(rs)
        conv_s.append(cs)

    return (xp, xs[:, :n_tok], jnp.stack(ret_p), jnp.stack(conv_p), jnp.stack(ret_s), jnp.stack(conv_s))
```

```python
import functools
import math

import jax
import jax.numpy as jnp
from jax import lax
from jax.experimental import pallas as pl
from jax.experimental.pallas import tpu as pltpu

F32 = jnp.float32
BF16 = jnp.bfloat16

PAST_LEN = 16384
RET_CHUNK = 128
ROPE_BASE = 10000.0
EPS = 1e-6
N_MOD = 9

SUBLANES = 8
LANES = 128
VMEM_LIMIT_BYTES = 56 * 1024 * 1024

PROMPT_TOKEN_TILE = 512
SAMPLE_MIX_SEQ_TILE = 16
FF_CHUNK = 256


def _resident(shape):
    n = len(shape)
    return pl.BlockSpec(shape, lambda *_: (0,) * n, pipeline_mode=pl.Buffered(1))


def _rmsnorm(x, gain):
    ms = jnp.mean(x * x, axis=-1, keepdims=True)
    return x * lax.rsqrt(ms + EPS) * gain


def _dot(a, b):
    return jnp.dot(a, b, preferred_element_type=F32)


def _ada_kernel(c_ref, w_ref, b_ref, o_ref):
    c = c_ref[...]
    h = (c * jax.nn.sigmoid(c)).astype(BF16)
    o_ref[0] = _dot(h, w_ref[...].astype(BF16)) + b_ref[...]


def _ada(c_all, w_ada, b_ada):
    nb, d = c_all.shape
    return pl.pallas_call(
        _ada_kernel,
        grid=(N_MOD,),
        in_specs=[
            pl.BlockSpec((nb, d), lambda j: (0, 0)),
            pl.BlockSpec((d, d), lambda j: (0, j)),
            pl.BlockSpec((1, d), lambda j: (0, j)),
        ],
        out_specs=pl.BlockSpec((1, nb, d), lambda j: (j, 0, 0)),
        out_shape=jax.ShapeDtypeStruct((N_MOD, nb, d), F32),
        compiler_params=pltpu.CompilerParams(dimension_semantics=("arbitrary",),
                                             vmem_limit_bytes=VMEM_LIMIT_BYTES),
        name="ada",
    )(c_all, w_ada, b_ada.reshape(1, -1))


def _ffn_kernel(xp_ref, xs_ref, modp_ref, mods_ref, ng_ref, wg_ref, wu_ref, wd_ref, *rest,
                n_stage, n_prompt, final_norm):
    if final_norm:
        fg_ref, op_ref, os_ref, wg_scr, wu_scr, wd_scr, act_scr = rest
    else:
        op_ref, os_ref, wg_scr, wu_scr, wd_scr, act_scr = rest
    i = pl.program_id(0)
    n_chunks, _, ffc = wg_scr.shape

    @pl.when(i < n_stage)
    def _():
        wg_scr[i] = wg_ref[...].astype(BF16)
        wu_scr[i] = wu_ref[...].astype(BF16)
        wd_scr[pl.ds(pl.multiple_of(i * ffc, ffc), ffc), :] = wd_ref[...].astype(BF16)

    def tile(x_ref, mod_ref, o_ref):
        x = x_ref[...]
        g_, r_, d_ = x.shape
        shift, scale, gate = mod_ref[0], mod_ref[1], mod_ref[2]
        h = _rmsnorm(x, ng_ref[...]) * (1.0 + scale) + shift
        hb = h.astype(BF16).reshape(g_ * r_, d_)
        for c in range(n_chunks):
            g = _dot(hb, wg_scr[c])
            u = _dot(hb, wu_scr[c])
            act_scr[:, c * ffc:(c + 1) * ffc] = (g * jax.nn.sigmoid(g) * u).astype(BF16)
        y = _dot(act_scr[...], wd_scr[...]).reshape(g_, r_, d_)
        out = x + 0.5 * gate * y
        if final_norm:
            out = _rmsnorm(out, fg_ref[...])
        o_ref[...] = out

    @pl.when(jnp.logical_and(i >= n_stage, i < n_stage + n_prompt))
    def _():
        tile(xp_ref, modp_ref, op_ref)

    @pl.when(i == n_stage + n_prompt)
    def _():
        tile(xs_ref, mods_ref, os_ref)


def _ffn(xp, xs, mods_p, mods_s, norm_gain, wg, wu, wd, final_gain):
    nb, seq, d = xp.shape
    d_ff = wg.shape[1]
    tm = PROMPT_TOKEN_TILE
    ffc = FF_CHUNK
    assert seq % tm == 0 and d_ff % ffc == 0 and xs.shape[0] * xs.shape[1] == tm
    per_seq = seq // tm
    n_stage, n_prompt = d_ff // ffc, nb * per_seq
    final_norm = final_gain is not None

    def prompt_tile(i):
        return jnp.clip(i - n_stage, 0, n_prompt - 1)

    def stage(i):
        return jnp.minimum(i, n_stage - 1)

    x_spec = pl.BlockSpec((1, tm, d), lambda i: (prompt_tile(i) // per_seq, prompt_tile(i) % per_seq, 0))
    in_specs = [
        x_spec,
        _resident(xs.shape),
        pl.BlockSpec((3, 1, 1, d), lambda i: (0, prompt_tile(i) // per_seq, 0, 0)),
        _resident(mods_s.shape),
        _resident((1, d)),
        pl.BlockSpec((d, ffc), lambda i: (0, stage(i))),
        pl.BlockSpec((d, ffc), lambda i: (0, stage(i))),
        pl.BlockSpec((ffc, d), lambda i: (stage(i), 0)),
    ]
    args = [xp, xs, mods_p, mods_s, norm_gain.reshape(1, d), wg, wu, wd]
    if final_norm:
        in_specs.append(_resident((1, d)))
        args.append(final_gain.reshape(1, d))
    return pl.pallas_call(
        functools.partial(_ffn_kernel, n_stage=n_stage, n_prompt=n_prompt, final_norm=final_norm),
        grid=(n_stage + n_prompt + 1,),
        in_specs=in_specs,
        out_specs=[x_spec, pl.BlockSpec(xs.shape, lambda i: (0, 0, 0))],
        out_shape=[jax.ShapeDtypeStruct(xp.shape, F32), jax.ShapeDtypeStruct(xs.shape, F32)],
        scratch_shapes=[
            pltpu.VMEM((n_stage, d, ffc), BF16),
            pltpu.VMEM((n_stage, d, ffc), BF16),
            pltpu.VMEM((d_ff, d), BF16),
            pltpu.VMEM((tm, d_ff), BF16),
        ],
        compiler_params=pltpu.CompilerParams(dimension_semantics=("arbitrary",),
                                             vmem_limit_bytes=VMEM_LIMIT_BYTES),
        name="ffn_final" if final_norm else "ffn",
    )(*args)


def _rotate(xh, cos2, sin2):
    return xh * cos2 + pltpu.roll(xh, xh.shape[-1] // 2, axis=xh.ndim - 1) * sin2


def _group_norm_gate(o, gate_pre, gain):
    mu = jnp.mean(o, axis=-1, keepdims=True)
    ctr = o - mu
    var = jnp.mean(ctr * ctr, axis=-1, keepdims=True)
    return gate_pre * jax.nn.sigmoid(gate_pre) * (ctr * lax.rsqrt(var + EPS) * gain)


def _layer_norm_silu(y, gain, bias):
    mu = jnp.mean(y, axis=-1, keepdims=True)
    ctr = y - mu
    var = jnp.mean(ctr * ctr, axis=-1, keepdims=True)
    yn = ctr * lax.rsqrt(var + EPS) * gain + bias
    return yn * jax.nn.sigmoid(yn)


def _conv_taps(win_ref, dww_ref, n_out, lead):
    n_taps = dww_ref.shape[0]
    pre = (slice(None),) * (len(win_ref.shape) - 2)
    total = None
    for phase in range(SUBLANES):
        offsets = [o for o in range(lead, lead + n_taps) if o % SUBLANES == phase]
        if not offsets:
            continue
        rows = n_out if phase == 0 else n_out + SUBLANES
        phase_sum = None
        for o in offsets:
            a = o - phase
            term = win_ref[pre + (slice(a, a + rows), slice(None))] * dww_ref[o - lead:o - lead + 1, :]
            phase_sum = term if phase_sum is None else phase_sum + term
        if phase:
            phase_sum = phase_sum[pre + (slice(phase, phase + n_out), slice(None))]
        total = phase_sum if total is None else total + phase_sum
    return total


def _mix_prompt_kernel(x_ref, mod_ref, ng_ref, win_ref, gn_ref, dww_ref, dwb_ref, lng_ref, lnb_ref,
                       wout_ref, cos_ref, sin_ref, dmask_ref, dq_ref, dk_ref, gc_ref,
                       o_ref, sout_ref, cout_ref, s_scr, u_scr, mix_scr, *, heads, hist):
    t_idx = pl.program_id(1)
    n_t = pl.num_programs(1)
    tt, d_ = x_ref.shape[1], x_ref.shape[2]
    ret_w = gn_ref.shape[1]
    dh = ret_w // heads
    conv_w = dww_ref.shape[1]
    n_taps = dww_ref.shape[0]

    @pl.when(t_idx == 0)
    def _():
        s_scr[...] = jnp.zeros_like(s_scr)
        u_scr[0:hist, :] = jnp.zeros((hist, conv_w), F32)

    x = x_ref[0]
    shift, scale, gate = mod_ref[0, 0], mod_ref[1, 0], mod_ref[2, 0]
    h = _rmsnorm(x, ng_ref[...]) * (1.0 + scale) + shift
    hb = h.astype(BF16)

    cos2, sin2 = cos_ref[...], sin_ref[...]
    k_scale = dh ** -0.5
    o_q, o_k, o_v, o_g = 0, ret_w, 2 * ret_w, 3 * ret_w
    o_a, o_b = 4 * ret_w, 4 * ret_w + conv_w

    ab = _dot(hb, win_ref[:, o_a:o_b + conv_w])
    u_scr[hist:hist + tt, :] = ab[:, :conv_w] * jax.nn.sigmoid(ab[:, conv_w:])
    y = _conv_taps(u_scr, dww_ref, tt, hist - (n_taps - 1)) + dwb_ref[...]
    conv_out = _layer_norm_silu(y, lng_ref[...], lnb_ref[...]).astype(BF16)
    u_scr[0:hist, :] = u_scr[tt:tt + hist, :]

    proj = _dot(hb, win_ref[:, o_q:o_a])
    for hd in range(heads):
        lo = hd * dh
        q = _rotate(proj[:, o_q + lo:o_q + lo + dh], cos2, sin2)
        k = _rotate(proj[:, o_k + lo:o_k + lo + dh], cos2, sin2) * k_scale
        v = proj[:, o_v + lo:o_v + lo + dh]
        gate_pre = proj[:, o_g + lo:o_g + lo + dh]
        dmask, dq, dk, gc = dmask_ref[hd], dq_ref[hd], dk_ref[hd], gc_ref[hd]
        state = s_scr[hd]
        outs = []
        for c0 in range(0, tt, RET_CHUNK):
            qc, kc = q[c0:c0 + RET_CHUNK], k[c0:c0 + RET_CHUNK]
            vb = v[c0:c0 + RET_CHUNK].astype(BF16)
            scores = lax.dot_general(qc.astype(BF16), kc.astype(BF16), (((1,), (1,)), ((), ())),
                                     preferred_element_type=F32)
            inner = _dot((scores * dmask).astype(BF16), vb)
            cross = _dot((qc * dq).astype(BF16), state.astype(BF16))
            outs.append(inner + cross)
            state = gc * state + lax.dot_general((kc * dk).astype(BF16), vb, (((0,), (0,)), ((), ())),
                                                 preferred_element_type=F32)
        s_scr[hd] = state
        o = jnp.concatenate(outs, axis=0) if len(outs) > 1 else outs[0]
        mix_scr[:, lo:lo + dh] = _group_norm_gate(o, gate_pre, gn_ref[:, lo:lo + dh]).astype(BF16)

    mixed = _dot(mix_scr[...], wout_ref[0:ret_w, :]) + _dot(conv_out, wout_ref[ret_w:ret_w + conv_w, :])
    o_ref[0] = x + gate * mixed

    @pl.when(t_idx == n_t - 1)
    def _():
        sout_ref[0] = s_scr[...]
        cout_ref[0] = u_scr[hist - (n_taps - 1):hist, :]


def _decay_tables(heads, chunk):
    lg = jnp.log(1.0 - 2.0 ** (-5.0 - jnp.arange(heads, dtype=F32)))
    idx = jnp.arange(chunk, dtype=F32)
    diff = idx[:, None] - idx[None, :]
    dmask = jnp.where(diff[None] >= 0, jnp.exp(lg[:, None, None] * jnp.maximum(diff, 0.0)[None]), 0.0)
    dq = jnp.exp(lg[:, None] * (idx[None, :] + 1.0))
    dk = jnp.exp(lg[:, None] * (chunk - 1.0 - idx[None, :]))
    gc = jnp.exp(lg * chunk)
    return dmask, dq, dk, gc


def _rotary_tables(pos0, n_pos, half):
    inv = ROPE_BASE ** (-jnp.arange(half, dtype=F32) / half)
    pos = (pos0 + jnp.arange(n_pos, dtype=jnp.int32)).astype(F32)
    ang = pos[:, None] * inv[None, :]
    cos, sin = jnp.cos(ang), jnp.sin(ang)
    return jnp.concatenate([cos, cos], axis=-1), jnp.concatenate([-sin, sin], axis=-1)


def _mix_prompt(x, mods, norm_gain, w_in, gn_gain, dw_w, dw_b, ln_g, ln_b, w_out, heads):
    nb, seq, d = x.shape
    tt = PROMPT_TOKEN_TILE
    ret_w = gn_gain.shape[0]
    dh = ret_w // heads
    n_taps, conv_w = dw_w.shape
    hist = -(-(n_taps - 1) // SUBLANES) * SUBLANES
    chunk = RET_CHUNK
    cos2, sin2 = _rotary_tables(0, seq, dh // 2)
    dmask, dq, dk, gc = _decay_tables(heads, chunk)
    dq_b = jnp.broadcast_to(dq[:, :, None], (heads, chunk, dh))
    dk_b = jnp.broadcast_to(dk[:, :, None], (heads, chunk, dh))
    gc_b = jnp.broadcast_to(gc[:, None, None], (heads, 1, dh))
    kern = functools.partial(_mix_prompt_kernel, heads=heads, hist=hist)
    return pl.pallas_call(
        kern,
        grid=(nb, seq // tt),
        in_specs=[
            pl.BlockSpec((1, tt, d), lambda b, t: (b, t, 0)),
            pl.BlockSpec((3, 1, 1, d), lambda b, t: (0, b, 0, 0)),
            _resident((1, d)),
            _resident(w_in.shape),
            _resident((1, ret_w)),
            _resident((n_taps, conv_w)),
            _resident((1, conv_w)),
            _resident((1, conv_w)),
            _resident((1, conv_w)),
            _resident(w_out.shape),
            pl.BlockSpec((tt, dh), lambda b, t: (t, 0)),
            pl.BlockSpec((tt, dh), lambda b, t: (t, 0)),
            _resident((heads, chunk, chunk)),
            _resident((heads, chunk, dh)),
            _resident((heads, chunk, dh)),
            _resident((heads, 1, dh)),
        ],
        out_specs=[
            pl.BlockSpec((1, tt, d), lambda b, t: (b, t, 0)),
            pl.BlockSpec((1, heads, dh, dh), lambda b, t: (b, 0, 0, 0)),
            pl.BlockSpec((1, n_taps - 1, conv_w), lambda b, t: (b, 0, 0)),
        ],
        out_shape=[
            jax.ShapeDtypeStruct(x.shape, F32),
            jax.ShapeDtypeStruct((nb, heads, dh, dh), F32),
            jax.ShapeDtypeStruct((nb, n_taps - 1, conv_w), F32),
        ],
        scratch_shapes=[
            pltpu.VMEM((heads, dh, dh), F32),
            pltpu.VMEM((hist + tt, conv_w), F32),
            pltpu.VMEM((tt, ret_w), BF16),
        ],
        compiler_params=pltpu.CompilerParams(dimension_semantics=("arbitrary", "arbitrary"),
                                             vmem_limit_bytes=VMEM_LIMIT_BYTES),
        name="mix_prompt",
    )(x, mods, norm_gain.reshape(1, d), w_in, gn_gain.reshape(1, -1), dw_w, dw_b.reshape(1, -1),
      ln_g.reshape(1, -1), ln_b.reshape(1, -1), w_out, cos2, sin2, dmask, dq_b, dk_b, gc_b)


def _mix_sample_kernel(x_ref, mod_ref, ng_ref, win_ref, gn_ref, dww_ref, dwb_ref, lng_ref, lnb_ref,
                       wout_ref, cos_ref, sin_ref, dmask_ref, dq_ref, dk_ref, gc_ref, sin_state_ref,
                       cin_ref, o_ref, sout_ref, cout_ref, qd_scr, kd_scr, v_scr, cross_scr, win_scr,
                       *, heads, n_tok):
    tb, rows, d_ = x_ref.shape
    ret_w = gn_ref.shape[1]
    dh = ret_w // heads
    n_taps, conv_w = dww_ref.shape
    n_buf = n_taps - 1

    x = x_ref[...]
    shift, scale, gate = mod_ref[0], mod_ref[1], mod_ref[2]
    h = _rmsnorm(x, ng_ref[...]) * (1.0 + scale) + shift
    proj = _dot(h.astype(BF16).reshape(tb * rows, d_), win_ref[...]).reshape(tb, rows, -1)

    cos2, sin2 = cos_ref[...], sin_ref[...]
    k_scale = dh ** -0.5
    o_q, o_k, o_v, o_g = 0, ret_w, 2 * ret_w, 3 * ret_w
    o_a, o_b = 4 * ret_w, 4 * ret_w + conv_w

    inner_heads = []
    for hd in range(heads):
        lo = hd * dh
        q = _rotate(proj[:, :, o_q + lo:o_q + lo + dh], cos2, sin2)
        k = _rotate(proj[:, :, o_k + lo:o_k + lo + dh], cos2, sin2) * k_scale
        v = proj[:, :, o_v + lo:o_v + lo + dh]
        qd_scr[:, :, lo:lo + dh] = q * dq_ref[hd]
        kd_scr[:, :, lo:lo + dh] = k * dk_ref[hd]
        v_scr[:, :, lo:lo + dh] = v
        row_id = lax.broadcasted_iota(jnp.int32, (tb, rows, dh), 1)
        inner = jnp.zeros((tb, rows, dh), F32)
        for i in range(n_tok):
            s_i = jnp.sum(q[:, i:i + 1, :] * k, axis=-1, keepdims=True)
            p_i = s_i * dmask_ref[hd, i]
            o_i = jnp.sum(p_i * v, axis=1, keepdims=True)
            inner = jnp.where(row_id == i, o_i, inner)
        inner_heads.append(inner)

    zeros_pad = jnp.zeros((dh - rows, dh), F32)

    def per_sequence(b, carry):
        for hd in range(heads):
            lo = hd * dh
            state = sin_state_ref[b, hd]
            cross_scr[b, :, lo:lo + dh] = _dot(qd_scr[b, :, lo:lo + dh].astype(BF16), state.astype(BF16))
            k_sq = jnp.concatenate([kd_scr[b, :, lo:lo + dh], zeros_pad], axis=0).astype(BF16)
            v_sq = jnp.concatenate([v_scr[b, :, lo:lo + dh], zeros_pad], axis=0).astype(BF16)
            upd = lax.dot_general(k_sq, v_sq, (((0,), (0,)), ((), ())), preferred_element_type=F32)
            sout_ref[b, hd] = gc_ref[hd] * state + upd
        return carry

    lax.fori_loop(0, tb, per_sequence, 0)

    mixed = []
    for hd in range(heads):
        lo = hd * dh
        o = inner_heads[hd] + cross_scr[:, :, lo:lo + dh]
        gate_pre = proj[:, :, o_g + lo:o_g + lo + dh]
        mixed.append(_group_norm_gate(o, gate_pre, gn_ref[:, lo:lo + dh]).astype(BF16))

    a = proj[:, :, o_a:o_a + conv_w]
    b_ = proj[:, :, o_b:o_b + conv_w]
    tail0 = (n_buf // SUBLANES) * SUBLANES
    win_scr[:, tail0:, :] = jnp.zeros((tb, win_scr.shape[1] - tail0, conv_w), F32)
    win_scr[:, 0:n_buf, :] = cin_ref[...]
    win_scr[:, n_buf:n_buf + rows, :] = a * jax.nn.sigmoid(b_)
    y = _conv_taps(win_scr, dww_ref, rows, 0) + dwb_ref[...]
    mixed.append(_layer_norm_silu(y, lng_ref[...], lnb_ref[...]).astype(BF16))
    cout_ref[...] = win_scr[:, n_tok:n_tok + n_buf, :]

    mixed = jnp.concatenate(mixed, axis=-1).reshape(tb * rows, ret_w + conv_w)
    o_ref[...] = x + gate * _dot(mixed, wout_ref[...]).reshape(tb, rows, d_)


def _mix_sample(x, mods, norm_gain, w_in, gn_gain, dw_w, dw_b, ln_g, ln_b, w_out, state_ret, state_conv,
                n_tok):
    nb, rows, d = x.shape
    tb = SAMPLE_MIX_SEQ_TILE
    heads, dh = state_ret.shape[1], state_ret.shape[2]
    ret_w = heads * dh
    n_taps, conv_w = dw_w.shape
    n_buf = n_taps - 1
    win_rows = -(-(n_buf + rows + SUBLANES) // SUBLANES) * SUBLANES
    cos2, sin2 = _rotary_tables(PAST_LEN, rows, dh // 2)
    dmask, dq, dk, gc = _decay_tables(heads, n_tok)
    pad = rows - n_tok
    dmask_b = jnp.broadcast_to(jnp.pad(dmask, ((0, 0), (0, 0), (0, pad)))[:, :, :, None],
                               (heads, n_tok, rows, dh))
    dq_b = jnp.broadcast_to(jnp.pad(dq, ((0, 0), (0, pad)))[:, :, None], (heads, rows, dh))
    dk_b = jnp.broadcast_to(jnp.pad(dk, ((0, 0), (0, pad)))[:, :, None], (heads, rows, dh))
    gc_b = jnp.broadcast_to(gc[:, None, None], (heads, 1, dh))
    kern = functools.partial(_mix_sample_kernel, heads=heads, n_tok=n_tok)
    return pl.pallas_call(
        kern,
        grid=(nb // tb,),
        in_specs=[
            pl.BlockSpec((tb, rows, d), lambda i: (i, 0, 0)),
            pl.BlockSpec((3, tb, 1, d), lambda i: (0, i, 0, 0)),
            _resident((1, d)),
            _resident(w_in.shape),
            _resident((1, ret_w)),
            _resident((n_taps, conv_w)),
            _resident((1, conv_w)),
            _resident((1, conv_w)),
            _resident((1, conv_w)),
            _resident(w_out.shape),
            _resident((rows, dh)),
            _resident((rows, dh)),
            _resident((heads, n_tok, rows, dh)),
            _resident((heads, rows, dh)),
            _resident((heads, rows, dh)),
            _resident((heads, 1, dh)),
            pl.BlockSpec((tb, heads, dh, dh), lambda i: (i, 0, 0, 0)),
            pl.BlockSpec((tb, n_buf, conv_w), lambda i: (i, 0, 0)),
        ],
        out_specs=[
            pl.BlockSpec((tb, rows, d), lambda i: (i, 0, 0)),
            pl.BlockSpec((tb, heads, dh, dh), lambda i: (i, 0, 0, 0)),
            pl.BlockSpec((tb, n_buf, conv_w), lambda i: (i, 0, 0)),
        ],
        out_shape=[
            jax.ShapeDtypeStruct(x.shape, F32),
            jax.ShapeDtypeStruct(state_ret.shape, F32),
            jax.ShapeDtypeStruct(state_conv.shape, F32),
        ],
        scratch_shapes=[
            pltpu.VMEM((tb, rows, ret_w), F32),
            pltpu.VMEM((tb, rows, ret_w), F32),
            pltpu.VMEM((tb, rows, ret_w), F32),
            pltpu.VMEM((tb, rows, ret_w), F32),
            pltpu.VMEM((tb, win_rows, conv_w), F32),
        ],
        compiler_params=pltpu.CompilerParams(dimension_semantics=("arbitrary",),
                                             vmem_limit_bytes=VMEM_LIMIT_BYTES),
        name="mix_sample",
    )(x, mods, norm_gain.reshape(1, d), w_in, gn_gain.reshape(1, -1), dw_w, dw_b.reshape(1, -1),
      ln_g.reshape(1, -1), ln_b.reshape(1, -1), w_out, cos2, sin2, dmask_b, dq_b, dk_b, gc_b,
      state_ret, state_conv)


def _layer(xp, xs, mods_p, mods_s, sret, sconv, lw, final_gain, n_tok):
    (norm_ffn1, w1g, w1u, w1d, norm_mix, w_in, gn_gain, dw_w, dw_b, ln_g, ln_b, w_out,
     norm_ffn2, w2g, w2u, w2d) = lw
    nb, seq, d = xp.shape
    n_tok, ns, _ = xs.shape
    heads = sret.shape[1]
    mods_p = mods_p.reshape(N_MOD, nb, 1, d)
    mods_s_tm = mods_s.reshape(N_MOD, 1, ns, d)

    xp, xs = _ffn(xp, xs, mods_p[0:3], mods_s_tm[0:3], norm_ffn1, w1g, w1u, w1d, None)
    xp, ret_p, conv_p = _mix_prompt(xp, mods_p[3:6], norm_mix, w_in, gn_gain, dw_w, dw_b, ln_g, ln_b,
                                    w_out, heads)
    xs_seq = jnp.pad(xs.transpose(1, 0, 2), ((0, 0), (0, SUBLANES - n_tok), (0, 0)))
    xs_seq, ret_s, conv_s = _mix_sample(xs_seq, mods_s[3:6].reshape(3, ns, 1, d), norm_mix, w_in, gn_gain,
                                        dw_w, dw_b, ln_g, ln_b, w_out, sret, sconv, n_tok)
    xs = xs_seq[:, :n_tok].transpose(1, 0, 2)
    xp, xs = _ffn(xp, xs, mods_p[6:9], mods_s_tm[6:9], norm_ffn2, w2g, w2u, w2d, final_gain)
    return xp, xs, ret_p, conv_p, ret_s, conv_s


def kernel(x_prompt, x_sample, c_prompt, c_sample, state_ret, state_conv, norm_ffn1, ffn1_w_gate,
           ffn1_w_up, ffn1_w_down, norm_mix, w_in, ret_gn_gain, dw_w, dw_b, conv_ln_gain, conv_ln_bias,
           w_out, norm_ffn2, ffn2_w_gate, ffn2_w_up, ffn2_w_down, w_ada, b_ada, norm_final):
    depth = w_in.shape[0]
    nb = x_prompt.shape[0]
    ns, n_tok, d = x_sample.shape
    assert n_tok <= SUBLANES and x_prompt.shape[1] % RET_CHUNK == 0

    xp = x_prompt
    xs = x_sample.transpose(1, 0, 2)
    c_all = jnp.concatenate([c_prompt, c_sample], axis=0)

    ret_p, conv_p, ret_s, conv_s = [], [], [], []
    for l in range(depth):
        ada = _ada(c_all, w_ada[l], b_ada[l])
        lw = (norm_ffn1[l], ffn1_w_gate[l], ffn1_w_up[l], ffn1_w_down[l], norm_mix[l],
              w_in[l].astype(BF16), ret_gn_gain[l], dw_w[l], dw_b[l], conv_ln_gain[l], conv_ln_bias[l],
              w_out[l].astype(BF16), norm_ffn2[l], ffn2_w_gate[l], ffn2_w_up[l], ffn2_w_down[l])
        final_gain = norm_final if l == depth - 1 else None
        xp, xs, rp, cp, rs, cs = _layer(xp, xs, ada[:, :nb], ada[:, nb:], state_ret[l], state_conv[l], lw,
                                        final_gain, n_tok)
        ret_p.append(rp)
        conv_p.append(cp)
        ret_s.append(rs)
        conv_s.append(cs)

    return (xp, xs.transpose(1, 0, 2), jnp.stack(ret_p), jnp.stack(conv_p), jnp.stack(ret_s),
            jnp.stack(conv_s))
```

```python
import functools
import math

import jax
import jax.numpy as jnp
from jax import lax
from jax.experimental import pallas as pl
from jax.experimental.pallas import tpu as pltpu

F32 = jnp.float32
BF16 = jnp.bfloat16

PAST_LEN = 16384
RET_CHUNK = 128
ROPE_BASE = 10000.0
EPS = 1e-6
N_MOD = 9

SUBLANES = 8
LANES = 128
VMEM_LIMIT_BYTES = 56 * 1024 * 1024

PROMPT_TOKEN_TILE = 512
SAMPLE_MIX_SEQ_TILE = 16
FF_CHUNK = 256


def _resident(shape):
    n = len(shape)
    return pl.BlockSpec(shape, lambda *_: (0,) * n, pipeline_mode=pl.Buffered(1))


def _rmsnorm(x, gain):
    ms = jnp.mean(x * x, axis=-1, keepdims=True)
    return x * lax.rsqrt(ms + EPS) * gain


def _dot(a, b):
    return jnp.dot(a, b, preferred_element_type=F32)


def _ada_kernel(c_ref, w_ref, b_ref, o_ref):
    c = c_ref[...]
    h = (c * jax.nn.sigmoid(c)).astype(BF16)
    o_ref[0] = _dot(h, w_ref[...].astype(BF16)) + b_ref[...]


def _ada(c_all, w_ada, b_ada):
    nb, d = c_all.shape
    return pl.pallas_call(
        _ada_kernel,
        grid=(N_MOD,),
        in_specs=[
            pl.BlockSpec((nb, d), lambda j: (0, 0)),
            pl.BlockSpec((d, d), lambda j: (0, j)),
            pl.BlockSpec((1, d), lambda j: (0, j)),
        ],
        out_specs=pl.BlockSpec((1, nb, d), lambda j: (j, 0, 0)),
        out_shape=jax.ShapeDtypeStruct((N_MOD, nb, d), F32),
        compiler_params=pltpu.CompilerParams(dimension_semantics=("arbitrary",),
                                             vmem_limit_bytes=VMEM_LIMIT_BYTES),
        name="ada",
    )(c_all, w_ada, b_ada.reshape(1, -1))


def _ffn_kernel(xp_ref, xs_ref, modp_ref, mods_ref, ng_ref, wg_ref, wu_ref, wd_ref, *rest,
                n_stage, n_prompt, final_norm):
    if final_norm:
        fg_ref, op_ref, os_ref, wg_scr, wu_scr, wd_scr, act_scr = rest
    else:
        op_ref, os_ref, wg_scr, wu_scr, wd_scr, act_scr = rest
    i = pl.program_id(0)
    n_chunks, _, ffc = wg_scr.shape

    @pl.when(i < n_stage)
    def _():
        wg_scr[i] = wg_ref[...].astype(BF16)
        wu_scr[i] = wu_ref[...].astype(BF16)
        wd_scr[pl.ds(pl.multiple_of(i * ffc, ffc), ffc), :] = wd_ref[...].astype(BF16)

    def tile(x_ref, mod_ref, o_ref):
        x = x_ref[...]
        g_, r_, d_ = x.shape
        shift, scale, gate = mod_ref[0], mod_ref[1], mod_ref[2]
        h = _rmsnorm(x, ng_ref[...]) * (1.0 + scale) + shift
        hb = h.astype(BF16).reshape(g_ * r_, d_)
        for c in range(n_chunks):
            g = _dot(hb, wg_scr[c])
            u = _dot(hb, wu_scr[c])
            act_scr[:, c * ffc:(c + 1) * ffc] = (g * jax.nn.sigmoid(g) * u).astype(BF16)
        y = _dot(act_scr[...], wd_scr[...]).reshape(g_, r_, d_)
        out = x + 0.5 * gate * y
        if final_norm:
            out = _rmsnorm(out, fg_ref[...])
        o_ref[...] = out

    @pl.when(jnp.logical_and(i >= n_stage, i < n_stage + n_prompt))
    def _():
        tile(xp_ref, modp_ref, op_ref)

    @pl.when(i == n_stage + n_prompt)
    def _():
        tile(xs_ref, mods_ref, os_ref)


def _ffn(xp, xs, mods_p, mods_s, norm_gain, wg, wu, wd, final_gain):
    nb, seq, d = xp.shape
    d_ff = wg.shape[1]
    tm = PROMPT_TOKEN_TILE
    ffc = FF_CHUNK
    assert seq % tm == 0 and d_ff % ffc == 0 and xs.shape[0] * xs.shape[1] == tm
    per_seq = seq // tm
    n_stage, n_prompt = d_ff // ffc, nb * per_seq
    final_norm = final_gain is not None

    def prompt_tile(i):
        return jnp.clip(i - n_stage, 0, n_prompt - 1)

    def stage(i):
        return jnp.minimum(i, n_stage - 1)

    x_spec = pl.BlockSpec((1, tm, d), lambda i: (prompt_tile(i) // per_seq, prompt_tile(i) % per_seq, 0))
    in_specs = [
        x_spec,
        _resident(xs.shape),
        pl.BlockSpec((3, 1, 1, d), lambda i: (0, prompt_tile(i) // per_seq, 0, 0)),
        _resident(mods_s.shape),
        _resident((1, d)),
        pl.BlockSpec((d, ffc), lambda i: (0, stage(i))),
        pl.BlockSpec((d, ffc), lambda i: (0, stage(i))),
        pl.BlockSpec((ffc, d), lambda i: (stage(i), 0)),
    ]
    args = [xp, xs, mods_p, mods_s, norm_gain.reshape(1, d), wg, wu, wd]
    if final_norm:
        in_specs.append(_resident((1, d)))
        args.append(final_gain.reshape(1, d))
    return pl.pallas_call(
        functools.partial(_ffn_kernel, n_stage=n_stage, n_prompt=n_prompt, final_norm=final_norm),
        grid=(n_stage + n_prompt + 1,),
        in_specs=in_specs,
        out_specs=[x_spec, pl.BlockSpec(xs.shape, lambda i: (0, 0, 0))],
        out_shape=[jax.ShapeDtypeStruct(xp.shape, F32), jax.ShapeDtypeStruct(xs.shape, F32)],
        scratch_shapes=[
            pltpu.VMEM((n_stage, d, ffc), BF16),
            pltpu.VMEM((n_stage, d, ffc), BF16),
            pltpu.VMEM((d_ff, d), BF16),
            pltpu.VMEM((tm, d_ff), BF16),
        ],
        compiler_params=pltpu.CompilerParams(dimension_semantics=("arbitrary",),
                                             vmem_limit_bytes=VMEM_LIMIT_BYTES),
        name="ffn_final" if final_norm else "ffn",
    )(*args)


def _rotate(xh, cos2, sin2):
    return xh * cos2 + pltpu.roll(xh, xh.shape[-1] // 2, axis=xh.ndim - 1) * sin2


def _group_norm_gate(o, gate_pre, gain):
    mu = jnp.mean(o, axis=-1, keepdims=True)
    ctr = o - mu
    var = jnp.mean(ctr * ctr, axis=-1, keepdims=True)
    return gate_pre * jax.nn.sigmoid(gate_pre) * (ctr * lax.rsqrt(var + EPS) * gain)


def _layer_norm_silu(y, gain, bias):
    mu = jnp.mean(y, axis=-1, keepdims=True)
    ctr = y - mu
    var = jnp.mean(ctr * ctr, axis=-1, keepdims=True)
    yn = ctr * lax.rsqrt(var + EPS) * gain + bias
    return yn * jax.nn.sigmoid(yn)


def _conv_taps(win_ref, dww_ref, n_out, lead):
    n_taps = dww_ref.shape[0]
    pre = (slice(None),) * (len(win_ref.shape) - 2)
    total = None
    for phase in range(SUBLANES):
        offsets = [o for o in range(lead, lead + n_taps) if o % SUBLANES == phase]
        if not offsets:
            continue
        rows = n_out if phase == 0 else n_out + SUBLANES
        phase_sum = None
        for o in offsets:
            a = o - phase
            term = win_ref[pre + (slice(a, a + rows), slice(None))] * dww_ref[o - lead:o - lead + 1, :]
            phase_sum = term if phase_sum is None else phase_sum + term
        if phase:
            phase_sum = phase_sum[pre + (slice(phase, phase + n_out), slice(None))]
        total = phase_sum if total is None else total + phase_sum
    return total


def _mix_prompt_kernel(xa_ref, xb_ref, moda_ref, modb_ref, ng_ref, win_ref, gn_ref, dww_ref, dwb_ref,
                       lng_ref, lnb_ref, wout_ref, cos_ref, sin_ref, dmask_ref, dq_ref, dk_ref, gc_ref,
                       o_ref, sout_ref, cout_ref, s_scr, u_scr, hb_scr, conv_scr, mix_scr,
                       *, heads, hist, per_seq, n_tiles):
    i = pl.program_id(0)
    tt, d_ = xa_ref.shape[1], xa_ref.shape[2]
    ret_w = gn_ref.shape[1]
    dh = ret_w // heads
    n_taps, conv_w = dww_ref.shape
    k_scale = dh ** -0.5
    o_q, o_k, o_v, o_g = 0, ret_w, 2 * ret_w, 3 * ret_w
    o_a, o_b = 4 * ret_w, 4 * ret_w + conv_w

    def stage_a_project():
        x = xa_ref[0]
        shift, scale = moda_ref[0, 0], moda_ref[1, 0]
        h = _rmsnorm(x, ng_ref[...]) * (1.0 + scale) + shift
        hb = h.astype(BF16)
        ab = _dot(hb, win_ref[:, o_a:o_b + conv_w])
        u_scr[hist:hist + tt, :] = ab[:, :conv_w] * jax.nn.sigmoid(ab[:, conv_w:])
        return hb

    def stage_a_convolve(hb):
        hb_scr[...] = hb
        y = _conv_taps(u_scr, dww_ref, tt, hist - (n_taps - 1)) + dwb_ref[...]
        conv_scr[...] = _layer_norm_silu(y, lng_ref[...], lnb_ref[...]).astype(BF16)
        u_scr[0:hist, :] = u_scr[tt:tt + hist, :]

    def stage_b():
        conv_part = _dot(conv_scr[...], wout_ref[ret_w:ret_w + conv_w, :])
        proj = _dot(hb_scr[...], win_ref[:, o_q:o_a])
        cos2, sin2 = cos_ref[...], sin_ref[...]
        for hd in range(heads):
            lo = hd * dh
            q = _rotate(proj[:, o_q + lo:o_q + lo + dh], cos2, sin2)
            k = _rotate(proj[:, o_k + lo:o_k + lo + dh], cos2, sin2) * k_scale
            v = proj[:, o_v + lo:o_v + lo + dh]
            gate_pre = proj[:, o_g + lo:o_g + lo + dh]
            dmask, dq, dk, gc = dmask_ref[hd], dq_ref[hd], dk_ref[hd], gc_ref[hd]
            state = s_scr[hd]
            outs = []
            for c0 in range(0, tt, RET_CHUNK):
                qc, kc = q[c0:c0 + RET_CHUNK], k[c0:c0 + RET_CHUNK]
                vb = v[c0:c0 + RET_CHUNK].astype(BF16)
                scores = lax.dot_general(qc.astype(BF16), kc.astype(BF16), (((1,), (1,)), ((), ())),
                                         preferred_element_type=F32)
                inner = _dot((scores * dmask).astype(BF16), vb)
                cross = _dot((qc * dq).astype(BF16), state.astype(BF16))
                outs.append(inner + cross)
                state = gc * state + lax.dot_general((kc * dk).astype(BF16), vb, (((0,), (0,)), ((), ())),
                                                     preferred_element_type=F32)
            s_scr[hd] = state
            o = jnp.concatenate(outs, axis=0) if len(outs) > 1 else outs[0]
            mix_scr[:, lo:lo + dh] = _group_norm_gate(o, gate_pre, gn_ref[:, lo:lo + dh]).astype(BF16)
        mixed = _dot(mix_scr[...], wout_ref[0:ret_w, :]) + conv_part
        o_ref[0] = xb_ref[0] + modb_ref[2, 0] * mixed

    has_a = i < n_tiles
    has_b = i >= 1
    pos_a = lax.rem(i, per_seq)
    pos_b = lax.rem(i + per_seq - 1, per_seq)

    @pl.when(jnp.logical_and(has_a, pos_a == 0))
    def _():
        u_scr[0:hist, :] = jnp.zeros((hist, conv_w), F32)

    @pl.when(jnp.logical_and(has_b, pos_b == 0))
    def _():
        s_scr[...] = jnp.zeros_like(s_scr)

    @pl.when(i == 0)
    def _():
        stage_a_convolve(stage_a_project())

    @pl.when(jnp.logical_and(has_a, has_b))
    def _():
        hb = stage_a_project()
        stage_b()
        stage_a_convolve(hb)

    @pl.when(i == n_tiles)
    def _():
        stage_b()

    @pl.when(jnp.logical_and(has_a, pos_a == per_seq - 1))
    def _():
        cout_ref[0] = u_scr[hist - (n_taps - 1):hist, :]

    @pl.when(jnp.logical_and(has_b, pos_b == per_seq - 1))
    def _():
        sout_ref[0] = s_scr[...]


def _decay_tables(heads, chunk):
    lg = jnp.log(1.0 - 2.0 ** (-5.0 - jnp.arange(heads, dtype=F32)))
    idx = jnp.arange(chunk, dtype=F32)
    diff = idx[:, None] - idx[None, :]
    dmask = jnp.where(diff[None] >= 0, jnp.exp(lg[:, None, None] * jnp.maximum(diff, 0.0)[None]), 0.0)
    dq = jnp.exp(lg[:, None] * (idx[None, :] + 1.0))
    dk = jnp.exp(lg[:, None] * (chunk - 1.0 - idx[None, :]))
    gc = jnp.exp(lg * chunk)
    return dmask, dq, dk, gc


def _rotary_tables(pos0, n_pos, half):
    inv = ROPE_BASE ** (-jnp.arange(half, dtype=F32) / half)
    pos = (pos0 + jnp.arange(n_pos, dtype=jnp.int32)).astype(F32)
    ang = pos[:, None] * inv[None, :]
    cos, sin = jnp.cos(ang), jnp.sin(ang)
    return jnp.concatenate([cos, cos], axis=-1), jnp.concatenate([-sin, sin], axis=-1)


def _mix_prompt(x, mods, norm_gain, w_in, gn_gain, dw_w, dw_b, ln_g, ln_b, w_out, heads):
    nb, seq, d = x.shape
    tt = PROMPT_TOKEN_TILE
    ret_w = gn_gain.shape[0]
    dh = ret_w // heads
    n_taps, conv_w = dw_w.shape
    hist = -(-(n_taps - 1) // SUBLANES) * SUBLANES
    chunk = RET_CHUNK
    cos2, sin2 = _rotary_tables(0, seq, dh // 2)
    dmask, dq, dk, gc = _decay_tables(heads, chunk)
    dq_b = jnp.broadcast_to(dq[:, :, None], (heads, chunk, dh))
    dk_b = jnp.broadcast_to(dk[:, :, None], (heads, chunk, dh))
    gc_b = jnp.broadcast_to(gc[:, None, None], (heads, 1, dh))
    per_seq = seq // tt
    n_tiles = nb * per_seq

    def tile_a(i):
        return jnp.minimum(i, n_tiles - 1)

    def tile_b(i):
        return jnp.maximum(i - 1, 0)

    kern = functools.partial(_mix_prompt_kernel, heads=heads, hist=hist, per_seq=per_seq, n_tiles=n_tiles)
    return pl.pallas_call(
        kern,
        grid=(n_tiles + 1,),
        in_specs=[
            pl.BlockSpec((1, tt, d), lambda i: (tile_a(i) // per_seq, tile_a(i) % per_seq, 0)),
            pl.BlockSpec((1, tt, d), lambda i: (tile_b(i) // per_seq, tile_b(i) % per_seq, 0)),
            pl.BlockSpec((3, 1, 1, d), lambda i: (0, tile_a(i) // per_seq, 0, 0)),
            pl.BlockSpec((3, 1, 1, d), lambda i: (0, tile_b(i) // per_seq, 0, 0)),
            _resident((1, d)),
            _resident(w_in.shape),
            _resident((1, ret_w)),
            _resident((n_taps, conv_w)),
            _resident((1, conv_w)),
            _resident((1, conv_w)),
            _resident((1, conv_w)),
            _resident(w_out.shape),
            pl.BlockSpec((tt, dh), lambda i: (tile_b(i) % per_seq, 0)),
            pl.BlockSpec((tt, dh), lambda i: (tile_b(i) % per_seq, 0)),
            _resident((heads, chunk, chunk)),
            _resident((heads, chunk, dh)),
            _resident((heads, chunk, dh)),
            _resident((heads, 1, dh)),
        ],
        out_specs=[
            pl.BlockSpec((1, tt, d), lambda i: (tile_b(i) // per_seq, tile_b(i) % per_seq, 0)),
            pl.BlockSpec((1, heads, dh, dh), lambda i: (tile_b(i) // per_seq, 0, 0, 0)),
            pl.BlockSpec((1, n_taps - 1, conv_w), lambda i: (tile_a(i) // per_seq, 0, 0)),
        ],
        out_shape=[
            jax.ShapeDtypeStruct(x.shape, F32),
            jax.ShapeDtypeStruct((nb, heads, dh, dh), F32),
            jax.ShapeDtypeStruct((nb, n_taps - 1, conv_w), F32),
        ],
        scratch_shapes=[
            pltpu.VMEM((heads, dh, dh), F32),
            pltpu.VMEM((hist + tt, conv_w), F32),
            pltpu.VMEM((tt, d), BF16),
            pltpu.VMEM((tt, conv_w), BF16),
            pltpu.VMEM((tt, ret_w), BF16),
        ],
        compiler_params=pltpu.CompilerParams(dimension_semantics=("arbitrary",),
                                             vmem_limit_bytes=VMEM_LIMIT_BYTES),
        name="mix_prompt",
    )(x, x, mods, mods, norm_gain.reshape(1, d), w_in, gn_gain.reshape(1, -1), dw_w, dw_b.reshape(1, -1),
      ln_g.reshape(1, -1), ln_b.reshape(1, -1), w_out, cos2, sin2, dmask, dq_b, dk_b, gc_b)


def _mix_sample_kernel(x_ref, mod_ref, ng_ref, win_ref, gn_ref, dww_ref, dwb_ref, lng_ref, lnb_ref,
                       wout_ref, cos_ref, sin_ref, dmask_ref, dq_ref, dk_ref, gc_ref, sin_state_ref,
                       cin_ref, o_ref, sout_ref, cout_ref, qd_scr, kd_scr, v_scr, cross_scr, win_scr,
                       *, heads, n_tok):
    tb, rows, d_ = x_ref.shape
    ret_w = gn_ref.shape[1]
    dh = ret_w // heads
    n_taps, conv_w = dww_ref.shape
    n_buf = n_taps - 1

    x = x_ref[...]
    shift, scale, gate = mod_ref[0], mod_ref[1], mod_ref[2]
    h = _rmsnorm(x, ng_ref[...]) * (1.0 + scale) + shift
    proj = _dot(h.astype(BF16).reshape(tb * rows, d_), win_ref[...]).reshape(tb, rows, -1)

    cos2, sin2 = cos_ref[...], sin_ref[...]
    k_scale = dh ** -0.5
    o_q, o_k, o_v, o_g = 0, ret_w, 2 * ret_w, 3 * ret_w
    o_a, o_b = 4 * ret_w, 4 * ret_w + conv_w

    inner_heads = []
    for hd in range(heads):
        lo = hd * dh
        q = _rotate(proj[:, :, o_q + lo:o_q + lo + dh], cos2, sin2)
        k = _rotate(proj[:, :, o_k + lo:o_k + lo + dh], cos2, sin2) * k_scale
        v = proj[:, :, o_v + lo:o_v + lo + dh]
        qd_scr[:, :, lo:lo + dh] = q * dq_ref[hd]
        kd_scr[:, :, lo:lo + dh] = k * dk_ref[hd]
        v_scr[:, :, lo:lo + dh] = v
        row_id = lax.broadcasted_iota(jnp.int32, (tb, rows, dh), 1)
        inner = jnp.zeros((tb, rows, dh), F32)
        for i in range(n_tok):
            s_i = jnp.sum(q[:, i:i + 1, :] * k, axis=-1, keepdims=True)
            p_i = s_i * dmask_ref[hd, i]
            o_i = jnp.sum(p_i * v, axis=1, keepdims=True)
            inner = jnp.where(row_id == i, o_i, inner)
        inner_heads.append(inner)

    zeros_pad = jnp.zeros((dh - rows, dh), F32)

    def per_sequence(b, carry):
        for hd in range(heads):
            lo = hd * dh
            state = sin_state_ref[b, hd]
            cross_scr[b, :, lo:lo + dh] = _dot(qd_scr[b, :, lo:lo + dh].astype(BF16), state.astype(BF16))
            k_sq = jnp.concatenate([kd_scr[b, :, lo:lo + dh], zeros_pad], axis=0).astype(BF16)
            v_sq = jnp.concatenate([v_scr[b, :, lo:lo + dh], zeros_pad], axis=0).astype(BF16)
            upd = lax.dot_general(k_sq, v_sq, (((0,), (0,)), ((), ())), preferred_element_type=F32)
            sout_ref[b, hd] = gc_ref[hd] * state + upd
        return carry

    lax.fori_loop(0, tb, per_sequence, 0)

    mixed = []
    for hd in range(heads):
        lo = hd * dh
        o = inner_heads[hd] + cross_scr[:, :, lo:lo + dh]
        gate_pre = proj[:, :, o_g + lo:o_g + lo + dh]
        mixed.append(_group_norm_gate(o, gate_pre, gn_ref[:, lo:lo + dh]).astype(BF16))

    a = proj[:, :, o_a:o_a + conv_w]
    b_ = proj[:, :, o_b:o_b + conv_w]
    tail0 = (n_buf // SUBLANES) * SUBLANES
    win_scr[:, tail0:, :] = jnp.zeros((tb, win_scr.shape[1] - tail0, conv_w), F32)
    win_scr[:, 0:n_buf, :] = cin_ref[...]
    win_scr[:, n_buf:n_buf + rows, :] = a * jax.nn.sigmoid(b_)
    y = _conv_taps(win_scr, dww_ref, rows, 0) + dwb_ref[...]
    mixed.append(_layer_norm_silu(y, lng_ref[...], lnb_ref[...]).astype(BF16))
    cout_ref[...] = win_scr[:, n_tok:n_tok + n_buf, :]

    mixed = jnp.concatenate(mixed, axis=-1).reshape(tb * rows, ret_w + conv_w)
    o_ref[...] = x + gate * _dot(mixed, wout_ref[...]).reshape(tb, rows, d_)


def _mix_sample(x, mods, norm_gain, w_in, gn_gain, dw_w, dw_b, ln_g, ln_b, w_out, state_ret, state_conv,
                n_tok):
    nb, rows, d = x.shape
    tb = SAMPLE_MIX_SEQ_TILE
    heads, dh = state_ret.shape[1], state_ret.shape[2]
    ret_w = heads * dh
    n_taps, conv_w = dw_w.shape
    n_buf = n_taps - 1
    win_rows = -(-(n_buf + rows + SUBLANES) // SUBLANES) * SUBLANES
    cos2, sin2 = _rotary_tables(PAST_LEN, rows, dh // 2)
    dmask, dq, dk, gc = _decay_tables(heads, n_tok)
    pad = rows - n_tok
    dmask_b = jnp.broadcast_to(jnp.pad(dmask, ((0, 0), (0, 0), (0, pad)))[:, :, :, None],
                               (heads, n_tok, rows, dh))
    dq_b = jnp.broadcast_to(jnp.pad(dq, ((0, 0), (0, pad)))[:, :, None], (heads, rows, dh))
    dk_b = jnp.broadcast_to(jnp.pad(dk, ((0, 0), (0, pad)))[:, :, None], (heads, rows, dh))
    gc_b = jnp.broadcast_to(gc[:, None, None], (heads, 1, dh))
    kern = functools.partial(_mix_sample_kernel, heads=heads, n_tok=n_tok)
    return pl.pallas_call(
        kern,
        grid=(nb // tb,),
        in_specs=[
            pl.BlockSpec((tb, rows, d), lambda i: (i, 0, 0)),
            pl.BlockSpec((3, tb, 1, d), lambda i: (0, i, 0, 0)),
            _resident((1, d)),
            _resident(w_in.shape),
            _resident((1, ret_w)),
            _resident((n_taps, conv_w)),
            _resident((1, conv_w)),
            _resident((1, conv_w)),
            _resident((1, conv_w)),
            _resident(w_out.shape),
            _resident((rows, dh)),
            _resident((rows, dh)),
            _resident((heads, n_tok, rows, dh)),
            _resident((heads, rows, dh)),
            _resident((heads, rows, dh)),
            _resident((heads, 1, dh)),
            pl.BlockSpec((tb, heads, dh, dh), lambda i: (i, 0, 0, 0)),
            pl.BlockSpec((tb, n_buf, conv_w), lambda i: (i, 0, 0)),
        ],
        out_specs=[
            pl.BlockSpec((tb, rows, d), lambda i: (i, 0, 0)),
            pl.BlockSpec((tb, heads, dh, dh), lambda i: (i, 0, 0, 0)),
            pl.BlockSpec((tb, n_buf, conv_w), lambda i: (i, 0, 0)),
        ],
        out_shape=[
            jax.ShapeDtypeStruct(x.shape, F32),
            jax.ShapeDtypeStruct(state_ret.shape, F32),
            jax.ShapeDtypeStruct(state_conv.shape, F32),
        ],
        scratch_shapes=[
            pltpu.VMEM((tb, rows, ret_w), F32),
            pltpu.VMEM((tb, rows, ret_w), F32),
            pltpu.VMEM((tb, rows, ret_w), F32),
            pltpu.VMEM((tb, rows, ret_w), F32),
            pltpu.VMEM((tb, win_rows, conv_w), F32),
        ],
        compiler_params=pltpu.CompilerParams(dimension_semantics=("arbitrary",),
                                             vmem_limit_bytes=VMEM_LIMIT_BYTES),
        name="mix_sample",
    )(x, mods, norm_gain.reshape(1, d), w_in, gn_gain.reshape(1, -1), dw_w, dw_b.reshape(1, -1),
      ln_g.reshape(1, -1), ln_b.reshape(1, -1), w_out, cos2, sin2, dmask_b, dq_b, dk_b, gc_b,
      state_ret, state_conv)


def _layer(xp, xs, mods_p, mods_s, sret, sconv, lw, final_gain, n_tok):
    (norm_ffn1, w1g, w1u, w1d, norm_mix, w_in, gn_gain, dw_w, dw_b, ln_g, ln_b, w_out,
     norm_ffn2, w2g, w2u, w2d) = lw
    nb, seq, d = xp.shape
    n_tok, ns, _ = xs.shape
    heads = sret.shape[1]
    mods_p = mods_p.reshape(N_MOD, nb, 1, d)
    mods_s_tm = mods_s.reshape(N_MOD, 1, ns, d)

    xp, xs = _ffn(xp, xs, mods_p[0:3], mods_s_tm[0:3], norm_ffn1, w1g, w1u, w1d, None)
    xp, ret_p, conv_p = _mix_prompt(xp, mods_p[3:6], norm_mix, w_in, gn_gain, dw_w, dw_b, ln_g, ln_b,
                                    w_out, heads)
    xs_seq = jnp.pad(xs.transpose(1, 0, 2), ((0, 0), (0, SUBLANES - n_tok), (0, 0)))
    xs_seq, ret_s, conv_s = _mix_sample(xs_seq, mods_s[3:6].reshape(3, ns, 1, d), norm_mix, w_in, gn_gain,
                                        dw_w, dw_b, ln_g, ln_b, w_out, sret, sconv, n_tok)
    xs = xs_seq[:, :n_tok].transpose(1, 0, 2)
    xp, xs = _ffn(xp, xs, mods_p[6:9], mods_s_tm[6:9], norm_ffn2, w2g, w2u, w2d, final_gain)
    return xp, xs, ret_p, conv_p, ret_s, conv_s


def kernel(x_prompt, x_sample, c_prompt, c_sample, state_ret, state_conv, norm_ffn1, ffn1_w_gate,
           ffn1_w_up, ffn1_w_down, norm_mix, w_in, ret_gn_gain, dw_w, dw_b, conv_ln_gain, conv_ln_bias,
           w_out, norm_ffn2, ffn2_w_gate, ffn2_w_up, ffn2_w_down, w_ada, b_ada, norm_final):
    depth = w_in.shape[0]
    nb = x_prompt.shape[0]
    ns, n_tok, d = x_sample.shape
    assert n_tok <= SUBLANES and x_prompt.shape[1] % RET_CHUNK == 0

    xp = x_prompt
    xs = x_sample.transpose(1, 0, 2)
    c_all = jnp.concatenate([c_prompt, c_sample], axis=0)

    ret_p, conv_p, ret_s, conv_s = [], [], [], []
    for l in range(depth):
        ada = _ada(c_all, w_ada[l], b_ada[l])
        lw = (norm_ffn1[l], ffn1_w_gate[l], ffn1_w_up[l], ffn1_w_down[l], norm_mix[l],
              w_in[l].astype(BF16), ret_gn_gain[l], dw_w[l], dw_b[l], conv_ln_gain[l], conv_ln_bias[l],
              w_out[l].astype(BF16), norm_ffn2[l], ffn2_w_gate[l], ffn2_w_up[l], ffn2_w_down[l])
        final_gain = norm_final if l == depth - 1 else None
        xp, xs, rp, cp, rs, cs = _layer(xp, xs, ada[:, :nb], ada[:, nb:], state_ret[l], state_conv[l], lw,
                                        final_gain, n_tok)
        ret_p.append(rp)
        conv_p.append(cp)
        ret_s.append(rs)
        conv_s.append(cs)

    return (xp, xs.transpose(1, 0, 2), jnp.stack(ret_p), jnp.stack(conv_p), jnp.stack(ret_s),
            jnp.stack(conv_s))
```

```python
import functools
import math

import jax
import jax.numpy as jnp
from jax import lax
from jax.experimental import pallas as pl
from jax.experimental.pallas import tpu as pltpu

F32 = jnp.float32
BF16 = jnp.bfloat16

PAST_LEN = 16384
RET_CHUNK = 256
ROPE_BASE = 10000.0
EPS = 1e-6
N_MOD = 9

SUBLANES = 8
LANES = 128
VMEM_LIMIT_BYTES = 56 * 1024 * 1024

FFN_TOKEN_TILE = 1024
MIX_TOKEN_TILE = 512
SAMPLE_MIX_SEQ_TILE = 16
FF_CHUNK = 256


def _resident(shape):
    n = len(shape)
    return pl.BlockSpec(shape, lambda *_: (0,) * n, pipeline_mode=pl.Buffered(1))


def _rmsnorm(x, gain):
    ms = jnp.mean(x * x, axis=-1, keepdims=True)
    return x * lax.rsqrt(ms + EPS) * gain


def _dot(a, b):
    return jnp.dot(a, b, preferred_element_type=F32)


def _ada_kernel(c_ref, w_ref, b_ref, o_ref):
    c = c_ref[...]
    h = (c * jax.nn.sigmoid(c)).astype(BF16)
    o_ref[0] = _dot(h, w_ref[...].astype(BF16)) + b_ref[...]


def _ada(c_all, w_ada, b_ada):
    nb, d = c_all.shape
    return pl.pallas_call(
        _ada_kernel,
        grid=(N_MOD,),
        in_specs=[
            pl.BlockSpec((nb, d), lambda j: (0, 0)),
            pl.BlockSpec((d, d), lambda j: (0, j)),
            pl.BlockSpec((1, d), lambda j: (0, j)),
        ],
        out_specs=pl.BlockSpec((1, nb, d), lambda j: (j, 0, 0)),
        out_shape=jax.ShapeDtypeStruct((N_MOD, nb, d), F32),
        compiler_params=pltpu.CompilerParams(dimension_semantics=("arbitrary",),
                                             vmem_limit_bytes=VMEM_LIMIT_BYTES),
        name="ada",
    )(c_all, w_ada, b_ada.reshape(1, -1))


def _ffn_kernel(xp_ref, xs_ref, modp_ref, mods_ref, ng_ref, wg_ref, wu_ref, wd_ref, *rest,
                n_stage, n_prompt, final_norm):
    if final_norm:
        fg_ref, op_ref, os_ref, wg_scr, wu_scr, wd_scr, act_scr = rest
    else:
        op_ref, os_ref, wg_scr, wu_scr, wd_scr, act_scr = rest
    i = pl.program_id(0)
    n_chunks, _, ffc = wg_scr.shape

    @pl.when(i < n_stage)
    def _():
        wg_scr[i] = wg_ref[...].astype(BF16)
        wu_scr[i] = wu_ref[...].astype(BF16)
        wd_scr[pl.ds(pl.multiple_of(i * ffc, ffc), ffc), :] = wd_ref[...].astype(BF16)

    def tile(x_ref, mod_ref, o_ref):
        x = x_ref[...]
        g_, r_, d_ = x.shape
        shift, scale, gate = mod_ref[0], mod_ref[1], mod_ref[2]
        h = _rmsnorm(x, ng_ref[...]) * (1.0 + scale) + shift
        rows = g_ * r_
        hb = h.astype(BF16).reshape(rows, d_)
        for c in range(n_chunks):
            g = _dot(hb, wg_scr[c])
            u = _dot(hb, wu_scr[c])
            act_scr[0:rows, c * ffc:(c + 1) * ffc] = (g * jax.nn.sigmoid(g) * u).astype(BF16)
        y = _dot(act_scr[0:rows, :], wd_scr[...]).reshape(g_, r_, d_)
        out = x + 0.5 * gate * y
        if final_norm:
            out = _rmsnorm(out, fg_ref[...])
        o_ref[...] = out

    @pl.when(jnp.logical_and(i >= n_stage, i < n_stage + n_prompt))
    def _():
        tile(xp_ref, modp_ref, op_ref)

    @pl.when(i == n_stage + n_prompt)
    def _():
        tile(xs_ref, mods_ref, os_ref)


def _ffn(xp, xs, mods_p, mods_s, norm_gain, wg, wu, wd, final_gain):
    nb, seq, d = xp.shape
    d_ff = wg.shape[1]
    tm = FFN_TOKEN_TILE
    ffc = FF_CHUNK
    assert seq % tm == 0 and d_ff % ffc == 0 and xs.shape[0] * xs.shape[1] <= tm
    per_seq = seq // tm
    n_stage, n_prompt = d_ff // ffc, nb * per_seq
    final_norm = final_gain is not None

    def prompt_tile(i):
        return jnp.clip(i - n_stage, 0, n_prompt - 1)

    def stage(i):
        return jnp.minimum(i, n_stage - 1)

    x_spec = pl.BlockSpec((1, tm, d), lambda i: (prompt_tile(i) // per_seq, prompt_tile(i) % per_seq, 0))
    in_specs = [
        x_spec,
        _resident(xs.shape),
        pl.BlockSpec((3, 1, 1, d), lambda i: (0, prompt_tile(i) // per_seq, 0, 0)),
        _resident(mods_s.shape),
        _resident((1, d)),
        pl.BlockSpec((d, ffc), lambda i: (0, stage(i))),
        pl.BlockSpec((d, ffc), lambda i: (0, stage(i))),
        pl.BlockSpec((ffc, d), lambda i: (stage(i), 0)),
    ]
    args = [xp, xs, mods_p, mods_s, norm_gain.reshape(1, d), wg, wu, wd]
    if final_norm:
        in_specs.append(_resident((1, d)))
        args.append(final_gain.reshape(1, d))
    return pl.pallas_call(
        functools.partial(_ffn_kernel, n_stage=n_stage, n_prompt=n_prompt, final_norm=final_norm),
        grid=(n_stage + n_prompt + 1,),
        in_specs=in_specs,
        out_specs=[x_spec, pl.BlockSpec(xs.shape, lambda i: (0, 0, 0))],
        out_shape=[jax.ShapeDtypeStruct(xp.shape, F32), jax.ShapeDtypeStruct(xs.shape, F32)],
        scratch_shapes=[
            pltpu.VMEM((n_stage, d, ffc), BF16),
            pltpu.VMEM((n_stage, d, ffc), BF16),
            pltpu.VMEM((d_ff, d), BF16),
            pltpu.VMEM((tm, d_ff), BF16),
        ],
        compiler_params=pltpu.CompilerParams(dimension_semantics=("arbitrary",),
                                             vmem_limit_bytes=VMEM_LIMIT_BYTES),
        name="ffn_final" if final_norm else "ffn",
    )(*args)


def _rotate(xh, cos2, sin2):
    return xh * cos2 + pltpu.roll(xh, xh.shape[-1] // 2, axis=xh.ndim - 1) * sin2


def _group_norm_gate(o, gate_pre, gain):
    mu = jnp.mean(o, axis=-1, keepdims=True)
    ctr = o - mu
    var = jnp.mean(ctr * ctr, axis=-1, keepdims=True)
    return gate_pre * jax.nn.sigmoid(gate_pre) * (ctr * lax.rsqrt(var + EPS) * gain)


def _layer_norm_silu(y, gain, bias):
    mu = jnp.mean(y, axis=-1, keepdims=True)
    ctr = y - mu
    var = jnp.mean(ctr * ctr, axis=-1, keepdims=True)
    yn = ctr * lax.rsqrt(var + EPS) * gain + bias
    return yn * jax.nn.sigmoid(yn)


def _conv_taps(win_ref, dww_ref, n_out, lead):
    n_taps = dww_ref.shape[0]
    pre = (slice(None),) * (len(win_ref.shape) - 2)
    total = None
    for phase in range(SUBLANES):
        offsets = [o for o in range(lead, lead + n_taps) if o % SUBLANES == phase]
        if not offsets:
            continue
        rows = n_out if phase == 0 else n_out + SUBLANES
        phase_sum = None
        for o in offsets:
            a = o - phase
            term = win_ref[pre + (slice(a, a + rows), slice(None))] * dww_ref[o - lead:o - lead + 1, :]
            phase_sum = term if phase_sum is None else phase_sum + term
        if phase:
            phase_sum = phase_sum[pre + (slice(phase, phase + n_out), slice(None))]
        total = phase_sum if total is None else total + phase_sum
    return total


def _conv_taps_slab(win_ref, dww_ref, s, n_out, lead):
    acc = None
    for j in range(dww_ref.shape[0]):
        term = win_ref[s, pl.ds(lead + j, n_out), :] * dww_ref[j:j + 1, s * LANES:(s + 1) * LANES]
        acc = term if acc is None else acc + term
    return acc


def _mix_prompt_kernel(xa_ref, xb_ref, moda_ref, modb_ref, ng_ref, win_ref, gn_ref, dww_ref, dwb_ref,
                       lng_ref, lnb_ref, wout_ref, cos_ref, sin_ref, dmask_ref, dq_ref, dk_ref, gc_ref,
                       o_ref, sout_ref, cout_ref, s_scr, u_scr, hb_scr, conv_scr, mix_scr,
                       *, heads, hist, per_seq, n_tiles):
    i = pl.program_id(0)
    tt, d_ = xa_ref.shape[1], xa_ref.shape[2]
    ret_w = gn_ref.shape[1]
    dh = ret_w // heads
    n_taps, conv_w = dww_ref.shape
    k_scale = dh ** -0.5
    o_a, o_b = 4 * ret_w, 4 * ret_w + conv_w
    n_slabs = u_scr.shape[0]

    slot_a = lax.rem(i, 2)
    slot_b = 1 - slot_a

    def stage_a_project():
        x = xa_ref[0]
        shift, scale = moda_ref[0, 0], moda_ref[1, 0]
        h = _rmsnorm(x, ng_ref[...]) * (1.0 + scale) + shift
        hb = h.astype(BF16)
        hb_scr[slot_a] = hb
        ab = _dot(hb, win_ref[:, o_a:o_b + conv_w])
        u = ab[:, :conv_w] * jax.nn.sigmoid(ab[:, conv_w:])
        for s in range(n_slabs):
            u_scr[s, hist:hist + tt, :] = u[:, s * LANES:(s + 1) * LANES]

    def stage_a_conv_slab(s):
        return _conv_taps_slab(u_scr, dww_ref, s, tt, hist - (n_taps - 1))

    def stage_a_finish(slabs):
        y = jnp.concatenate(slabs, axis=-1) + dwb_ref[...]
        conv_scr[slot_a] = _layer_norm_silu(y, lng_ref[...], lnb_ref[...]).astype(BF16)
        for s in range(n_slabs):
            u_scr[s, 0:hist, :] = u_scr[s, tt:tt + hist, :]

    def stage_b_project():
        return _dot(hb_scr[slot_b], win_ref[:, 0:o_a])

    def stage_b_head(proj, hd):
        lo = hd * dh
        cos2, sin2 = cos_ref[...], sin_ref[...]
        q = _rotate(proj[:, 4 * lo:4 * lo + dh], cos2, sin2)
        k = _rotate(proj[:, 4 * lo + dh:4 * lo + 2 * dh], cos2, sin2) * k_scale
        v = proj[:, 4 * lo + 2 * dh:4 * lo + 3 * dh]
        gate_pre = proj[:, 4 * lo + 3 * dh:4 * lo + 4 * dh]
        dmask, dq, dk, gc = dmask_ref[hd], dq_ref[hd], dk_ref[hd], gc_ref[hd]
        state = s_scr[hd]
        outs = []
        for c0 in range(0, tt, RET_CHUNK):
            qc, kc = q[c0:c0 + RET_CHUNK], k[c0:c0 + RET_CHUNK]
            vb = v[c0:c0 + RET_CHUNK].astype(BF16)
            scores = lax.dot_general(qc.astype(BF16), kc.astype(BF16), (((1,), (1,)), ((), ())),
                                     preferred_element_type=F32)
            inner = _dot((scores * dmask).astype(BF16), vb)
            cross = _dot((qc * dq).astype(BF16), state.astype(BF16))
            outs.append(inner + cross)
            state = gc * state + lax.dot_general((kc * dk).astype(BF16), vb, (((0,), (0,)), ((), ())),
                                                 preferred_element_type=F32)
        s_scr[hd] = state
        o = jnp.concatenate(outs, axis=0) if len(outs) > 1 else outs[0]
        mix_scr[:, lo:lo + dh] = _group_norm_gate(o, gate_pre, gn_ref[:, lo:lo + dh]).astype(BF16)

    def stage_b_finish():
        mixed = (_dot(mix_scr[...], wout_ref[0:ret_w, :])
                 + _dot(conv_scr[slot_b], wout_ref[ret_w:ret_w + conv_w, :]))
        o_ref[0] = xb_ref[0] + modb_ref[2, 0] * mixed

    def run(do_a, do_b):
        if do_a:
            stage_a_project()
        if do_b:
            proj = stage_b_project()
            for hd in range(heads):
                stage_b_head(proj, hd)
            stage_b_finish()
        if do_a:
            stage_a_finish([stage_a_conv_slab(s) for s in range(n_slabs)])

    has_a = i < n_tiles
    has_b = i >= 1
    pos_a = lax.rem(i, per_seq)
    pos_b = lax.rem(i + per_seq - 1, per_seq)

    @pl.when(jnp.logical_and(has_a, pos_a == 0))
    def _():
        u_scr[:, 0:hist, :] = jnp.zeros((n_slabs, hist, LANES), F32)

    @pl.when(jnp.logical_and(has_b, pos_b == 0))
    def _():
        s_scr[...] = jnp.zeros_like(s_scr)

    @pl.when(i == 0)
    def _():
        run(True, False)

    @pl.when(jnp.logical_and(has_a, has_b))
    def _():
        run(True, True)

    @pl.when(i == n_tiles)
    def _():
        run(False, True)

    @pl.when(jnp.logical_and(has_a, pos_a == per_seq - 1))
    def _():
        for s in range(n_slabs):
            cout_ref[0, :, s * LANES:(s + 1) * LANES] = u_scr[s, hist - (n_taps - 1):hist, :]

    @pl.when(jnp.logical_and(has_b, pos_b == per_seq - 1))
    def _():
        sout_ref[0] = s_scr[...]


def _decay_tables(heads, chunk):
    lg = jnp.log(1.0 - 2.0 ** (-5.0 - jnp.arange(heads, dtype=F32)))
    idx = jnp.arange(chunk, dtype=F32)
    diff = idx[:, None] - idx[None, :]
    dmask = jnp.where(diff[None] >= 0, jnp.exp(lg[:, None, None] * jnp.maximum(diff, 0.0)[None]), 0.0)
    dq = jnp.exp(lg[:, None] * (idx[None, :] + 1.0))
    dk = jnp.exp(lg[:, None] * (chunk - 1.0 - idx[None, :]))
    gc = jnp.exp(lg * chunk)
    return dmask, dq, dk, gc


def _rotary_tables(pos0, n_pos, half):
    inv = ROPE_BASE ** (-jnp.arange(half, dtype=F32) / half)
    pos = (pos0 + jnp.arange(n_pos, dtype=jnp.int32)).astype(F32)
    ang = pos[:, None] * inv[None, :]
    cos, sin = jnp.cos(ang), jnp.sin(ang)
    return jnp.concatenate([cos, cos], axis=-1), jnp.concatenate([-sin, sin], axis=-1)


def _mix_prompt(x, mods, norm_gain, w_in, gn_gain, dw_w, dw_b, ln_g, ln_b, w_out, heads):
    nb, seq, d = x.shape
    tt = MIX_TOKEN_TILE
    ret_w = gn_gain.shape[0]
    dh = ret_w // heads
    n_taps, conv_w = dw_w.shape
    hist = -(-(n_taps - 1) // SUBLANES) * SUBLANES
    chunk = RET_CHUNK
    cos2, sin2 = _rotary_tables(0, seq, dh // 2)
    dmask, dq, dk, gc = _decay_tables(heads, chunk)
    dq_b = jnp.broadcast_to(dq[:, :, None], (heads, chunk, dh))
    dk_b = jnp.broadcast_to(dk[:, :, None], (heads, chunk, dh))
    gc_b = jnp.broadcast_to(gc[:, None, None], (heads, 1, dh))
    per_seq = seq // tt
    n_tiles = nb * per_seq

    def tile_a(i):
        return jnp.minimum(i, n_tiles - 1)

    def tile_b(i):
        return jnp.maximum(i - 1, 0)

    kern = functools.partial(_mix_prompt_kernel, heads=heads, hist=hist, per_seq=per_seq, n_tiles=n_tiles)
    return pl.pallas_call(
        kern,
        grid=(n_tiles + 1,),
        in_specs=[
            pl.BlockSpec((1, tt, d), lambda i: (tile_a(i) // per_seq, tile_a(i) % per_seq, 0)),
            pl.BlockSpec((1, tt, d), lambda i: (tile_b(i) // per_seq, tile_b(i) % per_seq, 0)),
            pl.BlockSpec((3, 1, 1, d), lambda i: (0, tile_a(i) // per_seq, 0, 0)),
            pl.BlockSpec((3, 1, 1, d), lambda i: (0, tile_b(i) // per_seq, 0, 0)),
            _resident((1, d)),
            _resident(w_in.shape),
            _resident((1, ret_w)),
            _resident((n_taps, conv_w)),
            _resident((1, conv_w)),
            _resident((1, conv_w)),
            _resident((1, conv_w)),
            _resident(w_out.shape),
            pl.BlockSpec((tt, dh), lambda i: (tile_b(i) % per_seq, 0)),
            pl.BlockSpec((tt, dh), lambda i: (tile_b(i) % per_seq, 0)),
            _resident((heads, chunk, chunk)),
            _resident((heads, chunk, dh)),
            _resident((heads, chunk, dh)),
            _resident((heads, 1, dh)),
        ],
        out_specs=[
            pl.BlockSpec((1, tt, d), lambda i: (tile_b(i) // per_seq, tile_b(i) % per_seq, 0)),
            pl.BlockSpec((1, heads, dh, dh), lambda i: (tile_b(i) // per_seq, 0, 0, 0)),
            pl.BlockSpec((1, n_taps - 1, conv_w), lambda i: (tile_a(i) // per_seq, 0, 0)),
        ],
        out_shape=[
            jax.ShapeDtypeStruct(x.shape, F32),
            jax.ShapeDtypeStruct((nb, heads, dh, dh), F32),
            jax.ShapeDtypeStruct((nb, n_taps - 1, conv_w), F32),
        ],
        scratch_shapes=[
            pltpu.VMEM((heads, dh, dh), F32),
            pltpu.VMEM((conv_w // LANES, hist + tt, LANES), F32),
            pltpu.VMEM((2, tt, d), BF16),
            pltpu.VMEM((2, tt, conv_w), BF16),
            pltpu.VMEM((tt, ret_w), BF16),
        ],
        compiler_params=pltpu.CompilerParams(dimension_semantics=("arbitrary",),
                                             vmem_limit_bytes=VMEM_LIMIT_BYTES),
        name="mix_prompt",
    )(x, x, mods, mods, norm_gain.reshape(1, d), w_in, gn_gain.reshape(1, -1), dw_w, dw_b.reshape(1, -1),
      ln_g.reshape(1, -1), ln_b.reshape(1, -1), w_out, cos2, sin2, dmask, dq_b, dk_b, gc_b)


def _mix_sample_kernel(x_ref, mod_ref, ng_ref, win_ref, gn_ref, dww_ref, dwb_ref, lng_ref, lnb_ref,
                       wout_ref, cos_ref, sin_ref, dmask_ref, dq_ref, dk_ref, gc_ref, sin_state_ref,
                       cin_ref, o_ref, sout_ref, cout_ref, qd_scr, kd_scr, v_scr, cross_scr, win_scr,
                       *, heads, n_tok):
    tb, rows, d_ = x_ref.shape
    ret_w = gn_ref.shape[1]
    dh = ret_w // heads
    n_taps, conv_w = dww_ref.shape
    n_buf = n_taps - 1

    x = x_ref[...]
    shift, scale, gate = mod_ref[0], mod_ref[1], mod_ref[2]
    h = _rmsnorm(x, ng_ref[...]) * (1.0 + scale) + shift
    proj = _dot(h.astype(BF16).reshape(tb * rows, d_), win_ref[...]).reshape(tb, rows, -1)

    cos2, sin2 = cos_ref[...], sin_ref[...]
    k_scale = dh ** -0.5
    o_a, o_b = 4 * ret_w, 4 * ret_w + conv_w

    inner_heads = []
    for hd in range(heads):
        lo = hd * dh
        q = _rotate(proj[:, :, 4 * lo:4 * lo + dh], cos2, sin2)
        k = _rotate(proj[:, :, 4 * lo + dh:4 * lo + 2 * dh], cos2, sin2) * k_scale
        v = proj[:, :, 4 * lo + 2 * dh:4 * lo + 3 * dh]
        qd_scr[:, :, lo:lo + dh] = q * dq_ref[hd]
        kd_scr[:, :, lo:lo + dh] = k * dk_ref[hd]
        v_scr[:, :, lo:lo + dh] = v
        row_id = lax.broadcasted_iota(jnp.int32, (tb, rows, dh), 1)
        inner = jnp.zeros((tb, rows, dh), F32)
        for i in range(n_tok):
            s_i = jnp.sum(q[:, i:i + 1, :] * k, axis=-1, keepdims=True)
            p_i = s_i * dmask_ref[hd, i]
            o_i = jnp.sum(p_i * v, axis=1, keepdims=True)
            inner = jnp.where(row_id == i, o_i, inner)
        inner_heads.append(inner)

    zeros_pad = jnp.zeros((dh - rows, dh), F32)

    def per_sequence(b, carry):
        for hd in range(heads):
            lo = hd * dh
            state = sin_state_ref[b, hd]
            cross_scr[b, :, lo:lo + dh] = _dot(qd_scr[b, :, lo:lo + dh].astype(BF16), state.astype(BF16))
            k_sq = jnp.concatenate([kd_scr[b, :, lo:lo + dh], zeros_pad], axis=0).astype(BF16)
            v_sq = jnp.concatenate([v_scr[b, :, lo:lo + dh], zeros_pad], axis=0).astype(BF16)
            upd = lax.dot_general(k_sq, v_sq, (((0,), (0,)), ((), ())), preferred_element_type=F32)
            sout_ref[b, hd] = gc_ref[hd] * state + upd
        return carry

    lax.fori_loop(0, tb, per_sequence, 0)

    mixed = []
    for hd in range(heads):
        lo = hd * dh
        o = inner_heads[hd] + cross_scr[:, :, lo:lo + dh]
        gate_pre = proj[:, :, 4 * lo + 3 * dh:4 * lo + 4 * dh]
        mixed.append(_group_norm_gate(o, gate_pre, gn_ref[:, lo:lo + dh]).astype(BF16))

    a = proj[:, :, o_a:o_a + conv_w]
    b_ = proj[:, :, o_b:o_b + conv_w]
    tail0 = (n_buf // SUBLANES) * SUBLANES
    win_scr[:, tail0:, :] = jnp.zeros((tb, win_scr.shape[1] - tail0, conv_w), F32)
    win_scr[:, 0:n_buf, :] = cin_ref[...]
    win_scr[:, n_buf:n_buf + rows, :] = a * jax.nn.sigmoid(b_)
    y = _conv_taps(win_scr, dww_ref, rows, 0) + dwb_ref[...]
    mixed.append(_layer_norm_silu(y, lng_ref[...], lnb_ref[...]).astype(BF16))
    cout_ref[...] = win_scr[:, n_tok:n_tok + n_buf, :]

    mixed = jnp.concatenate(mixed, axis=-1).reshape(tb * rows, ret_w + conv_w)
    o_ref[...] = x + gate * _dot(mixed, wout_ref[...]).reshape(tb, rows, d_)


def _mix_sample(x, mods, norm_gain, w_in, gn_gain, dw_w, dw_b, ln_g, ln_b, w_out, state_ret, state_conv,
                n_tok):
    nb, rows, d = x.shape
    tb = SAMPLE_MIX_SEQ_TILE
    heads, dh = state_ret.shape[1], state_ret.shape[2]
    ret_w = heads * dh
    n_taps, conv_w = dw_w.shape
    n_buf = n_taps - 1
    win_rows = -(-(n_buf + rows + SUBLANES) // SUBLANES) * SUBLANES
    cos2, sin2 = _rotary_tables(PAST_LEN, rows, dh // 2)
    dmask, dq, dk, gc = _decay_tables(heads, n_tok)
    pad = rows - n_tok
    dmask_b = jnp.broadcast_to(jnp.pad(dmask, ((0, 0), (0, 0), (0, pad)))[:, :, :, None],
                               (heads, n_tok, rows, dh))
    dq_b = jnp.broadcast_to(jnp.pad(dq, ((0, 0), (0, pad)))[:, :, None], (heads, rows, dh))
    dk_b = jnp.broadcast_to(jnp.pad(dk, ((0, 0), (0, pad)))[:, :, None], (heads, rows, dh))
    gc_b = jnp.broadcast_to(gc[:, None, None], (heads, 1, dh))
    kern = functools.partial(_mix_sample_kernel, heads=heads, n_tok=n_tok)
    return pl.pallas_call(
        kern,
        grid=(nb // tb,),
        in_specs=[
            pl.BlockSpec((tb, rows, d), lambda i: (i, 0, 0)),
            pl.BlockSpec((3, tb, 1, d), lambda i: (0, i, 0, 0)),
            _resident((1, d)),
            _resident(w_in.shape),
            _resident((1, ret_w)),
            _resident((n_taps, conv_w)),
            _resident((1, conv_w)),
            _resident((1, conv_w)),
            _resident((1, conv_w)),
            _resident(w_out.shape),
            _resident((rows, dh)),
            _resident((rows, dh)),
            _resident((heads, n_tok, rows, dh)),
            _resident((heads, rows, dh)),
            _resident((heads, rows, dh)),
            _resident((heads, 1, dh)),
            pl.BlockSpec((tb, heads, dh, dh), lambda i: (i, 0, 0, 0)),
            pl.BlockSpec((tb, n_buf, conv_w), lambda i: (i, 0, 0)),
        ],
        out_specs=[
            pl.BlockSpec((tb, rows, d), lambda i: (i, 0, 0)),
            pl.BlockSpec((tb, heads, dh, dh), lambda i: (i, 0, 0, 0)),
            pl.BlockSpec((tb, n_buf, conv_w), lambda i: (i, 0, 0)),
        ],
        out_shape=[
            jax.ShapeDtypeStruct(x.shape, F32),
            jax.ShapeDtypeStruct(state_ret.shape, F32),
            jax.ShapeDtypeStruct(state_conv.shape, F32),
        ],
        scratch_shapes=[
            pltpu.VMEM((tb, rows, ret_w), F32),
            pltpu.VMEM((tb, rows, ret_w), F32),
            pltpu.VMEM((tb, rows, ret_w), F32),
            pltpu.VMEM((tb, rows, ret_w), F32),
            pltpu.VMEM((tb, win_rows, conv_w), F32),
        ],
        compiler_params=pltpu.CompilerParams(dimension_semantics=("arbitrary",),
                                             vmem_limit_bytes=VMEM_LIMIT_BYTES),
        name="mix_sample",
    )(x, mods, norm_gain.reshape(1, d), w_in, gn_gain.reshape(1, -1), dw_w, dw_b.reshape(1, -1),
      ln_g.reshape(1, -1), ln_b.reshape(1, -1), w_out, cos2, sin2, dmask_b, dq_b, dk_b, gc_b,
      state_ret, state_conv)


def _w_in_by_head(w_in, heads, ret_w):
    d = w_in.shape[0]
    dh = ret_w // heads
    qkvg = w_in[:, :4 * ret_w].reshape(d, 4, heads, dh).transpose(0, 2, 1, 3).reshape(d, 4 * ret_w)
    return jnp.concatenate([qkvg, w_in[:, 4 * ret_w:]], axis=1).astype(BF16)


def _layer(xp, xs, mods_p, mods_s, sret, sconv, lw, final_gain, n_tok):
    (norm_ffn1, w1g, w1u, w1d, norm_mix, w_in, gn_gain, dw_w, dw_b, ln_g, ln_b, w_out,
     norm_ffn2, w2g, w2u, w2d) = lw
    nb, seq, d = xp.shape
    n_tok, ns, _ = xs.shape
    heads = sret.shape[1]
    mods_p = mods_p.reshape(N_MOD, nb, 1, d)
    mods_s_tm = mods_s.reshape(N_MOD, 1, ns, d)

    xp, xs = _ffn(xp, xs, mods_p[0:3], mods_s_tm[0:3], norm_ffn1, w1g, w1u, w1d, None)
    xp, ret_p, conv_p = _mix_prompt(xp, mods_p[3:6], norm_mix, w_in, gn_gain, dw_w, dw_b, ln_g, ln_b,
                                    w_out, heads)
    xs_seq = jnp.pad(xs.transpose(1, 0, 2), ((0, 0), (0, SUBLANES - n_tok), (0, 0)))
    xs_seq, ret_s, conv_s = _mix_sample(xs_seq, mods_s[3:6].reshape(3, ns, 1, d), norm_mix, w_in, gn_gain,
                                        dw_w, dw_b, ln_g, ln_b, w_out, sret, sconv, n_tok)
    xs = xs_seq[:, :n_tok].transpose(1, 0, 2)
    xp, xs = _ffn(xp, xs, mods_p[6:9], mods_s_tm[6:9], norm_ffn2, w2g, w2u, w2d, final_gain)
    return xp, xs, ret_p, conv_p, ret_s, conv_s


def kernel(x_prompt, x_sample, c_prompt, c_sample, state_ret, state_conv, norm_ffn1, ffn1_w_gate,
           ffn1_w_up, ffn1_w_down, norm_mix, w_in, ret_gn_gain, dw_w, dw_b, conv_ln_gain, conv_ln_bias,
           w_out, norm_ffn2, ffn2_w_gate, ffn2_w_up, ffn2_w_down, w_ada, b_ada, norm_final):
    depth = w_in.shape[0]
    nb = x_prompt.shape[0]
    ns, n_tok, d = x_sample.shape
    assert n_tok <= SUBLANES and x_prompt.shape[1] % RET_CHUNK == 0

    xp = x_prompt
    xs = x_sample.transpose(1, 0, 2)
    c_all = jnp.concatenate([c_prompt, c_sample], axis=0)

    ret_p, conv_p, ret_s, conv_s = [], [], [], []
    for l in range(depth):
        ada = _ada(c_all, w_ada[l], b_ada[l])
        lw = (norm_ffn1[l], ffn1_w_gate[l], ffn1_w_up[l], ffn1_w_down[l], norm_mix[l],
              _w_in_by_head(w_in[l], state_ret.shape[2], ret_gn_gain.shape[1]), ret_gn_gain[l], dw_w[l],
              dw_b[l], conv_ln_gain[l], conv_ln_bias[l],
              w_out[l].astype(BF16), norm_ffn2[l], ffn2_w_gate[l], ffn2_w_up[l], ffn2_w_down[l])
        final_gain = norm_final if l == depth - 1 else None
        xp, xs, rp, cp, rs, cs = _layer(xp, xs, ada[:, :nb], ada[:, nb:], state_ret[l], state_conv[l], lw,
                                        final_gain, n_tok)
        ret_p.append(rp)
        conv_p.append(cp)
        ret_s.append(rs)
        conv_s.append(cs)

    return (xp, xs.transpose(1, 0, 2), jnp.stack(ret_p), jnp.stack(conv_p), jnp.stack(ret_s),
            jnp.stack(conv_s))
```

```python
import functools
import math

import jax
import jax.numpy as jnp
from jax import lax
from jax.experimental import pallas as pl
from jax.experimental.pallas import tpu as pltpu

F32 = jnp.float32
BF16 = jnp.bfloat16

PAST_LEN = 16384
RET_CHUNK = 256
ROPE_BASE = 10000.0
EPS = 1e-6
N_MOD = 9

SUBLANES = 8
LANES = 128
VMEM_LIMIT_BYTES = 56 * 1024 * 1024

FFN_TOKEN_TILE = 512
MIX_TOKEN_TILE = 512
SAMPLE_MIX_SEQ_TILE = 16
FF_CHUNK = 256


def _resident(shape):
    n = len(shape)
    return pl.BlockSpec(shape, lambda *_: (0,) * n, pipeline_mode=pl.Buffered(1))


def _rmsnorm(x, gain):
    ms = jnp.mean(x * x, axis=-1, keepdims=True)
    return x * lax.rsqrt(ms + EPS) * gain


def _dot(a, b):
    return jnp.dot(a, b, preferred_element_type=F32)


def _ada_kernel(c_ref, w_ref, b_ref, o_ref):
    c = c_ref[...]
    h = (c * jax.nn.sigmoid(c)).astype(BF16)
    o_ref[0] = _dot(h, w_ref[...].astype(BF16)) + b_ref[...]


def _ada(c_all, w_ada, b_ada):
    nb, d = c_all.shape
    return pl.pallas_call(
        _ada_kernel,
        grid=(N_MOD,),
        in_specs=[
            pl.BlockSpec((nb, d), lambda j: (0, 0)),
            pl.BlockSpec((d, d), lambda j: (0, j)),
            pl.BlockSpec((1, d), lambda j: (0, j)),
        ],
        out_specs=pl.BlockSpec((1, nb, d), lambda j: (j, 0, 0)),
        out_shape=jax.ShapeDtypeStruct((N_MOD, nb, d), F32),
        compiler_params=pltpu.CompilerParams(dimension_semantics=("arbitrary",),
                                             vmem_limit_bytes=VMEM_LIMIT_BYTES),
        name="ada",
    )(c_all, w_ada, b_ada.reshape(1, -1))


def _ffn_kernel(xp_ref, xs_ref, modp_ref, mods_ref, ng_ref, wg_ref, wu_ref, wd_ref, *rest,
                n_stage, n_prompt, final_norm):
    if final_norm:
        fg_ref, op_ref, os_ref, wg_scr, wu_scr, wd_scr, act_scr = rest
    else:
        op_ref, os_ref, wg_scr, wu_scr, wd_scr, act_scr = rest
    i = pl.program_id(0)
    n_chunks, _, ffc = wg_scr.shape

    @pl.when(i < n_stage)
    def _():
        wg_scr[i] = wg_ref[...].astype(BF16)
        wu_scr[i] = wu_ref[...].astype(BF16)
        wd_scr[pl.ds(pl.multiple_of(i * ffc, ffc), ffc), :] = wd_ref[...].astype(BF16)

    def tile(x_ref, mod_ref, o_ref):
        x = x_ref[...]
        g_, r_, d_ = x.shape
        shift, scale, gate = mod_ref[0], mod_ref[1], mod_ref[2]
        h = _rmsnorm(x, ng_ref[...]) * (1.0 + scale) + shift
        rows = g_ * r_
        hb = h.astype(BF16).reshape(rows, d_)
        for c in range(n_chunks):
            g = _dot(hb, wg_scr[c])
            u = _dot(hb, wu_scr[c])
            act_scr[0:rows, c * ffc:(c + 1) * ffc] = (g * jax.nn.sigmoid(g) * u).astype(BF16)
        y = _dot(act_scr[0:rows, :], wd_scr[...]).reshape(g_, r_, d_)
        out = x + 0.5 * gate * y
        if final_norm:
            out = _rmsnorm(out, fg_ref[...])
        o_ref[...] = out

    @pl.when(jnp.logical_and(i >= n_stage, i < n_stage + n_prompt))
    def _():
        tile(xp_ref, modp_ref, op_ref)

    @pl.when(i == n_stage + n_prompt)
    def _():
        tile(xs_ref, mods_ref, os_ref)


def _ffn(xp, xs, mods_p, mods_s, norm_gain, wg, wu, wd, final_gain):
    nb, seq, d = xp.shape
    d_ff = wg.shape[1]
    tm = FFN_TOKEN_TILE
    ffc = FF_CHUNK
    assert seq % tm == 0 and d_ff % ffc == 0 and xs.shape[0] * xs.shape[1] <= tm
    per_seq = seq // tm
    n_stage, n_prompt = d_ff // ffc, nb * per_seq
    final_norm = final_gain is not None

    def prompt_tile(i):
        return jnp.clip(i - n_stage, 0, n_prompt - 1)

    def stage(i):
        return jnp.minimum(i, n_stage - 1)

    x_spec = pl.BlockSpec((1, tm, d), lambda i: (prompt_tile(i) // per_seq, prompt_tile(i) % per_seq, 0))
    in_specs = [
        x_spec,
        _resident(xs.shape),
        pl.BlockSpec((3, 1, 1, d), lambda i: (0, prompt_tile(i) // per_seq, 0, 0)),
        _resident(mods_s.shape),
        _resident((1, d)),
        pl.BlockSpec((d, ffc), lambda i: (0, stage(i))),
        pl.BlockSpec((d, ffc), lambda i: (0, stage(i))),
        pl.BlockSpec((ffc, d), lambda i: (stage(i), 0)),
    ]
    args = [xp, xs, mods_p, mods_s, norm_gain.reshape(1, d), wg, wu, wd]
    if final_norm:
        in_specs.append(_resident((1, d)))
        args.append(final_gain.reshape(1, d))
    return pl.pallas_call(
        functools.partial(_ffn_kernel, n_stage=n_stage, n_prompt=n_prompt, final_norm=final_norm),
        grid=(n_stage + n_prompt + 1,),
        in_specs=in_specs,
        out_specs=[x_spec, pl.BlockSpec(xs.shape, lambda i: (0, 0, 0))],
        out_shape=[jax.ShapeDtypeStruct(xp.shape, F32), jax.ShapeDtypeStruct(xs.shape, F32)],
        scratch_shapes=[
            pltpu.VMEM((n_stage, d, ffc), BF16),
            pltpu.VMEM((n_stage, d, ffc), BF16),
            pltpu.VMEM((d_ff, d), BF16),
            pltpu.VMEM((tm, d_ff), BF16),
        ],
        compiler_params=pltpu.CompilerParams(dimension_semantics=("arbitrary",),
                                             vmem_limit_bytes=VMEM_LIMIT_BYTES),
        name="ffn_final" if final_norm else "ffn",
    )(*args)


def _rotate(xh, cos2, sin2):
    return xh * cos2 + pltpu.roll(xh, xh.shape[-1] // 2, axis=xh.ndim - 1) * sin2


def _group_norm_gate(o, gate_pre, gain):
    mu = jnp.mean(o, axis=-1, keepdims=True)
    ctr = o - mu
    var = jnp.mean(ctr * ctr, axis=-1, keepdims=True)
    return gate_pre * jax.nn.sigmoid(gate_pre) * (ctr * lax.rsqrt(var + EPS) * gain)


def _layer_norm_silu(y, gain, bias):
    mu = jnp.mean(y, axis=-1, keepdims=True)
    ctr = y - mu
    var = jnp.mean(ctr * ctr, axis=-1, keepdims=True)
    yn = ctr * lax.rsqrt(var + EPS) * gain + bias
    return yn * jax.nn.sigmoid(yn)


def _conv_taps_slab(win_ref, dww_ref, s, n_out, lead):
    acc = None
    for j in range(dww_ref.shape[0]):
        term = win_ref[s, pl.ds(lead + j, n_out), :] * dww_ref[j:j + 1, s * LANES:(s + 1) * LANES]
        acc = term if acc is None else acc + term
    return acc


def _mix_prompt_kernel(xa_ref, xb_ref, moda_ref, modb_ref, ng_ref, win_ref, gn_ref, dww_ref, dwb_ref,
                       lng_ref, lnb_ref, wout_ref, cos_ref, sin_ref, dmask_ref, dq_ref, dk_ref, gc_ref,
                       o_ref, sout_ref, cout_ref, s_scr, u_scr, hb_scr, conv_scr, mix_scr,
                       *, heads, hist, per_seq, n_tiles):
    i = pl.program_id(0)
    tt, d_ = xa_ref.shape[1], xa_ref.shape[2]
    ret_w = gn_ref.shape[1]
    dh = ret_w // heads
    n_taps, conv_w = dww_ref.shape
    k_scale = dh ** -0.5
    o_a, o_b = 4 * ret_w, 4 * ret_w + conv_w
    n_slabs = u_scr.shape[0]

    slot_a = lax.rem(i, 2)
    slot_b = 1 - slot_a

    def stage_a_project():
        x = xa_ref[0]
        shift, scale = moda_ref[0, 0], moda_ref[1, 0]
        h = _rmsnorm(x, ng_ref[...]) * (1.0 + scale) + shift
        hb = h.astype(BF16)
        hb_scr[slot_a] = hb
        ab = _dot(hb, win_ref[:, o_a:o_b + conv_w])
        u = ab[:, :conv_w] * jax.nn.sigmoid(ab[:, conv_w:])
        for s in range(n_slabs):
            u_scr[s, hist:hist + tt, :] = u[:, s * LANES:(s + 1) * LANES]

    def stage_a_conv_slab(s):
        return _conv_taps_slab(u_scr, dww_ref, s, tt, hist - (n_taps - 1))

    def stage_a_finish(slabs):
        y = jnp.concatenate(slabs, axis=-1) + dwb_ref[...]
        conv_scr[slot_a] = _layer_norm_silu(y, lng_ref[...], lnb_ref[...]).astype(BF16)
        for s in range(n_slabs):
            u_scr[s, 0:hist, :] = u_scr[s, tt:tt + hist, :]

    def stage_b_project():
        return _dot(hb_scr[slot_b], win_ref[:, 0:o_a])

    def stage_b_head(proj, hd):
        lo = hd * dh
        cos2, sin2 = cos_ref[...], sin_ref[...]
        q = _rotate(proj[:, lo:lo + dh], cos2, sin2)
        k = _rotate(proj[:, ret_w + lo:ret_w + lo + dh], cos2, sin2) * k_scale
        v = proj[:, 2 * ret_w + lo:2 * ret_w + lo + dh]
        gate_pre = proj[:, 3 * ret_w + lo:3 * ret_w + lo + dh]
        dmask, dq, dk, gc = dmask_ref[hd], dq_ref[hd], dk_ref[hd], gc_ref[hd]
        state = s_scr[hd]
        outs = []
        for c0 in range(0, tt, RET_CHUNK):
            qc, kc = q[c0:c0 + RET_CHUNK], k[c0:c0 + RET_CHUNK]
            vb = v[c0:c0 + RET_CHUNK].astype(BF16)
            scores = lax.dot_general(qc.astype(BF16), kc.astype(BF16), (((1,), (1,)), ((), ())),
                                     preferred_element_type=F32)
            inner = _dot((scores * dmask).astype(BF16), vb)
            cross = _dot((qc * dq).astype(BF16), state.astype(BF16))
            outs.append(inner + cross)
            state = gc * state + lax.dot_general((kc * dk).astype(BF16), vb, (((0,), (0,)), ((), ())),
                                                 preferred_element_type=F32)
        s_scr[hd] = state
        o = jnp.concatenate(outs, axis=0) if len(outs) > 1 else outs[0]
        mix_scr[:, lo:lo + dh] = _group_norm_gate(o, gate_pre, gn_ref[:, lo:lo + dh]).astype(BF16)

    def stage_b_finish():
        mixed = (_dot(mix_scr[...], wout_ref[0:ret_w, :])
                 + _dot(conv_scr[slot_b], wout_ref[ret_w:ret_w + conv_w, :]))
        o_ref[0] = xb_ref[0] + modb_ref[2, 0] * mixed

    def run(do_a, do_b):
        if do_a:
            stage_a_project()
        if do_b:
            proj = stage_b_project()
            for hd in range(heads):
                stage_b_head(proj, hd)
            stage_b_finish()
        if do_a:
            stage_a_finish([stage_a_conv_slab(s) for s in range(n_slabs)])

    has_a = i < n_tiles
    has_b = i >= 1
    pos_a = lax.rem(i, per_seq)
    pos_b = lax.rem(i + per_seq - 1, per_seq)

    @pl.when(jnp.logical_and(has_a, pos_a == 0))
    def _():
        u_scr[:, 0:hist, :] = jnp.zeros((n_slabs, hist, LANES), F32)

    @pl.when(jnp.logical_and(has_b, pos_b == 0))
    def _():
        s_scr[...] = jnp.zeros_like(s_scr)

    @pl.when(i == 0)
    def _():
        run(True, False)

    @pl.when(jnp.logical_and(has_a, has_b))
    def _():
        run(True, True)

    @pl.when(i == n_tiles)
    def _():
        run(False, True)

    @pl.when(jnp.logical_and(has_a, pos_a == per_seq - 1))
    def _():
        for s in range(n_slabs):
            cout_ref[0, :, s * LANES:(s + 1) * LANES] = u_scr[s, hist - (n_taps - 1):hist, :]

    @pl.when(jnp.logical_and(has_b, pos_b == per_seq - 1))
    def _():
        sout_ref[0] = s_scr[...]


def _decay_tables(heads, chunk):
    lg = jnp.log(1.0 - 2.0 ** (-5.0 - jnp.arange(heads, dtype=F32)))
    idx = jnp.arange(chunk, dtype=F32)
    diff = idx[:, None] - idx[None, :]
    dmask = jnp.where(diff[None] >= 0, jnp.exp(lg[:, None, None] * jnp.maximum(diff, 0.0)[None]), 0.0)
    dq = jnp.exp(lg[:, None] * (idx[None, :] + 1.0))
    dk = jnp.exp(lg[:, None] * (chunk - 1.0 - idx[None, :]))
    gc = jnp.exp(lg * chunk)
    return dmask, dq, dk, gc


def _rotary_tables(pos0, n_pos, half):
    inv = ROPE_BASE ** (-jnp.arange(half, dtype=F32) / half)
    pos = (pos0 + jnp.arange(n_pos, dtype=jnp.int32)).astype(F32)
    ang = pos[:, None] * inv[None, :]
    cos, sin = jnp.cos(ang), jnp.sin(ang)
    return jnp.concatenate([cos, cos], axis=-1), jnp.concatenate([-sin, sin], axis=-1)


def _mix_prompt(x, mods, norm_gain, w_in, gn_gain, dw_w, dw_b, ln_g, ln_b, w_out, heads):
    nb, seq, d = x.shape
    tt = MIX_TOKEN_TILE
    ret_w = gn_gain.shape[0]
    dh = ret_w // heads
    n_taps, conv_w = dw_w.shape
    hist = -(-(n_taps - 1) // SUBLANES) * SUBLANES
    chunk = RET_CHUNK
    cos2, sin2 = _rotary_tables(0, seq, dh // 2)
    dmask, dq, dk, gc = _decay_tables(heads, chunk)
    dq_b = jnp.broadcast_to(dq[:, :, None], (heads, chunk, dh))
    dk_b = jnp.broadcast_to(dk[:, :, None], (heads, chunk, dh))
    gc_b = jnp.broadcast_to(gc[:, None, None], (heads, 1, dh))
    per_seq = seq // tt
    n_tiles = nb * per_seq

    def tile_a(i):
        return jnp.minimum(i, n_tiles - 1)

    def tile_b(i):
        return jnp.maximum(i - 1, 0)

    kern = functools.partial(_mix_prompt_kernel, heads=heads, hist=hist, per_seq=per_seq, n_tiles=n_tiles)
    return pl.pallas_call(
        kern,
        grid=(n_tiles + 1,),
        in_specs=[
            pl.BlockSpec((1, tt, d), lambda i: (tile_a(i) // per_seq, tile_a(i) % per_seq, 0)),
            pl.BlockSpec((1, tt, d), lambda i: (tile_b(i) // per_seq, tile_b(i) % per_seq, 0)),
            pl.BlockSpec((3, 1, 1, d), lambda i: (0, tile_a(i) // per_seq, 0, 0)),
            pl.BlockSpec((3, 1, 1, d), lambda i: (0, tile_b(i) // per_seq, 0, 0)),
            _resident((1, d)),
            _resident(w_in.shape),
            _resident((1, ret_w)),
            _resident((n_taps, conv_w)),
            _resident((1, conv_w)),
            _resident((1, conv_w)),
            _resident((1, conv_w)),
            _resident(w_out.shape),
            pl.BlockSpec((tt, dh), lambda i: (tile_b(i) % per_seq, 0)),
            pl.BlockSpec((tt, dh), lambda i: (tile_b(i) % per_seq, 0)),
            _resident((heads, chunk, chunk)),
            _resident((heads, chunk, dh)),
            _resident((heads, chunk, dh)),
            _resident((heads, 1, dh)),
        ],
        out_specs=[
            pl.BlockSpec((1, tt, d), lambda i: (tile_b(i) // per_seq, tile_b(i) % per_seq, 0)),
            pl.BlockSpec((1, heads, dh, dh), lambda i: (tile_b(i) // per_seq, 0, 0, 0)),
            pl.BlockSpec((1, n_taps - 1, conv_w), lambda i: (tile_a(i) // per_seq, 0, 0)),
        ],
        out_shape=[
            jax.ShapeDtypeStruct(x.shape, F32),
            jax.ShapeDtypeStruct((nb, heads, dh, dh), F32),
            jax.ShapeDtypeStruct((nb, n_taps - 1, conv_w), F32),
        ],
        scratch_shapes=[
            pltpu.VMEM((heads, dh, dh), F32),
            pltpu.VMEM((conv_w // LANES, hist + tt, LANES), F32),
            pltpu.VMEM((2, tt, d), BF16),
            pltpu.VMEM((2, tt, conv_w), BF16),
            pltpu.VMEM((tt, ret_w), BF16),
        ],
        compiler_params=pltpu.CompilerParams(dimension_semantics=("arbitrary",),
                                             vmem_limit_bytes=VMEM_LIMIT_BYTES),
        name="mix_prompt",
    )(x, x, mods, mods, norm_gain.reshape(1, d), w_in, gn_gain.reshape(1, -1), dw_w, dw_b.reshape(1, -1),
      ln_g.reshape(1, -1), ln_b.reshape(1, -1), w_out, cos2, sin2, dmask, dq_b, dk_b, gc_b)


def _sample_proj_kernel(x_ref, mod_ref, ng_ref, win_ref, o_ref):
    x = x_ref[...]
    n_tok, ns, d_ = x.shape
    h = _rmsnorm(x, ng_ref[...]) * (1.0 + mod_ref[1]) + mod_ref[0]
    o_ref[...] = _dot(h.astype(BF16).reshape(n_tok * ns, d_), win_ref[...]).reshape(n_tok, ns, -1)


def _sample_out_kernel(x_ref, mod_ref, mixed_ref, wout_ref, o_ref):
    n_tok, ns, d_ = x_ref.shape
    y = _dot(mixed_ref[...].reshape(n_tok * ns, -1), wout_ref[...]).reshape(n_tok, ns, d_)
    o_ref[...] = x_ref[...] + mod_ref[2] * y


def _sample_core_kernel(proj_ref, state_ref, cin_ref, cos_ref, sin_ref, dm_ref, dq_ref, dk_ref, gc_ref,
                        gn_ref, dww_ref, dwb_ref, lng_ref, lnb_ref, mixed_ref, sout_ref, cout_ref,
                        qd_scr, kd_scr, v_scr, cross_scr, *, heads):
    n_tok, tb, _ = proj_ref.shape
    ret_w = gn_ref.shape[1]
    dh = ret_w // heads
    n_taps, conv_w = dww_ref.shape
    n_buf = n_taps - 1
    o_a, o_b = 4 * ret_w, 4 * ret_w + conv_w
    k_scale = dh ** -0.5
    cos2, sin2 = cos_ref[...], sin_ref[...]

    inner_heads = []
    for hd in range(heads):
        lo = hd * dh
        q = _rotate(proj_ref[:, :, lo:lo + dh], cos2, sin2)
        k = _rotate(proj_ref[:, :, ret_w + lo:ret_w + lo + dh], cos2, sin2) * k_scale
        v = proj_ref[:, :, 2 * ret_w + lo:2 * ret_w + lo + dh]
        qd_scr[:, :, lo:lo + dh] = q * dq_ref[hd]
        kd_scr[:, :, lo:lo + dh] = k * dk_ref[hd]
        v_scr[:, :, lo:lo + dh] = v
        rows_out = []
        for i in range(n_tok):
            acc = None
            for j in range(i + 1):
                s_ij = jnp.sum(q[i] * k[j], axis=-1, keepdims=True)
                term = (s_ij * dm_ref[hd, i, j]) * v[j]
                acc = term if acc is None else acc + term
            rows_out.append(acc)
        inner_heads.append(jnp.stack(rows_out, axis=0))

    row = lax.broadcasted_iota(jnp.int32, (n_tok, SUBLANES, dh), 1)

    def group(g, carry):
        rows = pl.ds(pl.multiple_of(g * SUBLANES, SUBLANES), SUBLANES)
        for hd in range(heads):
            lo = hd * dh
            q_tile = qd_scr[:, rows, lo:lo + dh].reshape(n_tok * SUBLANES, dh).astype(BF16)
            k_tile = kd_scr[:, rows, lo:lo + dh].reshape(n_tok * SUBLANES, dh).astype(BF16)
            v_group = v_scr[:, rows, lo:lo + dh]
            cross = jnp.zeros((n_tok, SUBLANES, dh), F32)
            for r in range(SUBLANES):
                b = g * SUBLANES + r
                state = state_ref[b, hd]
                out = _dot(q_tile, state.astype(BF16)).reshape(n_tok, SUBLANES, dh)
                cross = jnp.where(row == r, out, cross)
                v_own = jnp.where(row == r, v_group, 0.0).reshape(n_tok * SUBLANES, dh).astype(BF16)
                upd = lax.dot_general(k_tile, v_own, (((0,), (0,)), ((), ())), preferred_element_type=F32)
                sout_ref[b, hd] = gc_ref[hd] * state + upd
            cross_scr[:, rows, lo:lo + dh] = cross
        return carry

    lax.fori_loop(0, tb // SUBLANES, group, 0)

    for hd in range(heads):
        lo = hd * dh
        o = inner_heads[hd] + cross_scr[:, :, lo:lo + dh]
        gate_pre = proj_ref[:, :, 3 * ret_w + lo:3 * ret_w + lo + dh]
        mixed_ref[:, :, lo:lo + dh] = _group_norm_gate(o, gate_pre, gn_ref[:, lo:lo + dh]).astype(BF16)

    u = proj_ref[:, :, o_a:o_a + conv_w] * jax.nn.sigmoid(proj_ref[:, :, o_b:o_b + conv_w])

    def window(s):
        return cin_ref[s] if s < n_buf else u[s - n_buf]

    for t in range(n_tok):
        acc = None
        for j in range(n_taps):
            term = window(t + j) * dww_ref[j:j + 1, :]
            acc = term if acc is None else acc + term
        y = acc + dwb_ref[...]
        mixed_ref[t, :, ret_w:ret_w + conv_w] = _layer_norm_silu(y, lng_ref[...], lnb_ref[...]).astype(BF16)
    for s in range(n_buf):
        cout_ref[s] = window(s + n_tok)


def _whole(shape):
    n = len(shape)
    return pl.BlockSpec(shape, lambda *_: (0,) * n)


def _mix_sample(x, mods, norm_gain, w_in, gn_gain, dw_w, dw_b, ln_g, ln_b, w_out, state_ret, conv_tm):
    n_tok, ns, d = x.shape
    tb = SAMPLE_MIX_SEQ_TILE
    heads, dh = state_ret.shape[1], state_ret.shape[2]
    ret_w = heads * dh
    n_taps, conv_w = dw_w.shape
    n_buf = n_taps - 1
    n_cols = w_in.shape[1]
    params = pltpu.CompilerParams(vmem_limit_bytes=VMEM_LIMIT_BYTES)

    proj = pl.pallas_call(
        _sample_proj_kernel,
        in_specs=[_whole(x.shape), _whole(mods.shape), _whole((1, d)), _whole(w_in.shape)],
        out_specs=_whole((n_tok, ns, n_cols)),
        out_shape=jax.ShapeDtypeStruct((n_tok, ns, n_cols), F32),
        grid=(1,),
        compiler_params=params,
        name="sample_proj",
    )(x, mods, norm_gain.reshape(1, d), w_in)

    cos2, sin2 = _rotary_tables(PAST_LEN, n_tok, dh // 2)
    dmask, dq, dk, gc = _decay_tables(heads, n_tok)
    dm_b = jnp.broadcast_to(dmask[:, :, :, None, None], (heads, n_tok, n_tok, 1, dh))
    dq_b = jnp.broadcast_to(dq[:, :, None, None], (heads, n_tok, 1, dh))
    dk_b = jnp.broadcast_to(dk[:, :, None, None], (heads, n_tok, 1, dh))
    gc_b = jnp.broadcast_to(gc[:, None, None], (heads, 1, dh))
    mixed, state_new, conv_new = pl.pallas_call(
        functools.partial(_sample_core_kernel, heads=heads),
        grid=(ns // tb,),
        in_specs=[
            pl.BlockSpec((n_tok, tb, n_cols), lambda i: (0, i, 0)),
            pl.BlockSpec((tb, heads, dh, dh), lambda i: (i, 0, 0, 0)),
            pl.BlockSpec((n_buf, tb, conv_w), lambda i: (0, i, 0)),
            _resident((n_tok, 1, dh)),
            _resident((n_tok, 1, dh)),
            _resident((heads, n_tok, n_tok, 1, dh)),
            _resident((heads, n_tok, 1, dh)),
            _resident((heads, n_tok, 1, dh)),
            _resident((heads, 1, dh)),
            _resident((1, ret_w)),
            _resident((n_taps, conv_w)),
            _resident((1, conv_w)),
            _resident((1, conv_w)),
            _resident((1, conv_w)),
        ],
        out_specs=[
            pl.BlockSpec((n_tok, tb, ret_w + conv_w), lambda i: (0, i, 0)),
            pl.BlockSpec((tb, heads, dh, dh), lambda i: (i, 0, 0, 0)),
            pl.BlockSpec((n_buf, tb, conv_w), lambda i: (0, i, 0)),
        ],
        out_shape=[
            jax.ShapeDtypeStruct((n_tok, ns, ret_w + conv_w), BF16),
            jax.ShapeDtypeStruct(state_ret.shape, F32),
            jax.ShapeDtypeStruct(conv_tm.shape, F32),
        ],
        scratch_shapes=[pltpu.VMEM((n_tok, tb, ret_w), F32)] * 4,
        compiler_params=pltpu.CompilerParams(dimension_semantics=("arbitrary",),
                                             vmem_limit_bytes=VMEM_LIMIT_BYTES),
        name="sample_core",
    )(proj, state_ret, conv_tm, cos2.reshape(n_tok, 1, dh), sin2.reshape(n_tok, 1, dh), dm_b, dq_b, dk_b,
      gc_b, gn_gain.reshape(1, -1), dw_w, dw_b.reshape(1, -1), ln_g.reshape(1, -1), ln_b.reshape(1, -1))

    x_new = pl.pallas_call(
        _sample_out_kernel,
        in_specs=[_whole(x.shape), _whole(mods.shape), _whole(mixed.shape), _whole(w_out.shape)],
        out_specs=_whole(x.shape),
        out_shape=jax.ShapeDtypeStruct(x.shape, F32),
        grid=(1,),
        compiler_params=params,
        name="sample_out",
    )(x, mods, mixed, w_out)
    return x_new, state_new, conv_new


def _layer(xp, xs, mods_p, mods_s, sret, sconv, lw, final_gain, n_tok):
    (norm_ffn1, w1g, w1u, w1d, norm_mix, w_in, gn_gain, dw_w, dw_b, ln_g, ln_b, w_out,
     norm_ffn2, w2g, w2u, w2d) = lw
    nb, seq, d = xp.shape
    n_tok, ns, _ = xs.shape
    heads = sret.shape[1]
    mods_p = mods_p.reshape(N_MOD, nb, 1, d)
    mods_s_tm = mods_s.reshape(N_MOD, 1, ns, d)

    xp, xs = _ffn(xp, xs, mods_p[0:3], mods_s_tm[0:3], norm_ffn1, w1g, w1u, w1d, None)
    xp, ret_p, conv_p = _mix_prompt(xp, mods_p[3:6], norm_mix, w_in, gn_gain, dw_w, dw_b, ln_g, ln_b,
                                    w_out, heads)
    xs, ret_s, conv_s_tm = _mix_sample(xs, mods_s_tm[3:6], norm_mix, w_in, gn_gain, dw_w, dw_b, ln_g, ln_b,
                                       w_out, sret, sconv.transpose(1, 0, 2))
    xp, xs = _ffn(xp, xs, mods_p[6:9], mods_s_tm[6:9], norm_ffn2, w2g, w2u, w2d, final_gain)
    return xp, xs, ret_p, conv_p, ret_s, conv_s_tm.transpose(1, 0, 2)


def kernel(x_prompt, x_sample, c_prompt, c_sample, state_ret, state_conv, norm_ffn1, ffn1_w_gate,
           ffn1_w_up, ffn1_w_down, norm_mix, w_in, ret_gn_gain, dw_w, dw_b, conv_ln_gain, conv_ln_bias,
           w_out, norm_ffn2, ffn2_w_gate, ffn2_w_up, ffn2_w_down, w_ada, b_ada, norm_final):
    depth = w_in.shape[0]
    nb = x_prompt.shape[0]
    ns, n_tok, d = x_sample.shape
    assert n_tok <= SUBLANES and x_prompt.shape[1] % RET_CHUNK == 0

    xp = x_prompt
    xs = x_sample.transpose(1, 0, 2)
    c_all = jnp.concatenate([c_prompt, c_sample], axis=0)

    ret_p, conv_p, ret_s, conv_s = [], [], [], []
    for l in range(depth):
        ada = _ada(c_all, w_ada[l], b_ada[l])
        lw = (norm_ffn1[l], ffn1_w_gate[l], ffn1_w_up[l], ffn1_w_down[l], norm_mix[l],
              w_in[l].astype(BF16), ret_gn_gain[l], dw_w[l], dw_b[l], conv_ln_gain[l], conv_ln_bias[l],
              w_out[l].astype(BF16), norm_ffn2[l], ffn2_w_gate[l], ffn2_w_up[l], ffn2_w_down[l])
        final_gain = norm_final if l == depth - 1 else None
        xp, xs, rp, cp, rs, cs = _layer(xp, xs, ada[:, :nb], ada[:, nb:], state_ret[l], state_conv[l], lw,
                                        final_gain, n_tok)
        ret_p.append(rp)
        conv_p.append(cp)
        ret_s.append(rs)
        conv_s.append(cs)

    return (xp, xs.transpose(1, 0, 2), jnp.stack(ret_p), jnp.stack(conv_p), jnp.stack(ret_s),
            jnp.stack(conv_s))
```

```python
import functools

import jax
import jax.numpy as jnp
import numpy as np
from jax import lax
from jax.experimental import pallas as pl
from jax.experimental.pallas import tpu as pltpu

F32 = jnp.float32
BF16 = jnp.bfloat16

PAST_LEN = 16384
RET_CHUNK = 256
ROPE_BASE = 10000.0
EPS = 1e-6
N_MOD = 9

SUBLANES = 8
LANES = 128
VMEM_LIMIT_BYTES = 56 * 1024 * 1024

FFN_TOKEN_TILE = 512
MIX_TOKEN_TILE = 512
SAMPLE_MIX_SEQ_TILE = 16
ADA_MODS_PER_STEP = 3
FF_CHUNK = 256


def _resident(shape):
    n = len(shape)
    return pl.BlockSpec(shape, lambda *_: (0,) * n, pipeline_mode=pl.Buffered(1))


def _rmsnorm(x, gain):
    ms = jnp.mean(x * x, axis=-1, keepdims=True)
    return x * lax.rsqrt(ms + EPS) * gain


def _dot(a, b):
    return jnp.dot(a, b, preferred_element_type=F32)


def _ada_kernel(c_ref, w_ref, b_ref, o_ref):
    c = c_ref[...]
    h = (c * jax.nn.sigmoid(c)).astype(BF16)
    d = c.shape[1]
    for m in range(o_ref.shape[0]):
        cols = slice(m * d, (m + 1) * d)
        o_ref[m] = _dot(h, w_ref[:, cols].astype(BF16)) + b_ref[:, cols]


def _ada(c_all, w_ada, b_ada):
    nb, d = c_all.shape
    per_step = ADA_MODS_PER_STEP
    return pl.pallas_call(
        _ada_kernel,
        grid=(N_MOD // per_step,),
        in_specs=[
            pl.BlockSpec((nb, d), lambda j: (0, 0)),
            pl.BlockSpec((d, per_step * d), lambda j: (0, j)),
            pl.BlockSpec((1, per_step * d), lambda j: (0, j)),
        ],
        out_specs=pl.BlockSpec((per_step, nb, d), lambda j: (j, 0, 0)),
        out_shape=jax.ShapeDtypeStruct((N_MOD, nb, d), F32),
        compiler_params=pltpu.CompilerParams(dimension_semantics=("arbitrary",),
                                             vmem_limit_bytes=VMEM_LIMIT_BYTES),
        name="ada",
    )(c_all, w_ada, b_ada.reshape(1, -1))


def _ffn_kernel(xp_ref, xs_ref, modp_ref, mods_ref, ng_ref, wg_ref, wu_ref, wd_ref, *rest,
                n_stage, n_prompt, final_norm):
    if final_norm:
        fg_ref, op_ref, os_ref, wg_scr, wu_scr, wd_scr, act_scr = rest
    else:
        op_ref, os_ref, wg_scr, wu_scr, wd_scr, act_scr = rest
    i = pl.program_id(0)
    n_chunks, _, ffc = wg_scr.shape

    @pl.when(i < n_stage)
    def _():
        wg_scr[i] = wg_ref[...].astype(BF16)
        wu_scr[i] = wu_ref[...].astype(BF16)
        wd_scr[pl.ds(pl.multiple_of(i * ffc, ffc), ffc), :] = wd_ref[...].astype(BF16)

    def tile(x_ref, mod_ref, o_ref):
        x = x_ref[...]
        g_, r_, d_ = x.shape
        shift, scale, gate = mod_ref[0], mod_ref[1], mod_ref[2]
        h = _rmsnorm(x, ng_ref[...]) * (1.0 + scale) + shift
        rows = g_ * r_
        hb = h.astype(BF16).reshape(rows, d_)
        for c in range(n_chunks):
            g = _dot(hb, wg_scr[c])
            u = _dot(hb, wu_scr[c])
            act_scr[0:rows, c * ffc:(c + 1) * ffc] = (g * jax.nn.sigmoid(g) * u).astype(BF16)
        y = _dot(act_scr[0:rows, :], wd_scr[...]).reshape(g_, r_, d_)
        out = x + 0.5 * gate * y
        if final_norm:
            out = _rmsnorm(out, fg_ref[...])
        o_ref[...] = out

    @pl.when(jnp.logical_and(i >= n_stage, i < n_stage + n_prompt))
    def _():
        tile(xp_ref, modp_ref, op_ref)

    @pl.when(i == n_stage + n_prompt)
    def _():
        tile(xs_ref, mods_ref, os_ref)


def _ffn(xp, xs, mods_p, mods_s, norm_gain, wg, wu, wd, final_gain):
    nb, seq, d = xp.shape
    d_ff = wg.shape[1]
    tm = FFN_TOKEN_TILE
    ffc = FF_CHUNK
    assert seq % tm == 0 and d_ff % ffc == 0 and xs.shape[0] * xs.shape[1] <= tm
    per_seq = seq // tm
    n_stage, n_prompt = d_ff // ffc, nb * per_seq
    final_norm = final_gain is not None

    def prompt_tile(i):
        return jnp.clip(i - n_stage, 0, n_prompt - 1)

    def stage(i):
        return jnp.minimum(i, n_stage - 1)

    x_spec = pl.BlockSpec((1, tm, d), lambda i: (prompt_tile(i) // per_seq, prompt_tile(i) % per_seq, 0))
    in_specs = [
        x_spec,
        _resident(xs.shape),
        pl.BlockSpec((3, 1, 1, d), lambda i: (0, prompt_tile(i) // per_seq, 0, 0)),
        _resident(mods_s.shape),
        _resident((1, d)),
        pl.BlockSpec((d, ffc), lambda i: (0, stage(i))),
        pl.BlockSpec((d, ffc), lambda i: (0, stage(i))),
        pl.BlockSpec((ffc, d), lambda i: (stage(i), 0)),
    ]
    args = [xp, xs, mods_p, mods_s, norm_gain.reshape(1, d), wg, wu, wd]
    if final_norm:
        in_specs.append(_resident((1, d)))
        args.append(final_gain.reshape(1, d))
    return pl.pallas_call(
        functools.partial(_ffn_kernel, n_stage=n_stage, n_prompt=n_prompt, final_norm=final_norm),
        grid=(n_stage + n_prompt + 1,),
        in_specs=in_specs,
        out_specs=[x_spec, pl.BlockSpec(xs.shape, lambda i: (0, 0, 0))],
        out_shape=[jax.ShapeDtypeStruct(xp.shape, F32), jax.ShapeDtypeStruct(xs.shape, F32)],
        scratch_shapes=[
            pltpu.VMEM((n_stage, d, ffc), BF16),
            pltpu.VMEM((n_stage, d, ffc), BF16),
            pltpu.VMEM((d_ff, d), BF16),
            pltpu.VMEM((tm, d_ff), BF16),
        ],
        compiler_params=pltpu.CompilerParams(dimension_semantics=("arbitrary",),
                                             vmem_limit_bytes=VMEM_LIMIT_BYTES),
        name="ffn_final" if final_norm else "ffn",
    )(*args)


def _rotate(xh, cos2, sin2):
    return xh * cos2 + pltpu.roll(xh, xh.shape[-1] // 2, axis=xh.ndim - 1) * sin2


def _group_norm_gate(o, gate_pre, gain):
    mu = jnp.mean(o, axis=-1, keepdims=True)
    ctr = o - mu
    var = jnp.mean(ctr * ctr, axis=-1, keepdims=True)
    return gate_pre * jax.nn.sigmoid(gate_pre) * (ctr * lax.rsqrt(var + EPS) * gain)


def _layer_norm_silu(y, gain, bias):
    mu = jnp.mean(y, axis=-1, keepdims=True)
    ctr = y - mu
    var = jnp.mean(ctr * ctr, axis=-1, keepdims=True)
    yn = ctr * lax.rsqrt(var + EPS) * gain + bias
    return yn * jax.nn.sigmoid(yn)


def _conv_taps_slab(win_ref, dww_ref, s, n_out, lead):
    acc = None
    for j in range(dww_ref.shape[0]):
        term = win_ref[s, pl.ds(lead + j, n_out), :] * dww_ref[j:j + 1, s * LANES:(s + 1) * LANES]
        acc = term if acc is None else acc + term
    return acc


def _mix_prompt_kernel(xa_ref, xb_ref, moda_ref, modb_ref, ng_ref, win_ref, gn_ref, dww_ref, dwb_ref,
                       lng_ref, lnb_ref, wout_ref, cos_ref, sin_ref, dmask_ref, dq_ref, dk_ref, gc_ref,
                       o_ref, sout_ref, cout_ref, s_scr, u_scr, hb_scr, conv_scr, mix_scr,
                       *, heads, hist, per_seq, n_tiles):
    i = pl.program_id(0)
    tt, d_ = xa_ref.shape[1], xa_ref.shape[2]
    ret_w = gn_ref.shape[1]
    dh = ret_w // heads
    n_taps, conv_w = dww_ref.shape
    k_scale = dh ** -0.5
    o_a, o_b = 4 * ret_w, 4 * ret_w + conv_w
    n_slabs = u_scr.shape[0]

    slot_a = lax.rem(i, 2)
    slot_b = 1 - slot_a

    def stage_a_project():
        x = xa_ref[0]
        shift, scale = moda_ref[0, 0], moda_ref[1, 0]
        h = _rmsnorm(x, ng_ref[...]) * (1.0 + scale) + shift
        hb = h.astype(BF16)
        hb_scr[slot_a] = hb
        ab = _dot(hb, win_ref[:, o_a:o_b + conv_w])
        u = ab[:, :conv_w] * jax.nn.sigmoid(ab[:, conv_w:])
        for s in range(n_slabs):
            u_scr[s, hist:hist + tt, :] = u[:, s * LANES:(s + 1) * LANES]

    def stage_a_conv_slab(s):
        return _conv_taps_slab(u_scr, dww_ref, s, tt, hist - (n_taps - 1))

    def stage_a_finish(slabs):
        y = jnp.concatenate(slabs, axis=-1) + dwb_ref[...]
        conv_scr[slot_a] = _layer_norm_silu(y, lng_ref[...], lnb_ref[...]).astype(BF16)
        for s in range(n_slabs):
            u_scr[s, 0:hist, :] = u_scr[s, tt:tt + hist, :]

    def stage_b_project():
        return _dot(hb_scr[slot_b], win_ref[:, 0:o_a])

    def stage_b_head(proj, hd):
        lo = hd * dh
        cos2, sin2 = cos_ref[...], sin_ref[...]
        q = _rotate(proj[:, lo:lo + dh], cos2, sin2)
        k = _rotate(proj[:, ret_w + lo:ret_w + lo + dh], cos2, sin2) * k_scale
        v = proj[:, 2 * ret_w + lo:2 * ret_w + lo + dh]
        gate_pre = proj[:, 3 * ret_w + lo:3 * ret_w + lo + dh]
        dmask, dq, dk, gc = dmask_ref[hd], dq_ref[hd], dk_ref[hd], gc_ref[hd]
        state = s_scr[hd]
        outs = []
        for c0 in range(0, tt, RET_CHUNK):
            qc, kc = q[c0:c0 + RET_CHUNK], k[c0:c0 + RET_CHUNK]
            vb = v[c0:c0 + RET_CHUNK].astype(BF16)
            scores = lax.dot_general(qc.astype(BF16), kc.astype(BF16), (((1,), (1,)), ((), ())),
                                     preferred_element_type=F32)
            inner = _dot((scores * dmask).astype(BF16), vb)
            cross = _dot((qc * dq).astype(BF16), state.astype(BF16))
            outs.append(inner + cross)
            state = gc * state + lax.dot_general((kc * dk).astype(BF16), vb, (((0,), (0,)), ((), ())),
                                                 preferred_element_type=F32)
        s_scr[hd] = state
        o = jnp.concatenate(outs, axis=0) if len(outs) > 1 else outs[0]
        mix_scr[:, lo:lo + dh] = _group_norm_gate(o, gate_pre, gn_ref[:, lo:lo + dh]).astype(BF16)

    def stage_b_finish():
        mixed = (_dot(mix_scr[...], wout_ref[0:ret_w, :])
                 + _dot(conv_scr[slot_b], wout_ref[ret_w:ret_w + conv_w, :]))
        o_ref[0] = xb_ref[0] + modb_ref[2, 0] * mixed

    def run(do_a, do_b):
        if do_a:
            stage_a_project()
        if do_b:
            proj = stage_b_project()
            for hd in range(heads):
                stage_b_head(proj, hd)
            stage_b_finish()
        if do_a:
            stage_a_finish([stage_a_conv_slab(s) for s in range(n_slabs)])

    has_a = i < n_tiles
    has_b = i >= 1
    pos_a = lax.rem(i, per_seq)
    pos_b = lax.rem(i + per_seq - 1, per_seq)

    @pl.when(jnp.logical_and(has_a, pos_a == 0))
    def _():
        u_scr[:, 0:hist, :] = jnp.zeros((n_slabs, hist, LANES), F32)

    @pl.when(jnp.logical_and(has_b, pos_b == 0))
    def _():
        s_scr[...] = jnp.zeros_like(s_scr)

    @pl.when(i == 0)
    def _():
        run(True, False)

    @pl.when(jnp.logical_and(has_a, has_b))
    def _():
        run(True, True)

    @pl.when(i == n_tiles)
    def _():
        run(False, True)

    @pl.when(jnp.logical_and(has_a, pos_a == per_seq - 1))
    def _():
        for s in range(n_slabs):
            cout_ref[0, :, s * LANES:(s + 1) * LANES] = u_scr[s, hist - (n_taps - 1):hist, :]

    @pl.when(jnp.logical_and(has_b, pos_b == per_seq - 1))
    def _():
        sout_ref[0] = s_scr[...]


def _decay_tables(heads, chunk):
    lg = np.log(1.0 - 2.0 ** (-5.0 - np.arange(heads, dtype=np.float64)))
    idx = np.arange(chunk, dtype=np.float64)
    diff = idx[:, None] - idx[None, :]
    dmask = np.where(diff[None] >= 0, np.exp(lg[:, None, None] * np.maximum(diff, 0.0)[None]), 0.0)
    dq = np.exp(lg[:, None] * (idx[None, :] + 1.0))
    dk = np.exp(lg[:, None] * (chunk - 1.0 - idx[None, :]))
    gc = np.exp(lg * chunk)
    return dmask, dq, dk, gc


def _rotary_tables(pos0, n_pos, half):
    inv = ROPE_BASE ** (-np.arange(half, dtype=np.float64) / half)
    pos = (pos0 + np.arange(n_pos)).astype(np.float64)
    ang = pos[:, None] * inv[None, :]
    cos, sin = np.cos(ang), np.sin(ang)
    return np.concatenate([cos, cos], axis=-1), np.concatenate([-sin, sin], axis=-1)


def _const(a, shape=None):
    a = np.asarray(a, dtype=np.float32)
    if shape is not None:
        a = np.ascontiguousarray(np.broadcast_to(a, shape))
    return jnp.asarray(a)


def _mix_prompt(x, mods, norm_gain, w_in, gn_gain, dw_w, dw_b, ln_g, ln_b, w_out, heads):
    nb, seq, d = x.shape
    tt = MIX_TOKEN_TILE
    ret_w = gn_gain.shape[0]
    dh = ret_w // heads
    n_taps, conv_w = dw_w.shape
    hist = -(-(n_taps - 1) // SUBLANES) * SUBLANES
    chunk = RET_CHUNK
    cos2, sin2 = (_const(t) for t in _rotary_tables(0, seq, dh // 2))
    dmask, dq, dk, gc = _decay_tables(heads, chunk)
    dmask = _const(dmask)
    dq_b = _const(dq[:, :, None], (heads, chunk, dh))
    dk_b = _const(dk[:, :, None], (heads, chunk, dh))
    gc_b = _const(gc[:, None, None], (heads, 1, dh))
    per_seq = seq // tt
    n_tiles = nb * per_seq

    def tile_a(i):
        return jnp.minimum(i, n_tiles - 1)

    def tile_b(i):
        return jnp.maximum(i - 1, 0)

    kern = functools.partial(_mix_prompt_kernel, heads=heads, hist=hist, per_seq=per_seq, n_tiles=n_tiles)
    return pl.pallas_call(
        kern,
        grid=(n_tiles + 1,),
        in_specs=[
            pl.BlockSpec((1, tt, d), lambda i: (tile_a(i) // per_seq, tile_a(i) % per_seq, 0)),
            pl.BlockSpec((1, tt, d), lambda i: (tile_b(i) // per_seq, tile_b(i) % per_seq, 0)),
            pl.BlockSpec((3, 1, 1, d), lambda i: (0, tile_a(i) // per_seq, 0, 0)),
            pl.BlockSpec((3, 1, 1, d), lambda i: (0, tile_b(i) // per_seq, 0, 0)),
            _resident((1, d)),
            _resident(w_in.shape),
            _resident((1, ret_w)),
            _resident((n_taps, conv_w)),
            _resident((1, conv_w)),
            _resident((1, conv_w)),
            _resident((1, conv_w)),
            _resident(w_out.shape),
            pl.BlockSpec((tt, dh), lambda i: (tile_b(i) % per_seq, 0)),
            pl.BlockSpec((tt, dh), lambda i: (tile_b(i) % per_seq, 0)),
            _resident((heads, chunk, chunk)),
            _resident((heads, chunk, dh)),
            _resident((heads, chunk, dh)),
            _resident((heads, 1, dh)),
        ],
        out_specs=[
            pl.BlockSpec((1, tt, d), lambda i: (tile_b(i) // per_seq, tile_b(i) % per_seq, 0)),
            pl.BlockSpec((1, heads, dh, dh), lambda i: (tile_b(i) // per_seq, 0, 0, 0)),
            pl.BlockSpec((1, n_taps - 1, conv_w), lambda i: (tile_a(i) // per_seq, 0, 0)),
        ],
        out_shape=[
            jax.ShapeDtypeStruct(x.shape, F32),
            jax.ShapeDtypeStruct((nb, heads, dh, dh), F32),
            jax.ShapeDtypeStruct((nb, n_taps - 1, conv_w), F32),
        ],
        scratch_shapes=[
            pltpu.VMEM((heads, dh, dh), F32),
            pltpu.VMEM((conv_w // LANES, hist + tt, LANES), F32),
            pltpu.VMEM((2, tt, d), BF16),
            pltpu.VMEM((2, tt, conv_w), BF16),
            pltpu.VMEM((tt, ret_w), BF16),
        ],
        compiler_params=pltpu.CompilerParams(dimension_semantics=("arbitrary",),
                                             vmem_limit_bytes=VMEM_LIMIT_BYTES),
        name="mix_prompt",
    )(x, x, mods, mods, norm_gain.reshape(1, d), w_in, gn_gain.reshape(1, -1), dw_w, dw_b.reshape(1, -1),
      ln_g.reshape(1, -1), ln_b.reshape(1, -1), w_out, cos2, sin2, dmask, dq_b, dk_b, gc_b)


def _sample_proj_kernel(x_ref, mod_ref, ng_ref, win_ref, o_ref):
    x = x_ref[...]
    n_tok, ns, d_ = x.shape
    h = _rmsnorm(x, ng_ref[...]) * (1.0 + mod_ref[1]) + mod_ref[0]
    o_ref[...] = _dot(h.astype(BF16).reshape(n_tok * ns, d_), win_ref[...]).reshape(n_tok, ns, -1)


def _sample_out_kernel(x_ref, mod_ref, mixed_ref, wout_ref, o_ref):
    n_tok, ns, d_ = x_ref.shape
    y = _dot(mixed_ref[...].reshape(n_tok * ns, -1), wout_ref[...]).reshape(n_tok, ns, d_)
    o_ref[...] = x_ref[...] + mod_ref[2] * y


def _sample_core_kernel(proj_ref, state_ref, cin_ref, cos_ref, sin_ref, dm_ref, dq_ref, dk_ref, gc_ref,
                        gn_ref, dww_ref, dwb_ref, lng_ref, lnb_ref, mixed_ref, sout_ref, cout_ref,
                        qd_scr, kd_scr, v_scr, cross_scr, *, heads):
    n_tok, tb, _ = proj_ref.shape
    ret_w = gn_ref.shape[1]
    dh = ret_w // heads
    n_taps, conv_w = dww_ref.shape
    n_buf = n_taps - 1
    o_a, o_b = 4 * ret_w, 4 * ret_w + conv_w
    k_scale = dh ** -0.5
    cos2, sin2 = cos_ref[...], sin_ref[...]

    inner_heads = []
    for hd in range(heads):
        lo = hd * dh
        q = _rotate(proj_ref[:, :, lo:lo + dh], cos2, sin2)
        k = _rotate(proj_ref[:, :, ret_w + lo:ret_w + lo + dh], cos2, sin2) * k_scale
        v = proj_ref[:, :, 2 * ret_w + lo:2 * ret_w + lo + dh]
        qd_scr[:, :, lo:lo + dh] = q * dq_ref[hd]
        kd_scr[:, :, lo:lo + dh] = k * dk_ref[hd]
        v_scr[:, :, lo:lo + dh] = v
        rows_out = []
        for i in range(n_tok):
            acc = None
            for j in range(i + 1):
                s_ij = jnp.sum(q[i] * k[j], axis=-1, keepdims=True)
                term = (s_ij * dm_ref[hd, i, j]) * v[j]
                acc = term if acc is None else acc + term
            rows_out.append(acc)
        inner_heads.append(jnp.stack(rows_out, axis=0))

    row = lax.broadcasted_iota(jnp.int32, (n_tok, SUBLANES, dh), 1)

    def group(g, carry):
        rows = pl.ds(pl.multiple_of(g * SUBLANES, SUBLANES), SUBLANES)
        for hd in range(heads):
            lo = hd * dh
            q_tile = qd_scr[:, rows, lo:lo + dh].reshape(n_tok * SUBLANES, dh).astype(BF16)
            k_tile = kd_scr[:, rows, lo:lo + dh].reshape(n_tok * SUBLANES, dh).astype(BF16)
            v_group = v_scr[:, rows, lo:lo + dh]
            cross = jnp.zeros((n_tok, SUBLANES, dh), F32)
            for r in range(SUBLANES):
                b = g * SUBLANES + r
                state = state_ref[b, hd]
                out = _dot(q_tile, state.astype(BF16)).reshape(n_tok, SUBLANES, dh)
                cross = jnp.where(row == r, out, cross)
                v_own = jnp.where(row == r, v_group, 0.0).reshape(n_tok * SUBLANES, dh).astype(BF16)
                upd = lax.dot_general(k_tile, v_own, (((0,), (0,)), ((), ())), preferred_element_type=F32)
                sout_ref[b, hd] = gc_ref[hd] * state + upd
            cross_scr[:, rows, lo:lo + dh] = cross
        return carry

    lax.fori_loop(0, tb // SUBLANES, group, 0)

    for hd in range(heads):
        lo = hd * dh
        o = inner_heads[hd] + cross_scr[:, :, lo:lo + dh]
        gate_pre = proj_ref[:, :, 3 * ret_w + lo:3 * ret_w + lo + dh]
        mixed_ref[:, :, lo:lo + dh] = _group_norm_gate(o, gate_pre, gn_ref[:, lo:lo + dh]).astype(BF16)

    u = proj_ref[:, :, o_a:o_a + conv_w] * jax.nn.sigmoid(proj_ref[:, :, o_b:o_b + conv_w])

    def window(s):
        return cin_ref[s] if s < n_buf else u[s - n_buf]

    for t in range(n_tok):
        acc = None
        for j in range(n_taps):
            term = window(t + j) * dww_ref[j:j + 1, :]
            acc = term if acc is None else acc + term
        y = acc + dwb_ref[...]
        mixed_ref[t, :, ret_w:ret_w + conv_w] = _layer_norm_silu(y, lng_ref[...], lnb_ref[...]).astype(BF16)
    for s in range(n_buf):
        cout_ref[s] = window(s + n_tok)


def _whole(shape):
    n = len(shape)
    return pl.BlockSpec(shape, lambda *_: (0,) * n)


def _mix_sample(x, mods, norm_gain, w_in, gn_gain, dw_w, dw_b, ln_g, ln_b, w_out, state_ret, conv_tm):
    n_tok, ns, d = x.shape
    tb = SAMPLE_MIX_SEQ_TILE
    heads, dh = state_ret.shape[1], state_ret.shape[2]
    ret_w = heads * dh
    n_taps, conv_w = dw_w.shape
    n_buf = n_taps - 1
    n_cols = w_in.shape[1]
    params = pltpu.CompilerParams(vmem_limit_bytes=VMEM_LIMIT_BYTES)

    proj = pl.pallas_call(
        _sample_proj_kernel,
        in_specs=[_whole(x.shape), _whole(mods.shape), _whole((1, d)), _whole(w_in.shape)],
        out_specs=_whole((n_tok, ns, n_cols)),
        out_shape=jax.ShapeDtypeStruct((n_tok, ns, n_cols), F32),
        grid=(1,),
        compiler_params=params,
        name="sample_proj",
    )(x, mods, norm_gain.reshape(1, d), w_in)

    cos2, sin2 = (_const(t) for t in _rotary_tables(PAST_LEN, n_tok, dh // 2))
    dmask, dq, dk, gc = _decay_tables(heads, n_tok)
    dm_b = _const(dmask[:, :, :, None, None], (heads, n_tok, n_tok, 1, dh))
    dq_b = _const(dq[:, :, None, None], (heads, n_tok, 1, dh))
    dk_b = _const(dk[:, :, None, None], (heads, n_tok, 1, dh))
    gc_b = _const(gc[:, None, None], (heads, 1, dh))
    mixed, state_new, conv_new = pl.pallas_call(
        functools.partial(_sample_core_kernel, heads=heads),
        grid=(ns // tb,),
        in_specs=[
            pl.BlockSpec((n_tok, tb, n_cols), lambda i: (0, i, 0)),
            pl.BlockSpec((tb, heads, dh, dh), lambda i: (i, 0, 0, 0)),
            pl.BlockSpec((n_buf, tb, conv_w), lambda i: (0, i, 0)),
            _resident((n_tok, 1, dh)),
            _resident((n_tok, 1, dh)),
            _resident((heads, n_tok, n_tok, 1, dh)),
            _resident((heads, n_tok, 1, dh)),
            _resident((heads, n_tok, 1, dh)),
            _resident((heads, 1, dh)),
            _resident((1, ret_w)),
            _resident((n_taps, conv_w)),
            _resident((1, conv_w)),
            _resident((1, conv_w)),
            _resident((1, conv_w)),
        ],
        out_specs=[
            pl.BlockSpec((n_tok, tb, ret_w + conv_w), lambda i: (0, i, 0)),
            pl.BlockSpec((tb, heads, dh, dh), lambda i: (i, 0, 0, 0)),
            pl.BlockSpec((n_buf, tb, conv_w), lambda i: (0, i, 0)),
        ],
        out_shape=[
            jax.ShapeDtypeStruct((n_tok, ns, ret_w + conv_w), BF16),
            jax.ShapeDtypeStruct(state_ret.shape, F32),
            jax.ShapeDtypeStruct(conv_tm.shape, F32),
        ],
        scratch_shapes=[pltpu.VMEM((n_tok, tb, ret_w), F32)] * 4,
        compiler_params=pltpu.CompilerParams(dimension_semantics=("arbitrary",),
                                             vmem_limit_bytes=VMEM_LIMIT_BYTES),
        name="sample_core",
    )(proj, state_ret, conv_tm, cos2.reshape(n_tok, 1, dh), sin2.reshape(n_tok, 1, dh), dm_b, dq_b, dk_b,
      gc_b, gn_gain.reshape(1, -1), dw_w, dw_b.reshape(1, -1), ln_g.reshape(1, -1), ln_b.reshape(1, -1))

    x_new = pl.pallas_call(
        _sample_out_kernel,
        in_specs=[_whole(x.shape), _whole(mods.shape), _whole(mixed.shape), _whole(w_out.shape)],
        out_specs=_whole(x.shape),
        out_shape=jax.ShapeDtypeStruct(x.shape, F32),
        grid=(1,),
        compiler_params=params,
        name="sample_out",
    )(x, mods, mixed, w_out)
    return x_new, state_new, conv_new


def _layer(xp, xs, mods_p, mods_s, sret, sconv, lw, final_gain, n_tok):
    (norm_ffn1, w1g, w1u, w1d, norm_mix, w_in, gn_gain, dw_w, dw_b, ln_g, ln_b, w_out,
     norm_ffn2, w2g, w2u, w2d) = lw
    nb, seq, d = xp.shape
    n_tok, ns, _ = xs.shape
    heads = sret.shape[1]
    mods_p = mods_p.reshape(N_MOD, nb, 1, d)
    mods_s_tm = mods_s.reshape(N_MOD, 1, ns, d)

    xp, xs = _ffn(xp, xs, mods_p[0:3], mods_s_tm[0:3], norm_ffn1, w1g, w1u, w1d, None)
    xp, ret_p, conv_p = _mix_prompt(xp, mods_p[3:6], norm_mix, w_in, gn_gain, dw_w, dw_b, ln_g, ln_b,
                                    w_out, heads)
    xs, ret_s, conv_s_tm = _mix_sample(xs, mods_s_tm[3:6], norm_mix, w_in, gn_gain, dw_w, dw_b, ln_g, ln_b,
                                       w_out, sret, sconv.transpose(1, 0, 2))
    xp, xs = _ffn(xp, xs, mods_p[6:9], mods_s_tm[6:9], norm_ffn2, w2g, w2u, w2d, final_gain)
    return xp, xs, ret_p, conv_p, ret_s, conv_s_tm.transpose(1, 0, 2)


def kernel(x_prompt, x_sample, c_prompt, c_sample, state_ret, state_conv, norm_ffn1, ffn1_w_gate,
           ffn1_w_up, ffn1_w_down, norm_mix, w_in, ret_gn_gain, dw_w, dw_b, conv_ln_gain, conv_ln_bias,
           w_out, norm_ffn2, ffn2_w_gate, ffn2_w_up, ffn2_w_down, w_ada, b_ada, norm_final):
    depth = w_in.shape[0]
    nb = x_prompt.shape[0]
    ns, n_tok, d = x_sample.shape
    assert n_tok <= SUBLANES and x_prompt.shape[1] % RET_CHUNK == 0

    xp = x_prompt
    xs = x_sample.transpose(1, 0, 2)
    c_all = jnp.concatenate([c_prompt, c_sample], axis=0)

    ret_p, conv_p, ret_s, conv_s = [], [], [], []
    for l in range(depth):
        ada = _ada(c_all, w_ada[l], b_ada[l])
        lw = (norm_ffn1[l], ffn1_w_gate[l], ffn1_w_up[l], ffn1_w_down[l], norm_mix[l],
              w_in[l].astype(BF16), ret_gn_gain[l], dw_w[l], dw_b[l], conv_ln_gain[l], conv_ln_bias[l],
              w_out[l].astype(BF16), norm_ffn2[l], ffn2_w_gate[l], ffn2_w_up[l], ffn2_w_down[l])
        final_gain = norm_final if l == depth - 1 else None
        xp, xs, rp, cp, rs, cs = _layer(xp, xs, ada[:, :nb], ada[:, nb:], state_ret[l], state_conv[l], lw,
                                        final_gain, n_tok)
        ret_p.append(rp)
        conv_p.append(cp)
        ret_s.append(rs)
        conv_s.append(cs)

    return (xp, xs.transpose(1, 0, 2), jnp.stack(ret_p), jnp.stack(conv_p), jnp.stack(ret_s),
            jnp.stack(conv_s))
```

```python
import functools

import jax
import jax.numpy as jnp
import numpy as np
from jax import lax
from jax.experimental import pallas as pl
from jax.experimental.pallas import tpu as pltpu

F32 = jnp.float32
BF16 = jnp.bfloat16

PAST_LEN = 16384
RET_CHUNK = 256
ROPE_BASE = 10000.0
EPS = 1e-6
N_MOD = 9

SUBLANES = 8
LANES = 128
VMEM_LIMIT_BYTES = 56 * 1024 * 1024

FFN_TOKEN_TILE = 512
MIX_TOKEN_TILE = 512
SAMPLE_MIX_SEQ_TILE = 16
CONV_ROW_BLOCKS = 16
ADA_MODS_PER_STEP = 3
FF_CHUNK = 256


def _resident(shape):
    n = len(shape)
    return pl.BlockSpec(shape, lambda *_: (0,) * n, pipeline_mode=pl.Buffered(1))


def _rmsnorm(x, gain):
    ms = jnp.mean(x * x, axis=-1, keepdims=True)
    return x * lax.rsqrt(ms + EPS) * gain


def _dot(a, b):
    return jnp.dot(a, b, preferred_element_type=F32)


def _ada_kernel(c_ref, w_ref, b_ref, o_ref):
    c = c_ref[...]
    h = (c * jax.nn.sigmoid(c)).astype(BF16)
    d = c.shape[1]
    for m in range(o_ref.shape[0]):
        cols = slice(m * d, (m + 1) * d)
        o_ref[m] = _dot(h, w_ref[:, cols].astype(BF16)) + b_ref[:, cols]


def _ada(c_all, w_ada, b_ada):
    nb, d = c_all.shape
    per_step = ADA_MODS_PER_STEP
    return pl.pallas_call(
        _ada_kernel,
        grid=(N_MOD // per_step,),
        in_specs=[
            pl.BlockSpec((nb, d), lambda j: (0, 0)),
            pl.BlockSpec((d, per_step * d), lambda j: (0, j)),
            pl.BlockSpec((1, per_step * d), lambda j: (0, j)),
        ],
        out_specs=pl.BlockSpec((per_step, nb, d), lambda j: (j, 0, 0)),
        out_shape=jax.ShapeDtypeStruct((N_MOD, nb, d), F32),
        compiler_params=pltpu.CompilerParams(dimension_semantics=("arbitrary",),
                                             vmem_limit_bytes=VMEM_LIMIT_BYTES),
        name="ada",
    )(c_all, w_ada, b_ada.reshape(1, -1))


def _ffn_kernel(xp_ref, xs_ref, modp_ref, mods_ref, ng_ref, wg_ref, wu_ref, wd_ref, *rest,
                n_stage, n_prompt, per_seq, final_norm, conv_stage):
    rest = list(rest)
    fg_ref = rest.pop(0) if final_norm else None
    if conv_stage:
        mod2_ref, ng2_ref, wab_ref, dww_ref, dwb_ref, lng_ref, lnb_ref = rest[:7]
        del rest[:7]
        op_ref, os_ref, hb_ref, conv_ref, cout_ref = rest[:5]
        wg_scr, wu_scr, wd_scr, act_scr, hbf_scr, x1_scr, u_scr = rest[5:]
    else:
        op_ref, os_ref, wg_scr, wu_scr, wd_scr, act_scr, hbf_scr = rest
    i = pl.program_id(0)
    k = i - n_stage
    n_chunks, _, ffc = wg_scr.shape

    @pl.when(i < n_stage)
    def _():
        wg_scr[i] = wg_ref[...].astype(BF16)
        wu_scr[i] = wu_ref[...].astype(BF16)
        wd_scr[pl.ds(pl.multiple_of(i * ffc, ffc), ffc), :] = wd_ref[...].astype(BF16)

    def ffn_tile(x_ref, mod_ref, o_ref, keep, before_chunks=None, side_work=()):
        x = x_ref[...]
        g_, r_, d_ = x.shape
        shift, scale, gate = mod_ref[0], mod_ref[1], mod_ref[2]
        h = _rmsnorm(x, ng_ref[...]) * (1.0 + scale) + shift
        rows = g_ * r_
        hbf_scr[0:rows, :] = h.astype(BF16).reshape(rows, d_)
        if before_chunks is not None:
            before_chunks()
        side_work = list(side_work)
        for c in range(n_chunks):
            g = _dot(hbf_scr[0:rows, :], wg_scr[c])
            u = _dot(hbf_scr[0:rows, :], wu_scr[c])
            act_scr[0:rows, c * ffc:(c + 1) * ffc] = (g * jax.nn.sigmoid(g) * u).astype(BF16)
            n_now = 1 if c < n_chunks - 1 else len(side_work)
            for _ in range(min(n_now, len(side_work))):
                side_work.pop(0)(g[0:SUBLANES, 0:LANES])
        y = _dot(act_scr[0:rows, :], wd_scr[...]).reshape(g_, r_, d_)
        out = x + 0.5 * gate * y
        if final_norm:
            out = _rmsnorm(out, fg_ref[...])
        o_ref[...] = out
        if keep:
            x1_scr[...] = out.reshape(rows, d_)

    def conv_project():
        conv_w = dww_ref.shape[1]
        n_slabs, win_rows, _ = u_scr.shape
        tm = x1_scr.shape[0]
        hist = win_rows - tm
        x1 = x1_scr[...]
        h = _rmsnorm(x1, ng2_ref[...]) * (1.0 + mod2_ref[1, 0]) + mod2_ref[0, 0]
        hb = h.astype(BF16)
        hb_ref[0] = hb
        ab = _dot(hb, wab_ref[...])
        u = ab[:, :conv_w] * jax.nn.sigmoid(ab[:, conv_w:])
        for s in range(n_slabs):
            u_scr[s, hist:hist + tm, :] = u[:, s * LANES:(s + 1) * LANES]

    def conv_rows(r0, n_rows, anchor):
        n_taps = dww_ref.shape[0]
        n_slabs, win_rows, _ = u_scr.shape
        lead = win_rows - x1_scr.shape[0] - (n_taps - 1)
        zero = ((pltpu.bitcast(anchor, jnp.uint32) >> 16) >> 16).astype(F32)[0:1, :]
        y = jnp.concatenate([_conv_taps_slab(u_scr, dww_ref, s, n_rows, lead + r0, zero)
                             for s in range(n_slabs)], axis=-1)
        conv_ref[0, r0:r0 + n_rows, :] = _layer_norm_silu(y + dwb_ref[...], lng_ref[...],
                                                          lnb_ref[...]).astype(BF16)

    def conv_carry():
        n_slabs, win_rows, _ = u_scr.shape
        tm = x1_scr.shape[0]
        for s in range(n_slabs):
            u_scr[s, 0:win_rows - tm, :] = u_scr[s, tm:win_rows, :]

    def ffn_tile_with_conv(x_ref, mod_ref, o_ref, keep):
        tm = x1_scr.shape[0]
        n_rows = tm // CONV_ROW_BLOCKS
        ffn_tile(x_ref, mod_ref, o_ref, keep, conv_project,
                 [functools.partial(conv_rows, r * n_rows, n_rows) for r in range(CONV_ROW_BLOCKS)])
        conv_carry()

    if not conv_stage:
        @pl.when(jnp.logical_and(k >= 0, k < n_prompt))
        def _():
            ffn_tile(xp_ref, modp_ref, op_ref, False)

        @pl.when(k == n_prompt)
        def _():
            ffn_tile(xs_ref, mods_ref, os_ref, False)
        return

    pos = lax.rem(k + per_seq - 1, per_seq)
    has_conv = jnp.logical_and(k >= 1, k <= n_prompt)

    @pl.when(jnp.logical_and(has_conv, pos == 0))
    def _():
        u_scr[:, 0:u_scr.shape[1] - x1_scr.shape[0], :] = jnp.zeros(
            (u_scr.shape[0], u_scr.shape[1] - x1_scr.shape[0], LANES), F32)

    @pl.when(k == 0)
    def _():
        ffn_tile(xp_ref, modp_ref, op_ref, True)

    @pl.when(jnp.logical_and(k >= 1, k < n_prompt))
    def _():
        ffn_tile_with_conv(xp_ref, modp_ref, op_ref, True)

    @pl.when(k == n_prompt)
    def _():
        ffn_tile_with_conv(xs_ref, mods_ref, os_ref, False)

    @pl.when(jnp.logical_and(has_conv, pos == per_seq - 1))
    def _():
        n_taps = dww_ref.shape[0]
        hist = u_scr.shape[1] - x1_scr.shape[0]
        for s in range(u_scr.shape[0]):
            cout_ref[0, :, s * LANES:(s + 1) * LANES] = u_scr[s, hist - (n_taps - 1):hist, :]


def _ffn(xp, xs, mods_p, mods_s, norm_gain, wg, wu, wd, final_gain=None, conv_stage=None):
    nb, seq, d = xp.shape
    d_ff = wg.shape[1]
    tm = FFN_TOKEN_TILE
    ffc = FF_CHUNK
    assert seq % tm == 0 and d_ff % ffc == 0 and xs.shape[0] * xs.shape[1] <= tm
    per_seq = seq // tm
    n_stage, n_prompt = d_ff // ffc, nb * per_seq
    final_norm = final_gain is not None

    def prompt_tile(i):
        return jnp.clip(i - n_stage, 0, n_prompt - 1)

    def conv_tile(i):
        return jnp.clip(i - n_stage - 1, 0, n_prompt - 1)

    def stage(i):
        return jnp.minimum(i, n_stage - 1)

    x_spec = pl.BlockSpec((1, tm, d), lambda i: (prompt_tile(i) // per_seq, prompt_tile(i) % per_seq, 0))
    in_specs = [
        x_spec,
        _resident(xs.shape),
        pl.BlockSpec((3, 1, 1, d), lambda i: (0, prompt_tile(i) // per_seq, 0, 0)),
        _resident(mods_s.shape),
        _resident((1, d)),
        pl.BlockSpec((d, ffc), lambda i: (0, stage(i))),
        pl.BlockSpec((d, ffc), lambda i: (0, stage(i))),
        pl.BlockSpec((ffc, d), lambda i: (stage(i), 0)),
    ]
    args = [xp, xs, mods_p, mods_s, norm_gain.reshape(1, d), wg, wu, wd]
    out_specs = [x_spec, pl.BlockSpec(xs.shape, lambda i: (0, 0, 0))]
    out_shape = [jax.ShapeDtypeStruct(xp.shape, F32), jax.ShapeDtypeStruct(xs.shape, F32)]
    scratch = [
        pltpu.VMEM((n_stage, d, ffc), BF16),
        pltpu.VMEM((n_stage, d, ffc), BF16),
        pltpu.VMEM((d_ff, d), BF16),
        pltpu.VMEM((tm, d_ff), BF16),
        pltpu.VMEM((tm, d), BF16),
    ]
    if final_norm:
        in_specs.append(_resident((1, d)))
        args.append(final_gain.reshape(1, d))
    if conv_stage is not None:
        mods2_p, norm_mix, w_in, dw_w, dw_b, ln_g, ln_b = conv_stage
        n_taps, conv_w = dw_w.shape
        hist = -(-(n_taps - 1) // SUBLANES) * SUBLANES
        ab_block = (w_in.shape[1] - 2 * conv_w) // (2 * conv_w)
        assert ab_block * 2 * conv_w == w_in.shape[1] - 2 * conv_w
        in_specs += [
            pl.BlockSpec((3, 1, 1, d), lambda i: (0, conv_tile(i) // per_seq, 0, 0)),
            _resident((1, d)),
            pl.BlockSpec((d, 2 * conv_w), lambda i: (0, ab_block), pipeline_mode=pl.Buffered(1)),
            _resident((n_taps, conv_w)),
            _resident((1, conv_w)),
            _resident((1, conv_w)),
            _resident((1, conv_w)),
        ]
        args += [mods2_p, norm_mix.reshape(1, d), w_in, dw_w, dw_b.reshape(1, -1), ln_g.reshape(1, -1),
                 ln_b.reshape(1, -1)]
        out_specs += [
            pl.BlockSpec((1, tm, d), lambda i: (conv_tile(i) // per_seq, conv_tile(i) % per_seq, 0)),
            pl.BlockSpec((1, tm, conv_w), lambda i: (conv_tile(i) // per_seq, conv_tile(i) % per_seq, 0)),
            pl.BlockSpec((1, n_taps - 1, conv_w), lambda i: (conv_tile(i) // per_seq, 0, 0)),
        ]
        out_shape += [
            jax.ShapeDtypeStruct((nb, seq, d), BF16),
            jax.ShapeDtypeStruct((nb, seq, conv_w), BF16),
            jax.ShapeDtypeStruct((nb, n_taps - 1, conv_w), F32),
        ]
        scratch += [pltpu.VMEM((tm, d), F32), pltpu.VMEM((conv_w // LANES, hist + tm, LANES), F32)]
    return pl.pallas_call(
        functools.partial(_ffn_kernel, n_stage=n_stage, n_prompt=n_prompt, per_seq=per_seq,
                          final_norm=final_norm, conv_stage=conv_stage is not None),
        grid=(n_stage + n_prompt + 1,),
        in_specs=in_specs,
        out_specs=out_specs,
        out_shape=out_shape,
        scratch_shapes=scratch,
        compiler_params=pltpu.CompilerParams(dimension_semantics=("arbitrary",),
                                             vmem_limit_bytes=VMEM_LIMIT_BYTES),
        name="ffn_final" if final_norm else "ffn",
    )(*args)


def _rotate(xh, cos2, sin2):
    return xh * cos2 + pltpu.roll(xh, xh.shape[-1] // 2, axis=xh.ndim - 1) * sin2


def _group_norm_gate(o, gate_pre, gain):
    mu = jnp.mean(o, axis=-1, keepdims=True)
    ctr = o - mu
    var = jnp.mean(ctr * ctr, axis=-1, keepdims=True)
    return gate_pre * jax.nn.sigmoid(gate_pre) * (ctr * lax.rsqrt(var + EPS) * gain)


def _layer_norm_silu(y, gain, bias):
    mu = jnp.mean(y, axis=-1, keepdims=True)
    ctr = y - mu
    var = jnp.mean(ctr * ctr, axis=-1, keepdims=True)
    yn = ctr * lax.rsqrt(var + EPS) * gain + bias
    return yn * jax.nn.sigmoid(yn)


def _conv_taps_slab(win_ref, dww_ref, s, n_out, lead, w_offset):
    acc = None
    for j in range(dww_ref.shape[0]):
        w_row = dww_ref[j:j + 1, s * LANES:(s + 1) * LANES] + w_offset
        term = win_ref[s, pl.ds(lead + j, n_out), :] * w_row
        acc = term if acc is None else acc + term
    return acc


def _decay_tables(heads, chunk):
    lg = np.log(1.0 - 2.0 ** (-5.0 - np.arange(heads, dtype=np.float64)))
    idx = np.arange(chunk, dtype=np.float64)
    diff = idx[:, None] - idx[None, :]
    dmask = np.where(diff[None] >= 0, np.exp(lg[:, None, None] * np.maximum(diff, 0.0)[None]), 0.0)
    dq = np.exp(lg[:, None] * (idx[None, :] + 1.0))
    dk = np.exp(lg[:, None] * (chunk - 1.0 - idx[None, :]))
    gc = np.exp(lg * chunk)
    return dmask, dq, dk, gc


def _rotary_tables(pos0, n_pos, half):
    inv = ROPE_BASE ** (-np.arange(half, dtype=np.float64) / half)
    pos = (pos0 + np.arange(n_pos)).astype(np.float64)
    ang = pos[:, None] * inv[None, :]
    cos, sin = np.cos(ang), np.sin(ang)
    return np.concatenate([cos, cos], axis=-1), np.concatenate([-sin, sin], axis=-1)


def _const(a, shape=None):
    a = np.asarray(a, dtype=np.float32)
    if shape is not None:
        a = np.ascontiguousarray(np.broadcast_to(a, shape))
    return jnp.asarray(a)


def _retention_kernel(x_ref, mod_ref, hb_ref, conv_ref, wqkvg_ref, gn_ref, wout_ref, cos_ref, sin_ref,
                      dmask_ref, dq_ref, dk_ref, gc_ref, o_ref, sout_ref, s_scr, mix_scr, *, heads):
    t_idx = pl.program_id(1)
    tt = x_ref.shape[1]
    ret_w = gn_ref.shape[1]
    dh = ret_w // heads
    conv_w = conv_ref.shape[2]
    k_scale = dh ** -0.5

    @pl.when(t_idx == 0)
    def _():
        s_scr[...] = jnp.zeros_like(s_scr)

    proj = _dot(hb_ref[0], wqkvg_ref[...])
    cos2, sin2 = cos_ref[...], sin_ref[...]
    for hd in range(heads):
        lo = hd * dh
        q = _rotate(proj[:, lo:lo + dh], cos2, sin2)
        k = _rotate(proj[:, ret_w + lo:ret_w + lo + dh], cos2, sin2) * k_scale
        v = proj[:, 2 * ret_w + lo:2 * ret_w + lo + dh]
        gate_pre = proj[:, 3 * ret_w + lo:3 * ret_w + lo + dh]
        dmask, dq, dk, gc = dmask_ref[hd], dq_ref[hd], dk_ref[hd], gc_ref[hd]
        state = s_scr[hd]
        outs = []
        for c0 in range(0, tt, RET_CHUNK):
            qc, kc = q[c0:c0 + RET_CHUNK], k[c0:c0 + RET_CHUNK]
            vb = v[c0:c0 + RET_CHUNK].astype(BF16)
            scores = lax.dot_general(qc.astype(BF16), kc.astype(BF16), (((1,), (1,)), ((), ())),
                                     preferred_element_type=F32)
            inner = _dot((scores * dmask).astype(BF16), vb)
            cross = _dot((qc * dq).astype(BF16), state.astype(BF16))
            outs.append(inner + cross)
            state = gc * state + lax.dot_general((kc * dk).astype(BF16), vb, (((0,), (0,)), ((), ())),
                                                 preferred_element_type=F32)
        s_scr[hd] = state
        o = jnp.concatenate(outs, axis=0) if len(outs) > 1 else outs[0]
        mix_scr[:, lo:lo + dh] = _group_norm_gate(o, gate_pre, gn_ref[:, lo:lo + dh]).astype(BF16)

    mixed = (_dot(mix_scr[...], wout_ref[0:ret_w, :])
             + _dot(conv_ref[0], wout_ref[ret_w:ret_w + conv_w, :]))
    o_ref[0] = x_ref[0] + mod_ref[2, 0] * mixed

    @pl.when(t_idx == pl.num_programs(1) - 1)
    def _():
        sout_ref[0] = s_scr[...]


def _retention(x, mods, hb, conv_out, w_in, gn_gain, w_out, heads):
    nb, seq, d = x.shape
    tt = MIX_TOKEN_TILE
    ret_w = gn_gain.shape[0]
    dh = ret_w // heads
    conv_w = conv_out.shape[2]
    chunk = RET_CHUNK
    cos2, sin2 = (_const(t) for t in _rotary_tables(0, seq, dh // 2))
    dmask, dq, dk, gc = _decay_tables(heads, chunk)
    return pl.pallas_call(
        functools.partial(_retention_kernel, heads=heads),
        grid=(nb, seq // tt),
        in_specs=[
            pl.BlockSpec((1, tt, d), lambda b, t: (b, t, 0)),
            pl.BlockSpec((3, 1, 1, d), lambda b, t: (0, b, 0, 0)),
            pl.BlockSpec((1, tt, d), lambda b, t: (b, t, 0)),
            pl.BlockSpec((1, tt, conv_w), lambda b, t: (b, t, 0)),
            pl.BlockSpec((d, 4 * ret_w), lambda b, t: (0, 0), pipeline_mode=pl.Buffered(1)),
            _resident((1, ret_w)),
            _resident(w_out.shape),
            pl.BlockSpec((tt, dh), lambda b, t: (t, 0)),
            pl.BlockSpec((tt, dh), lambda b, t: (t, 0)),
            _resident((heads, chunk, chunk)),
            _resident((heads, chunk, dh)),
            _resident((heads, chunk, dh)),
            _resident((heads, 1, dh)),
        ],
        out_specs=[
            pl.BlockSpec((1, tt, d), lambda b, t: (b, t, 0)),
            pl.BlockSpec((1, heads, dh, dh), lambda b, t: (b, 0, 0, 0)),
        ],
        out_shape=[
            jax.ShapeDtypeStruct(x.shape, F32),
            jax.ShapeDtypeStruct((nb, heads, dh, dh), F32),
        ],
        scratch_shapes=[
            pltpu.VMEM((heads, dh, dh), F32),
            pltpu.VMEM((tt, ret_w), BF16),
        ],
        compiler_params=pltpu.CompilerParams(dimension_semantics=("arbitrary", "arbitrary"),
                                             vmem_limit_bytes=VMEM_LIMIT_BYTES),
        name="retention",
    )(x, mods, hb, conv_out, w_in, gn_gain.reshape(1, -1), w_out, cos2, sin2, _const(dmask),
      _const(dq[:, :, None], (heads, chunk, dh)), _const(dk[:, :, None], (heads, chunk, dh)),
      _const(gc[:, None, None], (heads, 1, dh)))


def _sample_proj_kernel(x_ref, mod_ref, ng_ref, win_ref, o_ref):
    x = x_ref[...]
    n_tok, ns, d_ = x.shape
    h = _rmsnorm(x, ng_ref[...]) * (1.0 + mod_ref[1]) + mod_ref[0]
    o_ref[...] = _dot(h.astype(BF16).reshape(n_tok * ns, d_), win_ref[...]).reshape(n_tok, ns, -1)


def _sample_out_kernel(x_ref, mod_ref, mixed_ref, wout_ref, o_ref):
    n_tok, ns, d_ = x_ref.shape
    y = _dot(mixed_ref[...].reshape(n_tok * ns, -1), wout_ref[...]).reshape(n_tok, ns, d_)
    o_ref[...] = x_ref[...] + mod_ref[2] * y


def _sample_core_kernel(proj_ref, state_ref, cin_ref, cos_ref, sin_ref, dm_ref, dq_ref, dk_ref, gc_ref,
                        gn_ref, dww_ref, dwb_ref, lng_ref, lnb_ref, mixed_ref, sout_ref, cout_ref,
                        qd_scr, kd_scr, v_scr, cross_scr, *, heads):
    n_tok, tb, _ = proj_ref.shape
    ret_w = gn_ref.shape[1]
    dh = ret_w // heads
    n_taps, conv_w = dww_ref.shape
    n_buf = n_taps - 1
    o_a, o_b = 4 * ret_w, 4 * ret_w + conv_w
    k_scale = dh ** -0.5
    cos2, sin2 = cos_ref[...], sin_ref[...]

    inner_heads = []
    for hd in range(heads):
        lo = hd * dh
        q = _rotate(proj_ref[:, :, lo:lo + dh], cos2, sin2)
        k = _rotate(proj_ref[:, :, ret_w + lo:ret_w + lo + dh], cos2, sin2) * k_scale
        v = proj_ref[:, :, 2 * ret_w + lo:2 * ret_w + lo + dh]
        qd_scr[:, :, lo:lo + dh] = q * dq_ref[hd]
        kd_scr[:, :, lo:lo + dh] = k * dk_ref[hd]
        v_scr[:, :, lo:lo + dh] = v
        rows_out = []
        for i in range(n_tok):
            acc = None
            for j in range(i + 1):
                s_ij = jnp.sum(q[i] * k[j], axis=-1, keepdims=True)
                term = (s_ij * dm_ref[hd, i, j]) * v[j]
                acc = term if acc is None else acc + term
            rows_out.append(acc)
        inner_heads.append(jnp.stack(rows_out, axis=0))

    row = lax.broadcasted_iota(jnp.int32, (n_tok, SUBLANES, dh), 1)

    def group(g, carry):
        rows = pl.ds(pl.multiple_of(g * SUBLANES, SUBLANES), SUBLANES)
        for hd in range(heads):
            lo = hd * dh
            q_tile = qd_scr[:, rows, lo:lo + dh].reshape(n_tok * SUBLANES, dh).astype(BF16)
            k_tile = kd_scr[:, rows, lo:lo + dh].reshape(n_tok * SUBLANES, dh).astype(BF16)
            v_group = v_scr[:, rows, lo:lo + dh]
            cross = jnp.zeros((n_tok, SUBLANES, dh), F32)
            for r in range(SUBLANES):
                b = g * SUBLANES + r
                state = state_ref[b, hd]
                out = _dot(q_tile, state.astype(BF16)).reshape(n_tok, SUBLANES, dh)
                cross = jnp.where(row == r, out, cross)
                v_own = jnp.where(row == r, v_group, 0.0).reshape(n_tok * SUBLANES, dh).astype(BF16)
                upd = lax.dot_general(k_tile, v_own, (((0,), (0,)), ((), ())), preferred_element_type=F32)
                sout_ref[b, hd] = gc_ref[hd] * state + upd
            cross_scr[:, rows, lo:lo + dh] = cross
        return carry

    lax.fori_loop(0, tb // SUBLANES, group, 0)

    for hd in range(heads):
        lo = hd * dh
        o = inner_heads[hd] + cross_scr[:, :, lo:lo + dh]
        gate_pre = proj_ref[:, :, 3 * ret_w + lo:3 * ret_w + lo + dh]
        mixed_ref[:, :, lo:lo + dh] = _group_norm_gate(o, gate_pre, gn_ref[:, lo:lo + dh]).astype(BF16)

    u = proj_ref[:, :, o_a:o_a + conv_w] * jax.nn.sigmoid(proj_ref[:, :, o_b:o_b + conv_w])

    def window(s):
        return cin_ref[s] if s < n_buf else u[s - n_buf]

    for t in range(n_tok):
        acc = None
        for j in range(n_taps):
            term = window(t + j) * dww_ref[j:j + 1, :]
            acc = term if acc is None else acc + term
        y = acc + dwb_ref[...]
        mixed_ref[t, :, ret_w:ret_w + conv_w] = _layer_norm_silu(y, lng_ref[...], lnb_ref[...]).astype(BF16)
    for s in range(n_buf):
        cout_ref[s] = window(s + n_tok)


def _whole(shape):
    n = len(shape)
    return pl.BlockSpec(shape, lambda *_: (0,) * n)


def _mix_sample(x, mods, norm_gain, w_in, gn_gain, dw_w, dw_b, ln_g, ln_b, w_out, state_ret, conv_tm):
    n_tok, ns, d = x.shape
    tb = SAMPLE_MIX_SEQ_TILE
    heads, dh = state_ret.shape[1], state_ret.shape[2]
    ret_w = heads * dh
    n_taps, conv_w = dw_w.shape
    n_buf = n_taps - 1
    n_cols = w_in.shape[1]
    params = pltpu.CompilerParams(vmem_limit_bytes=VMEM_LIMIT_BYTES)

    proj = pl.pallas_call(
        _sample_proj_kernel,
        in_specs=[_whole(x.shape), _whole(mods.shape), _whole((1, d)), _whole(w_in.shape)],
        out_specs=_whole((n_tok, ns, n_cols)),
        out_shape=jax.ShapeDtypeStruct((n_tok, ns, n_cols), F32),
        grid=(1,),
        compiler_params=params,
        name="sample_proj",
    )(x, mods, norm_gain.reshape(1, d), w_in)

    cos2, sin2 = (_const(t) for t in _rotary_tables(PAST_LEN, n_tok, dh // 2))
    dmask, dq, dk, gc = _decay_tables(heads, n_tok)
    dm_b = _const(dmask[:, :, :, None, None], (heads, n_tok, n_tok, 1, dh))
    dq_b = _const(dq[:, :, None, None], (heads, n_tok, 1, dh))
    dk_b = _const(dk[:, :, None, None], (heads, n_tok, 1, dh))
    gc_b = _const(gc[:, None, None], (heads, 1, dh))
    mixed, state_new, conv_new = pl.pallas_call(
        functools.partial(_sample_core_kernel, heads=heads),
        grid=(ns // tb,),
        in_specs=[
            pl.BlockSpec((n_tok, tb, n_cols), lambda i: (0, i, 0)),
            pl.BlockSpec((tb, heads, dh, dh), lambda i: (i, 0, 0, 0)),
            pl.BlockSpec((n_buf, tb, conv_w), lambda i: (0, i, 0)),
            _resident((n_tok, 1, dh)),
            _resident((n_tok, 1, dh)),
            _resident((heads, n_tok, n_tok, 1, dh)),
            _resident((heads, n_tok, 1, dh)),
            _resident((heads, n_tok, 1, dh)),
            _resident((heads, 1, dh)),
            _resident((1, ret_w)),
            _resident((n_taps, conv_w)),
            _resident((1, conv_w)),
            _resident((1, conv_w)),
            _resident((1, conv_w)),
        ],
        out_specs=[
            pl.BlockSpec((n_tok, tb, ret_w + conv_w), lambda i: (0, i, 0)),
            pl.BlockSpec((tb, heads, dh, dh), lambda i: (i, 0, 0, 0)),
            pl.BlockSpec((n_buf, tb, conv_w), lambda i: (0, i, 0)),
        ],
        out_shape=[
            jax.ShapeDtypeStruct((n_tok, ns, ret_w + conv_w), BF16),
            jax.ShapeDtypeStruct(state_ret.shape, F32),
            jax.ShapeDtypeStruct(conv_tm.shape, F32),
        ],
        scratch_shapes=[pltpu.VMEM((n_tok, tb, ret_w), F32)] * 4,
        compiler_params=pltpu.CompilerParams(dimension_semantics=("arbitrary",),
                                             vmem_limit_bytes=VMEM_LIMIT_BYTES),
        name="sample_core",
    )(proj, state_ret, conv_tm, cos2.reshape(n_tok, 1, dh), sin2.reshape(n_tok, 1, dh), dm_b, dq_b, dk_b,
      gc_b, gn_gain.reshape(1, -1), dw_w, dw_b.reshape(1, -1), ln_g.reshape(1, -1), ln_b.reshape(1, -1))

    x_new = pl.pallas_call(
        _sample_out_kernel,
        in_specs=[_whole(x.shape), _whole(mods.shape), _whole(mixed.shape), _whole(w_out.shape)],
        out_specs=_whole(x.shape),
        out_shape=jax.ShapeDtypeStruct(x.shape, F32),
        grid=(1,),
        compiler_params=params,
        name="sample_out",
    )(x, mods, mixed, w_out)
    return x_new, state_new, conv_new


def _layer(xp, xs, mods_p, mods_s, sret, sconv, lw, final_gain, n_tok):
    (norm_ffn1, w1g, w1u, w1d, norm_mix, w_in, gn_gain, dw_w, dw_b, ln_g, ln_b, w_out,
     norm_ffn2, w2g, w2u, w2d) = lw
    nb, seq, d = xp.shape
    n_tok, ns, _ = xs.shape
    heads = sret.shape[1]
    mods_p = mods_p.reshape(N_MOD, nb, 1, d)
    mods_s_tm = mods_s.reshape(N_MOD, 1, ns, d)

    xp, xs, hb_p, conv_out_p, conv_p = _ffn(
        xp, xs, mods_p[0:3], mods_s_tm[0:3], norm_ffn1, w1g, w1u, w1d,
        conv_stage=(mods_p[3:6], norm_mix, w_in, dw_w, dw_b, ln_g, ln_b))
    xp, ret_p = _retention(xp, mods_p[3:6], hb_p, conv_out_p, w_in, gn_gain, w_out, heads)
    xs, ret_s, conv_s_tm = _mix_sample(xs, mods_s_tm[3:6], norm_mix, w_in, gn_gain, dw_w, dw_b, ln_g, ln_b,
                                       w_out, sret, sconv.transpose(1, 0, 2))
    xp, xs = _ffn(xp, xs, mods_p[6:9], mods_s_tm[6:9], norm_ffn2, w2g, w2u, w2d, final_gain=final_gain)
    return xp, xs, ret_p, conv_p, ret_s, conv_s_tm.transpose(1, 0, 2)


def kernel(x_prompt, x_sample, c_prompt, c_sample, state_ret, state_conv, norm_ffn1, ffn1_w_gate,
           ffn1_w_up, ffn1_w_down, norm_mix, w_in, ret_gn_gain, dw_w, dw_b, conv_ln_gain, conv_ln_bias,
           w_out, norm_ffn2, ffn2_w_gate, ffn2_w_up, ffn2_w_down, w_ada, b_ada, norm_final):
    depth = w_in.shape[0]
    nb = x_prompt.shape[0]
    ns, n_tok, d = x_sample.shape
    assert n_tok <= SUBLANES and x_prompt.shape[1] % RET_CHUNK == 0

    xp = x_prompt
    xs = x_sample.transpose(1, 0, 2)
    c_all = jnp.concatenate([c_prompt, c_sample], axis=0)

    ret_p, conv_p, ret_s, conv_s = [], [], [], []
    for l in range(depth):
        ada = _ada(c_all, w_ada[l], b_ada[l])
        lw = (norm_ffn1[l], ffn1_w_gate[l], ffn1_w_up[l], ffn1_w_down[l], norm_mix[l],
              w_in[l].astype(BF16), ret_gn_gain[l], dw_w[l], dw_b[l], conv_ln_gain[l], conv_ln_bias[l],
              w_out[l].astype(BF16), norm_ffn2[l], ffn2_w_gate[l], ffn2_w_up[l], ffn2_w_down[l])
        final_gain = norm_final if l == depth - 1 else None
        xp, xs, rp, cp, rs, cs = _layer(xp, xs, ada[:, :nb], ada[:, nb:], state_ret[l], state_conv[l], lw,
                                        final_gain, n_tok)
        ret_p.append(rp)
        conv_p.append(cp)
        ret_s.append(rs)
        conv_s.append(cs)

    return (xp, xs.transpose(1, 0, 2), jnp.stack(ret_p), jnp.stack(conv_p), jnp.stack(ret_s),
            jnp.stack(conv_s))
```

```python
import functools

import jax
import jax.numpy as jnp
import numpy as np
from jax import lax
from jax.experimental import pallas as pl
from jax.experimental.pallas import tpu as pltpu

F32 = jnp.float32
BF16 = jnp.bfloat16

PAST_LEN = 16384
RET_CHUNK = 256
ROPE_BASE = 10000.0
EPS = 1e-6
N_MOD = 9

SUBLANES = 8
LANES = 128
VMEM_LIMIT_BYTES = 56 * 1024 * 1024

FFN_TOKEN_TILE = 512
MIX_TOKEN_TILE = 512
SAMPLE_MIX_SEQ_TILE = 16
ADA_MODS_PER_STEP = 3
FF_CHUNK = 256


def _resident(shape):
    n = len(shape)
    return pl.BlockSpec(shape, lambda *_: (0,) * n, pipeline_mode=pl.Buffered(1))


def _rmsnorm(x, gain):
    ms = jnp.mean(x * x, axis=-1, keepdims=True)
    return x * lax.rsqrt(ms + EPS) * gain


def _dot(a, b):
    return jnp.dot(a, b, preferred_element_type=F32)


def _ada_kernel(c_ref, w_ref, b_ref, o_ref):
    c = c_ref[...]
    h = (c * jax.nn.sigmoid(c)).astype(BF16)
    d = c.shape[1]
    for m in range(o_ref.shape[0]):
        cols = slice(m * d, (m + 1) * d)
        o_ref[m] = _dot(h, w_ref[:, cols].astype(BF16)) + b_ref[:, cols]


def _ada(c_all, w_ada, b_ada):
    nb, d = c_all.shape
    per_step = ADA_MODS_PER_STEP
    return pl.pallas_call(
        _ada_kernel,
        grid=(N_MOD // per_step,),
        in_specs=[
            pl.BlockSpec((nb, d), lambda j: (0, 0)),
            pl.BlockSpec((d, per_step * d), lambda j: (0, j)),
            pl.BlockSpec((1, per_step * d), lambda j: (0, j)),
        ],
        out_specs=pl.BlockSpec((per_step, nb, d), lambda j: (j, 0, 0)),
        out_shape=jax.ShapeDtypeStruct((N_MOD, nb, d), F32),
        compiler_params=pltpu.CompilerParams(dimension_semantics=("arbitrary",),
                                             vmem_limit_bytes=VMEM_LIMIT_BYTES),
        name="ada",
    )(c_all, w_ada, b_ada.reshape(1, -1))


def _ffn_kernel(xp_ref, xs_ref, modp_ref, mods_ref, ng_ref, wg_ref, wu_ref, wd_ref, *rest,
                n_stage, n_prompt, final_norm):
    if final_norm:
        fg_ref, op_ref, os_ref, wg_scr, wu_scr, wd_scr, act_scr, hb_scr = rest
    else:
        op_ref, os_ref, wg_scr, wu_scr, wd_scr, act_scr, hb_scr = rest
    i = pl.program_id(0)
    n_chunks, _, ffc = wg_scr.shape

    @pl.when(i < n_stage)
    def _():
        wg_scr[i] = wg_ref[...].astype(BF16)
        wu_scr[i] = wu_ref[...].astype(BF16)
        wd_scr[pl.ds(pl.multiple_of(i * ffc, ffc), ffc), :] = wd_ref[...].astype(BF16)

    def tile(x_ref, mod_ref, o_ref):
        x = x_ref[...]
        g_, r_, d_ = x.shape
        shift, scale, gate = mod_ref[0], mod_ref[1], mod_ref[2]
        h = _rmsnorm(x, ng_ref[...]) * (1.0 + scale) + shift
        rows = g_ * r_
        hb_scr[0:rows, :] = h.astype(BF16).reshape(rows, d_)
        for c in range(n_chunks):
            g = _dot(hb_scr[0:rows, :], wg_scr[c])
            u = _dot(hb_scr[0:rows, :], wu_scr[c])
            act_scr[0:rows, c * ffc:(c + 1) * ffc] = (g * jax.nn.sigmoid(g) * u).astype(BF16)
        y = _dot(act_scr[0:rows, :], wd_scr[...]).reshape(g_, r_, d_)
        out = x + 0.5 * gate * y
        if final_norm:
            out = _rmsnorm(out, fg_ref[...])
        o_ref[...] = out

    @pl.when(jnp.logical_and(i >= n_stage, i < n_stage + n_prompt))
    def _():
        tile(xp_ref, modp_ref, op_ref)

    @pl.when(i == n_stage + n_prompt)
    def _():
        tile(xs_ref, mods_ref, os_ref)


def _ffn(xp, xs, mods_p, mods_s, norm_gain, wg, wu, wd, final_gain):
    nb, seq, d = xp.shape
    d_ff = wg.shape[1]
    tm = FFN_TOKEN_TILE
    ffc = FF_CHUNK
    assert seq % tm == 0 and d_ff % ffc == 0 and xs.shape[0] * xs.shape[1] <= tm
    per_seq = seq // tm
    n_stage, n_prompt = d_ff // ffc, nb * per_seq
    final_norm = final_gain is not None

    def prompt_tile(i):
        return jnp.clip(i - n_stage, 0, n_prompt - 1)

    def stage(i):
        return jnp.minimum(i, n_stage - 1)

    x_spec = pl.BlockSpec((1, tm, d), lambda i: (prompt_tile(i) // per_seq, prompt_tile(i) % per_seq, 0))
    in_specs = [
        x_spec,
        _resident(xs.shape),
        pl.BlockSpec((3, 1, 1, d), lambda i: (0, prompt_tile(i) // per_seq, 0, 0)),
        _resident(mods_s.shape),
        _resident((1, d)),
        pl.BlockSpec((d, ffc), lambda i: (0, stage(i))),
        pl.BlockSpec((d, ffc), lambda i: (0, stage(i))),
        pl.BlockSpec((ffc, d), lambda i: (stage(i), 0)),
    ]
    args = [xp, xs, mods_p, mods_s, norm_gain.reshape(1, d), wg, wu, wd]
    if final_norm:
        in_specs.append(_resident((1, d)))
        args.append(final_gain.reshape(1, d))
    return pl.pallas_call(
        functools.partial(_ffn_kernel, n_stage=n_stage, n_prompt=n_prompt, final_norm=final_norm),
        grid=(n_stage + n_prompt + 1,),
        in_specs=in_specs,
        out_specs=[x_spec, pl.BlockSpec(xs.shape, lambda i: (0, 0, 0))],
        out_shape=[jax.ShapeDtypeStruct(xp.shape, F32), jax.ShapeDtypeStruct(xs.shape, F32)],
        scratch_shapes=[
            pltpu.VMEM((n_stage, d, ffc), BF16),
            pltpu.VMEM((n_stage, d, ffc), BF16),
            pltpu.VMEM((d_ff, d), BF16),
            pltpu.VMEM((tm, d_ff), BF16),
            pltpu.VMEM((tm, d), BF16),
        ],
        compiler_params=pltpu.CompilerParams(dimension_semantics=("arbitrary",),
                                             vmem_limit_bytes=VMEM_LIMIT_BYTES),
        name="ffn_final" if final_norm else "ffn",
    )(*args)


def _rotate(xh, cos2, sin2):
    return xh * cos2 + pltpu.roll(xh, xh.shape[-1] // 2, axis=xh.ndim - 1) * sin2


def _group_norm_gate(o, gate_pre, gain):
    mu = jnp.mean(o, axis=-1, keepdims=True)
    ctr = o - mu
    var = jnp.mean(ctr * ctr, axis=-1, keepdims=True)
    return gate_pre * jax.nn.sigmoid(gate_pre) * (ctr * lax.rsqrt(var + EPS) * gain)


def _layer_norm_silu(y, gain, bias):
    mu = jnp.mean(y, axis=-1, keepdims=True)
    ctr = y - mu
    var = jnp.mean(ctr * ctr, axis=-1, keepdims=True)
    yn = ctr * lax.rsqrt(var + EPS) * gain + bias
    return yn * jax.nn.sigmoid(yn)


def _conv_taps_slab(win_ref, dww_ref, s, n_out, lead):
    acc = None
    for j in range(dww_ref.shape[0]):
        term = win_ref[s, pl.ds(lead + j, n_out), :] * dww_ref[j:j + 1, s * LANES:(s + 1) * LANES]
        acc = term if acc is None else acc + term
    return acc


def _mix_prompt_kernel(xa_ref, xb_ref, moda_ref, modb_ref, ng_ref, win_ref, gn_ref, dww_ref, dwb_ref,
                       lng_ref, lnb_ref, wout_ref, cos_ref, sin_ref, dmask_ref, dq_ref, dk_ref, gc_ref,
                       o_ref, sout_ref, cout_ref, s_scr, u_scr, hb_scr, conv_scr, mix_scr,
                       *, heads, hist, per_seq, n_tiles):
    i = pl.program_id(0)
    tt, d_ = xa_ref.shape[1], xa_ref.shape[2]
    ret_w = gn_ref.shape[1]
    dh = ret_w // heads
    n_taps, conv_w = dww_ref.shape
    k_scale = dh ** -0.5
    o_a, o_b = 4 * ret_w, 4 * ret_w + conv_w
    n_slabs = u_scr.shape[0]

    slot_a = lax.rem(i, 2)
    slot_b = 1 - slot_a

    def stage_a_project():
        x = xa_ref[0]
        shift, scale = moda_ref[0, 0], moda_ref[1, 0]
        h = _rmsnorm(x, ng_ref[...]) * (1.0 + scale) + shift
        hb = h.astype(BF16)
        hb_scr[slot_a] = hb
        ab = _dot(hb, win_ref[:, o_a:o_b + conv_w])
        u = ab[:, :conv_w] * jax.nn.sigmoid(ab[:, conv_w:])
        for s in range(n_slabs):
            u_scr[s, hist:hist + tt, :] = u[:, s * LANES:(s + 1) * LANES]

    def stage_a_conv_slab(s):
        return _conv_taps_slab(u_scr, dww_ref, s, tt, hist - (n_taps - 1))

    def stage_a_finish(slabs):
        y = jnp.concatenate(slabs, axis=-1) + dwb_ref[...]
        conv_scr[slot_a] = _layer_norm_silu(y, lng_ref[...], lnb_ref[...]).astype(BF16)
        for s in range(n_slabs):
            u_scr[s, 0:hist, :] = u_scr[s, tt:tt + hist, :]

    def stage_b_project():
        return _dot(hb_scr[slot_b], win_ref[:, 0:o_a])

    def stage_b_head(proj, hd):
        lo = hd * dh
        cos2, sin2 = cos_ref[...], sin_ref[...]
        q = _rotate(proj[:, lo:lo + dh], cos2, sin2)
        k = _rotate(proj[:, ret_w + lo:ret_w + lo + dh], cos2, sin2) * k_scale
        v = proj[:, 2 * ret_w + lo:2 * ret_w + lo + dh]
        gate_pre = proj[:, 3 * ret_w + lo:3 * ret_w + lo + dh]
        dmask, dq, dk, gc = dmask_ref[hd], dq_ref[hd], dk_ref[hd], gc_ref[hd]
        state = s_scr[hd]
        outs = []
        for c0 in range(0, tt, RET_CHUNK):
            qc, kc = q[c0:c0 + RET_CHUNK], k[c0:c0 + RET_CHUNK]
            vb = v[c0:c0 + RET_CHUNK].astype(BF16)
            scores = lax.dot_general(qc.astype(BF16), kc.astype(BF16), (((1,), (1,)), ((), ())),
                                     preferred_element_type=F32)
            inner = _dot((scores * dmask).astype(BF16), vb)
            cross = _dot((qc * dq).astype(BF16), state.astype(BF16))
            outs.append(inner + cross)
            state = gc * state + lax.dot_general((kc * dk).astype(BF16), vb, (((0,), (0,)), ((), ())),
                                                 preferred_element_type=F32)
        s_scr[hd] = state
        o = jnp.concatenate(outs, axis=0) if len(outs) > 1 else outs[0]
        mix_scr[:, lo:lo + dh] = _group_norm_gate(o, gate_pre, gn_ref[:, lo:lo + dh]).astype(BF16)

    def stage_b_finish():
        mixed = (_dot(mix_scr[...], wout_ref[0:ret_w, :])
                 + _dot(conv_scr[slot_b], wout_ref[ret_w:ret_w + conv_w, :]))
        o_ref[0] = xb_ref[0] + modb_ref[2, 0] * mixed

    def run(do_a, do_b):
        if do_a:
            stage_a_project()
        if do_b:
            proj = stage_b_project()
            for hd in range(heads):
                stage_b_head(proj, hd)
            stage_b_finish()
        if do_a:
            stage_a_finish([stage_a_conv_slab(s) for s in range(n_slabs)])

    has_a = i < n_tiles
    has_b = i >= 1
    pos_a = lax.rem(i, per_seq)
    pos_b = lax.rem(i + per_seq - 1, per_seq)

    @pl.when(jnp.logical_and(has_a, pos_a == 0))
    def _():
        u_scr[:, 0:hist, :] = jnp.zeros((n_slabs, hist, LANES), F32)

    @pl.when(jnp.logical_and(has_b, pos_b == 0))
    def _():
        s_scr[...] = jnp.zeros_like(s_scr)

    @pl.when(i == 0)
    def _():
        run(True, False)

    @pl.when(jnp.logical_and(has_a, has_b))
    def _():
        run(True, True)

    @pl.when(i == n_tiles)
    def _():
        run(False, True)

    @pl.when(jnp.logical_and(has_a, pos_a == per_seq - 1))
    def _():
        for s in range(n_slabs):
            cout_ref[0, :, s * LANES:(s + 1) * LANES] = u_scr[s, hist - (n_taps - 1):hist, :]

    @pl.when(jnp.logical_and(has_b, pos_b == per_seq - 1))
    def _():
        sout_ref[0] = s_scr[...]


def _decay_tables(heads, chunk):
    lg = np.log(1.0 - 2.0 ** (-5.0 - np.arange(heads, dtype=np.float64)))
    idx = np.arange(chunk, dtype=np.float64)
    diff = idx[:, None] - idx[None, :]
    dmask = np.where(diff[None] >= 0, np.exp(lg[:, None, None] * np.maximum(diff, 0.0)[None]), 0.0)
    dq = np.exp(lg[:, None] * (idx[None, :] + 1.0))
    dk = np.exp(lg[:, None] * (chunk - 1.0 - idx[None, :]))
    gc = np.exp(lg * chunk)
    return dmask, dq, dk, gc


def _rotary_tables(pos0, n_pos, half):
    inv = ROPE_BASE ** (-np.arange(half, dtype=np.float64) / half)
    pos = (pos0 + np.arange(n_pos)).astype(np.float64)
    ang = pos[:, None] * inv[None, :]
    cos, sin = np.cos(ang), np.sin(ang)
    return np.concatenate([cos, cos], axis=-1), np.concatenate([-sin, sin], axis=-1)


def _const(a, shape=None):
    a = np.asarray(a, dtype=np.float32)
    if shape is not None:
        a = np.ascontiguousarray(np.broadcast_to(a, shape))
    return jnp.asarray(a)


def _mix_prompt(x, mods, norm_gain, w_in, gn_gain, dw_w, dw_b, ln_g, ln_b, w_out, heads):
    nb, seq, d = x.shape
    tt = MIX_TOKEN_TILE
    ret_w = gn_gain.shape[0]
    dh = ret_w // heads
    n_taps, conv_w = dw_w.shape
    hist = -(-(n_taps - 1) // SUBLANES) * SUBLANES
    chunk = RET_CHUNK
    cos2, sin2 = (_const(t) for t in _rotary_tables(0, seq, dh // 2))
    dmask, dq, dk, gc = _decay_tables(heads, chunk)
    dmask = _const(dmask)
    dq_b = _const(dq[:, :, None], (heads, chunk, dh))
    dk_b = _const(dk[:, :, None], (heads, chunk, dh))
    gc_b = _const(gc[:, None, None], (heads, 1, dh))
    per_seq = seq // tt
    n_tiles = nb * per_seq

    def tile_a(i):
        return jnp.minimum(i, n_tiles - 1)

    def tile_b(i):
        return jnp.maximum(i - 1, 0)

    kern = functools.partial(_mix_prompt_kernel, heads=heads, hist=hist, per_seq=per_seq, n_tiles=n_tiles)
    return pl.pallas_call(
        kern,
        grid=(n_tiles + 1,),
        in_specs=[
            pl.BlockSpec((1, tt, d), lambda i: (tile_a(i) // per_seq, tile_a(i) % per_seq, 0)),
            pl.BlockSpec((1, tt, d), lambda i: (tile_b(i) // per_seq, tile_b(i) % per_seq, 0)),
            pl.BlockSpec((3, 1, 1, d), lambda i: (0, tile_a(i) // per_seq, 0, 0)),
            pl.BlockSpec((3, 1, 1, d), lambda i: (0, tile_b(i) // per_seq, 0, 0)),
            _resident((1, d)),
            _resident(w_in.shape),
            _resident((1, ret_w)),
            _resident((n_taps, conv_w)),
            _resident((1, conv_w)),
            _resident((1, conv_w)),
            _resident((1, conv_w)),
            _resident(w_out.shape),
            pl.BlockSpec((tt, dh), lambda i: (tile_b(i) % per_seq, 0)),
            pl.BlockSpec((tt, dh), lambda i: (tile_b(i) % per_seq, 0)),
            _resident((heads, chunk, chunk)),
            _resident((heads, chunk, dh)),
            _resident((heads, chunk, dh)),
            _resident((heads, 1, dh)),
        ],
        out_specs=[
            pl.BlockSpec((1, tt, d), lambda i: (tile_b(i) // per_seq, tile_b(i) % per_seq, 0)),
            pl.BlockSpec((1, heads, dh, dh), lambda i: (tile_b(i) // per_seq, 0, 0, 0)),
            pl.BlockSpec((1, n_taps - 1, conv_w), lambda i: (tile_a(i) // per_seq, 0, 0)),
        ],
        out_shape=[
            jax.ShapeDtypeStruct(x.shape, F32),
            jax.ShapeDtypeStruct((nb, heads, dh, dh), F32),
            jax.ShapeDtypeStruct((nb, n_taps - 1, conv_w), F32),
        ],
        scratch_shapes=[
            pltpu.VMEM((heads, dh, dh), F32),
            pltpu.VMEM((conv_w // LANES, hist + tt, LANES), F32),
            pltpu.VMEM((2, tt, d), BF16),
            pltpu.VMEM((2, tt, conv_w), BF16),
            pltpu.VMEM((tt, ret_w), BF16),
        ],
        compiler_params=pltpu.CompilerParams(dimension_semantics=("arbitrary",),
                                             vmem_limit_bytes=VMEM_LIMIT_BYTES),
        name="mix_prompt",
    )(x, x, mods, mods, norm_gain.reshape(1, d), w_in, gn_gain.reshape(1, -1), dw_w, dw_b.reshape(1, -1),
      ln_g.reshape(1, -1), ln_b.reshape(1, -1), w_out, cos2, sin2, dmask, dq_b, dk_b, gc_b)


def _sample_proj_kernel(x_ref, mod_ref, ng_ref, win_ref, o_ref):
    x = x_ref[...]
    n_tok, ns, d_ = x.shape
    h = _rmsnorm(x, ng_ref[...]) * (1.0 + mod_ref[1]) + mod_ref[0]
    o_ref[...] = _dot(h.astype(BF16).reshape(n_tok * ns, d_), win_ref[...]).reshape(n_tok, ns, -1)


def _sample_out_kernel(x_ref, mod_ref, mixed_ref, wout_ref, o_ref):
    n_tok, ns, d_ = x_ref.shape
    y = _dot(mixed_ref[...].reshape(n_tok * ns, -1), wout_ref[...]).reshape(n_tok, ns, d_)
    o_ref[...] = x_ref[...] + mod_ref[2] * y


def _sample_core_kernel(proj_ref, state_ref, cin_ref, cos_ref, sin_ref, dm_ref, dq_ref, dk_ref, gc_ref,
                        gn_ref, dww_ref, dwb_ref, lng_ref, lnb_ref, mixed_ref, sout_ref, cout_ref,
                        qd_scr, kd_scr, v_scr, cross_scr, *, heads):
    n_tok, tb, _ = proj_ref.shape
    ret_w = gn_ref.shape[1]
    dh = ret_w // heads
    n_taps, conv_w = dww_ref.shape
    n_buf = n_taps - 1
    o_a, o_b = 4 * ret_w, 4 * ret_w + conv_w
    k_scale = dh ** -0.5
    cos2, sin2 = cos_ref[...], sin_ref[...]

    inner_heads = []
    for hd in range(heads):
        lo = hd * dh
        q = _rotate(proj_ref[:, :, lo:lo + dh], cos2, sin2)
        k = _rotate(proj_ref[:, :, ret_w + lo:ret_w + lo + dh], cos2, sin2) * k_scale
        v = proj_ref[:, :, 2 * ret_w + lo:2 * ret_w + lo + dh]
        qd_scr[:, :, lo:lo + dh] = q * dq_ref[hd]
        kd_scr[:, :, lo:lo + dh] = k * dk_ref[hd]
        v_scr[:, :, lo:lo + dh] = v
        rows_out = []
        for i in range(n_tok):
            acc = None
            for j in range(i + 1):
                s_ij = jnp.sum(q[i] * k[j], axis=-1, keepdims=True)
                term = (s_ij * dm_ref[hd, i, j]) * v[j]
                acc = term if acc is None else acc + term
            rows_out.append(acc)
        inner_heads.append(jnp.stack(rows_out, axis=0))

    row = lax.broadcasted_iota(jnp.int32, (n_tok, SUBLANES, dh), 1)

    def group(g, carry):
        rows = pl.ds(pl.multiple_of(g * SUBLANES, SUBLANES), SUBLANES)
        for hd in range(heads):
            lo = hd * dh
            q_tile = qd_scr[:, rows, lo:lo + dh].reshape(n_tok * SUBLANES, dh).astype(BF16)
            k_tile = kd_scr[:, rows, lo:lo + dh].reshape(n_tok * SUBLANES, dh).astype(BF16)
            v_group = v_scr[:, rows, lo:lo + dh]
            cross = jnp.zeros((n_tok, SUBLANES, dh), F32)
            for r in range(SUBLANES):
                b = g * SUBLANES + r
                state = state_ref[b, hd]
                out = _dot(q_tile, state.astype(BF16)).reshape(n_tok, SUBLANES, dh)
                cross = jnp.where(row == r, out, cross)
                v_own = jnp.where(row == r, v_group, 0.0).reshape(n_tok * SUBLANES, dh).astype(BF16)
                upd = lax.dot_general(k_tile, v_own, (((0,), (0,)), ((), ())), preferred_element_type=F32)
                sout_ref[b, hd] = gc_ref[hd] * state + upd
            cross_scr[:, rows, lo:lo + dh] = cross
        return carry

    lax.fori_loop(0, tb // SUBLANES, group, 0)

    for hd in range(heads):
        lo = hd * dh
        o = inner_heads[hd] + cross_scr[:, :, lo:lo + dh]
        gate_pre = proj_ref[:, :, 3 * ret_w + lo:3 * ret_w + lo + dh]
        mixed_ref[:, :, lo:lo + dh] = _group_norm_gate(o, gate_pre, gn_ref[:, lo:lo + dh]).astype(BF16)

    u = proj_ref[:, :, o_a:o_a + conv_w] * jax.nn.sigmoid(proj_ref[:, :, o_b:o_b + conv_w])

    def window(s):
        return cin_ref[s] if s < n_buf else u[s - n_buf]

    for t in range(n_tok):
        acc = None
        for j in range(n_taps):
            term = window(t + j) * dww_ref[j:j + 1, :]
            acc = term if acc is None else acc + term
        y = acc + dwb_ref[...]
        mixed_ref[t, :, ret_w:ret_w + conv_w] = _layer_norm_silu(y, lng_ref[...], lnb_ref[...]).astype(BF16)
    for s in range(n_buf):
        cout_ref[s] = window(s + n_tok)


def _whole(shape):
    n = len(shape)
    return pl.BlockSpec(shape, lambda *_: (0,) * n)


def _mix_sample(x, mods, norm_gain, w_in, gn_gain, dw_w, dw_b, ln_g, ln_b, w_out, state_ret, conv_tm):
    n_tok, ns, d = x.shape
    tb = SAMPLE_MIX_SEQ_TILE
    heads, dh = state_ret.shape[1], state_ret.shape[2]
    ret_w = heads * dh
    n_taps, conv_w = dw_w.shape
    n_buf = n_taps - 1
    n_cols = w_in.shape[1]
    params = pltpu.CompilerParams(vmem_limit_bytes=VMEM_LIMIT_BYTES)

    proj = pl.pallas_call(
        _sample_proj_kernel,
        in_specs=[_whole(x.shape), _whole(mods.shape), _whole((1, d)), _whole(w_in.shape)],
        out_specs=_whole((n_tok, ns, n_cols)),
        out_shape=jax.ShapeDtypeStruct((n_tok, ns, n_cols), F32),
        grid=(1,),
        compiler_params=params,
        name="sample_proj",
    )(x, mods, norm_gain.reshape(1, d), w_in)

    cos2, sin2 = (_const(t) for t in _rotary_tables(PAST_LEN, n_tok, dh // 2))
    dmask, dq, dk, gc = _decay_tables(heads, n_tok)
    dm_b = _const(dmask[:, :, :, None, None], (heads, n_tok, n_tok, 1, dh))
    dq_b = _const(dq[:, :, None, None], (heads, n_tok, 1, dh))
    dk_b = _const(dk[:, :, None, None], (heads, n_tok, 1, dh))
    gc_b = _const(gc[:, None, None], (heads, 1, dh))
    mixed, state_new, conv_new = pl.pallas_call(
        functools.partial(_sample_core_kernel, heads=heads),
        grid=(ns // tb,),
        in_specs=[
            pl.BlockSpec((n_tok, tb, n_cols), lambda i: (0, i, 0)),
            pl.BlockSpec((tb, heads, dh, dh), lambda i: (i, 0, 0, 0)),
            pl.BlockSpec((n_buf, tb, conv_w), lambda i: (0, i, 0)),
            _resident((n_tok, 1, dh)),
            _resident((n_tok, 1, dh)),
            _resident((heads, n_tok, n_tok, 1, dh)),
            _resident((heads, n_tok, 1, dh)),
            _resident((heads, n_tok, 1, dh)),
            _resident((heads, 1, dh)),
            _resident((1, ret_w)),
            _resident((n_taps, conv_w)),
            _resident((1, conv_w)),
            _resident((1, conv_w)),
            _resident((1, conv_w)),
        ],
        out_specs=[
            pl.BlockSpec((n_tok, tb, ret_w + conv_w), lambda i: (0, i, 0)),
            pl.BlockSpec((tb, heads, dh, dh), lambda i: (i, 0, 0, 0)),
            pl.BlockSpec((n_buf, tb, conv_w), lambda i: (0, i, 0)),
        ],
        out_shape=[
            jax.ShapeDtypeStruct((n_tok, ns, ret_w + conv_w), BF16),
            jax.ShapeDtypeStruct(state_ret.shape, F32),
            jax.ShapeDtypeStruct(conv_tm.shape, F32),
        ],
        scratch_shapes=[pltpu.VMEM((n_tok, tb, ret_w), F32)] * 4,
        compiler_params=pltpu.CompilerParams(dimension_semantics=("arbitrary",),
                                             vmem_limit_bytes=VMEM_LIMIT_BYTES),
        name="sample_core",
    )(proj, state_ret, conv_tm, cos2.reshape(n_tok, 1, dh), sin2.reshape(n_tok, 1, dh), dm_b, dq_b, dk_b,
      gc_b, gn_gain.reshape(1, -1), dw_w, dw_b.reshape(1, -1), ln_g.reshape(1, -1), ln_b.reshape(1, -1))

    x_new = pl.pallas_call(
        _sample_out_kernel,
        in_specs=[_whole(x.shape), _whole(mods.shape), _whole(mixed.shape), _whole(w_out.shape)],
        out_specs=_whole(x.shape),
        out_shape=jax.ShapeDtypeStruct(x.shape, F32),
        grid=(1,),
        compiler_params=params,
        name="sample_out",
    )(x, mods, mixed, w_out)
    return x_new, state_new, conv_new


def _layer(xp, xs, mods_p, mods_s, sret, sconv, lw, final_gain, n_tok):
    (norm_ffn1, w1g, w1u, w1d, norm_mix, w_in, gn_gain, dw_w, dw_b, ln_g, ln_b, w_out,
     norm_ffn2, w2g, w2u, w2d) = lw
    nb, seq, d = xp.shape
    n_tok, ns, _ = xs.shape
    heads = sret.shape[1]
    mods_p = mods_p.reshape(N_MOD, nb, 1, d)
    mods_s_tm = mods_s.reshape(N_MOD, 1, ns, d)

    xp, xs = _ffn(xp, xs, mods_p[0:3], mods_s_tm[0:3], norm_ffn1, w1g, w1u, w1d, None)
    xp, ret_p, conv_p = _mix_prompt(xp, mods_p[3:6], norm_mix, w_in, gn_gain, dw_w, dw_b, ln_g, ln_b,
                                    w_out, heads)
    xs, ret_s, conv_s_tm = _mix_sample(xs, mods_s_tm[3:6], norm_mix, w_in, gn_gain, dw_w, dw_b, ln_g, ln_b,
                                       w_out, sret, sconv.transpose(1, 0, 2))
    xp, xs = _ffn(xp, xs, mods_p[6:9], mods_s_tm[6:9], norm_ffn2, w2g, w2u, w2d, final_gain)
    return xp, xs, ret_p, conv_p, ret_s, conv_s_tm.transpose(1, 0, 2)


def kernel(x_prompt, x_sample, c_prompt, c_sample, state_ret, state_conv, norm_ffn1, ffn1_w_gate,
           ffn1_w_up, ffn1_w_down, norm_mix, w_in, ret_gn_gain, dw_w, dw_b, conv_ln_gain, conv_ln_bias,
           w_out, norm_ffn2, ffn2_w_gate, ffn2_w_up, ffn2_w_down, w_ada, b_ada, norm_final):
    depth = w_in.shape[0]
    nb = x_prompt.shape[0]
    ns, n_tok, d = x_sample.shape
    assert n_tok <= SUBLANES and x_prompt.shape[1] % RET_CHUNK == 0

    xp = x_prompt
    xs = x_sample.transpose(1, 0, 2)
    c_all = jnp.concatenate([c_prompt, c_sample], axis=0)

    ret_p, conv_p, ret_s, conv_s = [], [], [], []
    for l in range(depth):
        ada = _ada(c_all, w_ada[l], b_ada[l])
        lw = (norm_ffn1[l], ffn1_w_gate[l], ffn1_w_up[l], ffn1_w_down[l], norm_mix[l],
              w_in[l].astype(BF16), ret_gn_gain[l], dw_w[l], dw_b[l], conv_ln_gain[l], conv_ln_bias[l],
              w_out[l].astype(BF16), norm_ffn2[l], ffn2_w_gate[l], ffn2_w_up[l], ffn2_w_down[l])
        final_gain = norm_final if l == depth - 1 else None
        xp, xs, rp, cp, rs, cs = _layer(xp, xs, ada[:, :nb], ada[:, nb:], state_ret[l], state_conv[l], lw,
                                        final_gain, n_tok)
        ret_p.append(rp)
        conv_p.append(cp)
        ret_s.append(rs)
        conv_s.append(cs)

    return (xp, xs.transpose(1, 0, 2), jnp.stack(ret_p), jnp.stack(conv_p), jnp.stack(ret_s),
            jnp.stack(conv_s))
```

```python
import functools

import jax
import jax.numpy as jnp
import numpy as np
from jax import lax
from jax.experimental import pallas as pl
from jax.experimental.pallas import tpu as pltpu

F32 = jnp.float32
BF16 = jnp.bfloat16

PAST_LEN = 16384
RET_CHUNK = 256
ROPE_BASE = 10000.0
EPS = 1e-6
N_MOD = 9

SUBLANES = 8
LANES = 128
VMEM_LIMIT_BYTES = 56 * 1024 * 1024

FFN_TOKEN_TILE = 512
MIX_TOKEN_TILE = 512
SAMPLE_MIX_SEQ_TILE = 32
SAMPLE_PROJ_COLS = 512
ADA_ROWS_PER_STEP = 128
FF_CHUNK = 256


def _resident(shape):
    n = len(shape)
    return pl.BlockSpec(shape, lambda *_: (0,) * n, pipeline_mode=pl.Buffered(1))


def _rmsnorm(x, gain):
    ms = jnp.mean(x * x, axis=-1, keepdims=True)
    return x * lax.rsqrt(ms + EPS) * gain


def _dot(a, b):
    return jnp.dot(a, b, preferred_element_type=F32)


def _ada_kernel(c_ref, w_ref, b_ref, op_ref, os_ref):
    k = pl.program_id(0)
    n_prompt = op_ref.shape[1]
    d = op_ref.shape[2]
    c = c_ref[...]
    h = (c * jax.nn.sigmoid(c)).astype(BF16)

    @pl.when(k == 0)
    def _():
        for m in range(op_ref.shape[0]):
            bias = b_ref[:, m * d:(m + 1) * d]
            op_ref[m] = jnp.broadcast_to(bias, op_ref.shape[1:])
            os_ref[m] = jnp.broadcast_to(bias, os_ref.shape[1:])

    for m in range(op_ref.shape[0]):
        part = _dot(h, w_ref[:, m * d:(m + 1) * d].astype(BF16))
        op_ref[m] += part[0:n_prompt]
        os_ref[m] += part[n_prompt:]


def _ada(c_all, w_ada, b_ada, n_prompt):
    nb, d = c_all.shape
    tk = ADA_ROWS_PER_STEP
    n_sample = nb - n_prompt
    return pl.pallas_call(
        _ada_kernel,
        grid=(d // tk,),
        in_specs=[
            pl.BlockSpec((nb, tk), lambda k: (0, k)),
            pl.BlockSpec((tk, N_MOD * d), lambda k: (k, 0)),
            _resident((1, N_MOD * d)),
        ],
        out_specs=[pl.BlockSpec((N_MOD, n_prompt, d), lambda k: (0, 0, 0)),
                   pl.BlockSpec((N_MOD, n_sample, d), lambda k: (0, 0, 0))],
        out_shape=[jax.ShapeDtypeStruct((N_MOD, n_prompt, d), F32),
                   jax.ShapeDtypeStruct((N_MOD, n_sample, d), F32)],
        compiler_params=pltpu.CompilerParams(dimension_semantics=("arbitrary",),
                                             vmem_limit_bytes=VMEM_LIMIT_BYTES),
        name="ada",
    )(c_all, w_ada, b_ada.reshape(1, -1))


def _ffn_kernel(xp_ref, xs_ref, modp_ref, mods_ref, ng_ref, wg_ref, wu_ref, wd_ref, *rest,
                n_stage, n_prompt, final_norm):
    if final_norm:
        fg_ref, op_ref, os_ref, wg_scr, wu_scr, wd_scr, act_scr, hb_scr = rest
    else:
        op_ref, os_ref, wg_scr, wu_scr, wd_scr, act_scr, hb_scr = rest
    i = pl.program_id(0)
    n_chunks, _, ffc = wg_scr.shape

    @pl.when(i < n_stage)
    def _():
        wg_scr[i] = wg_ref[...].astype(BF16)
        wu_scr[i] = wu_ref[...].astype(BF16)
        wd_scr[pl.ds(pl.multiple_of(i * ffc, ffc), ffc), :] = wd_ref[...].astype(BF16)

    def tile(x_ref, mod_ref, o_ref):
        x = x_ref[...]
        g_, r_, d_ = x.shape
        shift, scale, gate = mod_ref[0], mod_ref[1], mod_ref[2]
        h = _rmsnorm(x, ng_ref[...]) * (1.0 + scale) + shift
        rows = g_ * r_
        hb_scr[0:rows, :] = h.astype(BF16).reshape(rows, d_)
        for c in range(n_chunks):
            g = _dot(hb_scr[0:rows, :], wg_scr[c])
            u = _dot(hb_scr[0:rows, :], wu_scr[c])
            act_scr[0:rows, c * ffc:(c + 1) * ffc] = (g * jax.nn.sigmoid(g) * u).astype(BF16)
        y = _dot(act_scr[0:rows, :], wd_scr[...]).reshape(g_, r_, d_)
        out = x + 0.5 * gate * y
        if final_norm:
            out = _rmsnorm(out, fg_ref[...])
        o_ref[...] = out

    @pl.when(jnp.logical_and(i >= n_stage, i < n_stage + n_prompt))
    def _():
        tile(xp_ref, modp_ref, op_ref)

    @pl.when(i == n_stage + n_prompt)
    def _():
        tile(xs_ref, mods_ref, os_ref)


def _ffn(xp, xs, mods_p, mods_s, norm_gain, wg, wu, wd, final_gain):
    nb, seq, d = xp.shape
    d_ff = wg.shape[1]
    tm = FFN_TOKEN_TILE
    ffc = FF_CHUNK
    assert seq % tm == 0 and d_ff % ffc == 0 and xs.shape[0] * xs.shape[1] <= tm
    per_seq = seq // tm
    n_stage, n_prompt = d_ff // ffc, nb * per_seq
    final_norm = final_gain is not None

    def prompt_tile(i):
        return jnp.clip(i - n_stage, 0, n_prompt - 1)

    def stage(i):
        return jnp.minimum(i, n_stage - 1)

    x_spec = pl.BlockSpec((1, tm, d), lambda i: (prompt_tile(i) // per_seq, prompt_tile(i) % per_seq, 0))
    in_specs = [
        x_spec,
        _resident(xs.shape),
        pl.BlockSpec((3, 1, 1, d), lambda i: (0, prompt_tile(i) // per_seq, 0, 0)),
        _resident(mods_s.shape),
        _resident((1, d)),
        pl.BlockSpec((d, ffc), lambda i: (0, stage(i))),
        pl.BlockSpec((d, ffc), lambda i: (0, stage(i))),
        pl.BlockSpec((ffc, d), lambda i: (stage(i), 0)),
    ]
    args = [xp, xs, mods_p, mods_s, norm_gain.reshape(1, d), wg, wu, wd]
    if final_norm:
        in_specs.append(_resident((1, d)))
        args.append(final_gain.reshape(1, d))
    return pl.pallas_call(
        functools.partial(_ffn_kernel, n_stage=n_stage, n_prompt=n_prompt, final_norm=final_norm),
        grid=(n_stage + n_prompt + 1,),
        in_specs=in_specs,
        out_specs=[x_spec, pl.BlockSpec(xs.shape, lambda i: (0, 0, 0))],
        out_shape=[jax.ShapeDtypeStruct(xp.shape, F32), jax.ShapeDtypeStruct(xs.shape, F32)],
        scratch_shapes=[
            pltpu.VMEM((n_stage, d, ffc), BF16),
            pltpu.VMEM((n_stage, d, ffc), BF16),
            pltpu.VMEM((d_ff, d), BF16),
            pltpu.VMEM((tm, d_ff), BF16),
            pltpu.VMEM((tm, d), BF16),
        ],
        compiler_params=pltpu.CompilerParams(dimension_semantics=("arbitrary",),
                                             vmem_limit_bytes=VMEM_LIMIT_BYTES),
        name="ffn_final" if final_norm else "ffn",
    )(*args)


def _rotate(xh, cos2, sin2):
    return xh * cos2 + pltpu.roll(xh, xh.shape[-1] // 2, axis=xh.ndim - 1) * sin2


def _group_norm_gate(o, gate_pre, gain):
    mu = jnp.mean(o, axis=-1, keepdims=True)
    ctr = o - mu
    var = jnp.mean(ctr * ctr, axis=-1, keepdims=True)
    return gate_pre * jax.nn.sigmoid(gate_pre) * (ctr * lax.rsqrt(var + EPS) * gain)


def _layer_norm_silu(y, gain, bias):
    mu = jnp.mean(y, axis=-1, keepdims=True)
    ctr = y - mu
    var = jnp.mean(ctr * ctr, axis=-1, keepdims=True)
    yn = ctr * lax.rsqrt(var + EPS) * gain + bias
    return yn * jax.nn.sigmoid(yn)


def _conv_taps_slab(win_ref, dww_ref, s, n_out, lead):
    acc = None
    for j in range(dww_ref.shape[0]):
        term = win_ref[s, pl.ds(lead + j, n_out), :] * dww_ref[j:j + 1, s * LANES:(s + 1) * LANES]
        acc = term if acc is None else acc + term
    return acc


def _mix_prompt_kernel(xa_ref, xb_ref, moda_ref, modb_ref, ng_ref, win_ref, gn_ref, dww_ref, dwb_ref,
                       lng_ref, lnb_ref, wout_ref, cos_ref, sin_ref, dmask_ref, dq_ref, dk_ref, gc_ref,
                       o_ref, sout_ref, cout_ref, s_scr, u_scr, hb_scr, conv_scr, mix_scr,
                       *, heads, hist, per_seq, n_tiles):
    i = pl.program_id(0)
    tt, d_ = xa_ref.shape[1], xa_ref.shape[2]
    ret_w = gn_ref.shape[1]
    dh = ret_w // heads
    n_taps, conv_w = dww_ref.shape
    k_scale = dh ** -0.5
    o_a, o_b = 4 * ret_w, 4 * ret_w + conv_w
    n_slabs = u_scr.shape[0]

    slot_a = lax.rem(i, 2)
    slot_b = 1 - slot_a

    def stage_a_project():
        x = xa_ref[0]
        shift, scale = moda_ref[0, 0], moda_ref[1, 0]
        h = _rmsnorm(x, ng_ref[...]) * (1.0 + scale) + shift
        hb = h.astype(BF16)
        hb_scr[slot_a] = hb
        ab = _dot(hb, win_ref[:, o_a:o_b + conv_w])
        u = ab[:, :conv_w] * jax.nn.sigmoid(ab[:, conv_w:])
        for s in range(n_slabs):
            u_scr[s, hist:hist + tt, :] = u[:, s * LANES:(s + 1) * LANES]

    def stage_a_conv_slab(s):
        return _conv_taps_slab(u_scr, dww_ref, s, tt, hist - (n_taps - 1))

    def stage_a_finish(slabs):
        y = jnp.concatenate(slabs, axis=-1) + dwb_ref[...]
        conv_scr[slot_a] = _layer_norm_silu(y, lng_ref[...], lnb_ref[...]).astype(BF16)
        for s in range(n_slabs):
            u_scr[s, 0:hist, :] = u_scr[s, tt:tt + hist, :]

    def stage_b_project():
        return _dot(hb_scr[slot_b], win_ref[:, 0:o_a])

    def stage_b_head(proj, hd):
        lo = hd * dh
        cos2, sin2 = cos_ref[...], sin_ref[...]
        q = _rotate(proj[:, lo:lo + dh], cos2, sin2)
        k = _rotate(proj[:, ret_w + lo:ret_w + lo + dh], cos2, sin2) * k_scale
        v = proj[:, 2 * ret_w + lo:2 * ret_w + lo + dh]
        gate_pre = proj[:, 3 * ret_w + lo:3 * ret_w + lo + dh]
        dmask, dq, dk, gc = dmask_ref[hd], dq_ref[hd], dk_ref[hd], gc_ref[hd]
        state = s_scr[hd]
        outs = []
        for c0 in range(0, tt, RET_CHUNK):
            qc, kc = q[c0:c0 + RET_CHUNK], k[c0:c0 + RET_CHUNK]
            vb = v[c0:c0 + RET_CHUNK].astype(BF16)
            scores = lax.dot_general(qc.astype(BF16), kc.astype(BF16), (((1,), (1,)), ((), ())),
                                     preferred_element_type=F32)
            inner = _dot((scores * dmask).astype(BF16), vb)
            cross = _dot((qc * dq).astype(BF16), state.astype(BF16))
            outs.append(inner + cross)
            state = gc * state + lax.dot_general((kc * dk).astype(BF16), vb, (((0,), (0,)), ((), ())),
                                                 preferred_element_type=F32)
        s_scr[hd] = state
        o = jnp.concatenate(outs, axis=0) if len(outs) > 1 else outs[0]
        mix_scr[:, lo:lo + dh] = _group_norm_gate(o, gate_pre, gn_ref[:, lo:lo + dh]).astype(BF16)

    def stage_b_finish():
        mixed = (_dot(mix_scr[...], wout_ref[0:ret_w, :])
                 + _dot(conv_scr[slot_b], wout_ref[ret_w:ret_w + conv_w, :]))
        o_ref[0] = xb_ref[0] + modb_ref[2, 0] * mixed

    def run(do_a, do_b):
        if do_a:
            stage_a_project()
        if do_b:
            proj = stage_b_project()
            for hd in range(heads):
                stage_b_head(proj, hd)
            stage_b_finish()
        if do_a:
            stage_a_finish([stage_a_conv_slab(s) for s in range(n_slabs)])

    has_a = i < n_tiles
    has_b = i >= 1
    pos_a = lax.rem(i, per_seq)
    pos_b = lax.rem(i + per_seq - 1, per_seq)

    @pl.when(jnp.logical_and(has_a, pos_a == 0))
    def _():
        u_scr[:, 0:hist, :] = jnp.zeros((n_slabs, hist, LANES), F32)

    @pl.when(jnp.logical_and(has_b, pos_b == 0))
    def _():
        s_scr[...] = jnp.zeros_like(s_scr)

    @pl.when(i == 0)
    def _():
        run(True, False)

    @pl.when(jnp.logical_and(has_a, has_b))
    def _():
        run(True, True)

    @pl.when(i == n_tiles)
    def _():
        run(False, True)

    @pl.when(jnp.logical_and(has_a, pos_a == per_seq - 1))
    def _():
        for s in range(n_slabs):
            cout_ref[0, :, s * LANES:(s + 1) * LANES] = u_scr[s, hist - (n_taps - 1):hist, :]

    @pl.when(jnp.logical_and(has_b, pos_b == per_seq - 1))
    def _():
        sout_ref[0] = s_scr[...]


def _decay_tables(heads, chunk):
    lg = np.log(1.0 - 2.0 ** (-5.0 - np.arange(heads, dtype=np.float64)))
    idx = np.arange(chunk, dtype=np.float64)
    diff = idx[:, None] - idx[None, :]
    dmask = np.where(diff[None] >= 0, np.exp(lg[:, None, None] * np.maximum(diff, 0.0)[None]), 0.0)
    dq = np.exp(lg[:, None] * (idx[None, :] + 1.0))
    dk = np.exp(lg[:, None] * (chunk - 1.0 - idx[None, :]))
    gc = np.exp(lg * chunk)
    return dmask, dq, dk, gc


def _rotary_tables(pos0, n_pos, half):
    inv = ROPE_BASE ** (-np.arange(half, dtype=np.float64) / half)
    pos = (pos0 + np.arange(n_pos)).astype(np.float64)
    ang = pos[:, None] * inv[None, :]
    cos, sin = np.cos(ang), np.sin(ang)
    return np.concatenate([cos, cos], axis=-1), np.concatenate([-sin, sin], axis=-1)


def _const(a, shape=None):
    a = np.asarray(a, dtype=np.float32)
    if shape is not None:
        a = np.ascontiguousarray(np.broadcast_to(a, shape))
    return jnp.asarray(a)


def _mix_prompt(x, mods, norm_gain, w_in, gn_gain, dw_w, dw_b, ln_g, ln_b, w_out, heads):
    nb, seq, d = x.shape
    tt = MIX_TOKEN_TILE
    ret_w = gn_gain.shape[0]
    dh = ret_w // heads
    n_taps, conv_w = dw_w.shape
    hist = -(-(n_taps - 1) // SUBLANES) * SUBLANES
    chunk = RET_CHUNK
    cos2, sin2 = (_const(t) for t in _rotary_tables(0, seq, dh // 2))
    dmask, dq, dk, gc = _decay_tables(heads, chunk)
    dmask = _const(dmask)
    dq_b = _const(dq[:, :, None], (heads, chunk, dh))
    dk_b = _const(dk[:, :, None], (heads, chunk, dh))
    gc_b = _const(gc[:, None, None], (heads, 1, dh))
    per_seq = seq // tt
    n_tiles = nb * per_seq

    def tile_a(i):
        return jnp.minimum(i, n_tiles - 1)

    def tile_b(i):
        return jnp.maximum(i - 1, 0)

    kern = functools.partial(_mix_prompt_kernel, heads=heads, hist=hist, per_seq=per_seq, n_tiles=n_tiles)
    return pl.pallas_call(
        kern,
        grid=(n_tiles + 1,),
        in_specs=[
            pl.BlockSpec((1, tt, d), lambda i: (tile_a(i) // per_seq, tile_a(i) % per_seq, 0)),
            pl.BlockSpec((1, tt, d), lambda i: (tile_b(i) // per_seq, tile_b(i) % per_seq, 0)),
            pl.BlockSpec((3, 1, 1, d), lambda i: (0, tile_a(i) // per_seq, 0, 0)),
            pl.BlockSpec((3, 1, 1, d), lambda i: (0, tile_b(i) // per_seq, 0, 0)),
            _resident((1, d)),
            _resident(w_in.shape),
            _resident((1, ret_w)),
            _resident((n_taps, conv_w)),
            _resident((1, conv_w)),
            _resident((1, conv_w)),
            _resident((1, conv_w)),
            _resident(w_out.shape),
            pl.BlockSpec((tt, dh), lambda i: (tile_b(i) % per_seq, 0)),
            pl.BlockSpec((tt, dh), lambda i: (tile_b(i) % per_seq, 0)),
            _resident((heads, chunk, chunk)),
            _resident((heads, chunk, dh)),
            _resident((heads, chunk, dh)),
            _resident((heads, 1, dh)),
        ],
        out_specs=[
            pl.BlockSpec((1, tt, d), lambda i: (tile_b(i) // per_seq, tile_b(i) % per_seq, 0)),
            pl.BlockSpec((1, heads, dh, dh), lambda i: (tile_b(i) // per_seq, 0, 0, 0)),
            pl.BlockSpec((1, n_taps - 1, conv_w), lambda i: (tile_a(i) // per_seq, 0, 0)),
        ],
        out_shape=[
            jax.ShapeDtypeStruct(x.shape, F32),
            jax.ShapeDtypeStruct((nb, heads, dh, dh), F32),
            jax.ShapeDtypeStruct((nb, n_taps - 1, conv_w), F32),
        ],
        scratch_shapes=[
            pltpu.VMEM((heads, dh, dh), F32),
            pltpu.VMEM((conv_w // LANES, hist + tt, LANES), F32),
            pltpu.VMEM((2, tt, d), BF16),
            pltpu.VMEM((2, tt, conv_w), BF16),
            pltpu.VMEM((tt, ret_w), BF16),
        ],
        compiler_params=pltpu.CompilerParams(dimension_semantics=("arbitrary",),
                                             vmem_limit_bytes=VMEM_LIMIT_BYTES),
        name="mix_prompt",
    )(x, x, mods, mods, norm_gain.reshape(1, d), w_in, gn_gain.reshape(1, -1), dw_w, dw_b.reshape(1, -1),
      ln_g.reshape(1, -1), ln_b.reshape(1, -1), w_out, cos2, sin2, dmask, dq_b, dk_b, gc_b)


def _sample_proj_kernel(x_ref, mod_ref, ng_ref, win_ref, o_ref, wbf_ref, hb_scr):
    n_tok, ns, d_ = x_ref.shape

    @pl.when(pl.program_id(0) == 0)
    def _():
        h = _rmsnorm(x_ref[...], ng_ref[...]) * (1.0 + mod_ref[1]) + mod_ref[0]
        hb_scr[...] = h.astype(BF16).reshape(n_tok * ns, d_)

    w = win_ref[...].astype(BF16)
    wbf_ref[...] = w
    o_ref[...] = _dot(hb_scr[...], w).reshape(n_tok, ns, -1)


def _sample_out_kernel(x_ref, mod_ref, mixed_ref, wout_ref, o_ref, wbf_ref):
    n_tok, ns, _ = x_ref.shape
    w = wout_ref[...].astype(BF16)
    wbf_ref[...] = w
    y = _dot(mixed_ref[...].reshape(n_tok * ns, -1), w).reshape(n_tok, ns, -1)
    o_ref[...] = x_ref[...] + mod_ref[2] * y


def _sample_core_kernel(proj_ref, state_ref, cin_ref, cos_ref, sin_ref, dm_ref, dq_ref, dk_ref, gc_ref,
                        gn_ref, dww_ref, dwb_ref, lng_ref, lnb_ref, mixed_ref, sout_ref, cout_ref,
                        qd_scr, kd_scr, v_scr, cross_scr, *, heads):
    n_tok, tb, _ = proj_ref.shape
    ret_w = gn_ref.shape[1]
    dh = ret_w // heads
    n_taps, conv_w = dww_ref.shape
    n_buf = n_taps - 1
    o_a, o_b = 4 * ret_w, 4 * ret_w + conv_w
    k_scale = dh ** -0.5
    cos2, sin2 = cos_ref[...], sin_ref[...]

    inner_heads = []
    for hd in range(heads):
        lo = hd * dh
        q = _rotate(proj_ref[:, :, lo:lo + dh], cos2, sin2)
        k = _rotate(proj_ref[:, :, ret_w + lo:ret_w + lo + dh], cos2, sin2) * k_scale
        v = proj_ref[:, :, 2 * ret_w + lo:2 * ret_w + lo + dh]
        qd_scr[:, :, lo:lo + dh] = q * dq_ref[hd]
        kd_scr[:, :, lo:lo + dh] = k * dk_ref[hd]
        v_scr[:, :, lo:lo + dh] = v
        rows_out = []
        for i in range(n_tok):
            acc = None
            for j in range(i + 1):
                s_ij = jnp.sum(q[i] * k[j], axis=-1, keepdims=True)
                term = (s_ij * dm_ref[hd, i, j]) * v[j]
                acc = term if acc is None else acc + term
            rows_out.append(acc)
        inner_heads.append(jnp.stack(rows_out, axis=0))

    row = lax.broadcasted_iota(jnp.int32, (n_tok, SUBLANES, dh), 1)

    def group(g, carry):
        rows = pl.ds(pl.multiple_of(g * SUBLANES, SUBLANES), SUBLANES)
        for hd in range(heads):
            lo = hd * dh
            q_tile = qd_scr[:, rows, lo:lo + dh].reshape(n_tok * SUBLANES, dh).astype(BF16)
            k_tile = kd_scr[:, rows, lo:lo + dh].reshape(n_tok * SUBLANES, dh).astype(BF16)
            v_group = v_scr[:, rows, lo:lo + dh]
            cross = jnp.zeros((n_tok, SUBLANES, dh), F32)
            for r in range(SUBLANES):
                b = g * SUBLANES + r
                state = state_ref[b, hd]
                out = _dot(q_tile, state.astype(BF16)).reshape(n_tok, SUBLANES, dh)
                cross = jnp.where(row == r, out, cross)
                v_own = jnp.where(row == r, v_group, 0.0).reshape(n_tok * SUBLANES, dh).astype(BF16)
                upd = lax.dot_general(k_tile, v_own, (((0,), (0,)), ((), ())), preferred_element_type=F32)
                sout_ref[b, hd] = gc_ref[hd] * state + upd
            cross_scr[:, rows, lo:lo + dh] = cross
        return carry

    lax.fori_loop(0, tb // SUBLANES, group, 0)

    for hd in range(heads):
        lo = hd * dh
        o = inner_heads[hd] + cross_scr[:, :, lo:lo + dh]
        gate_pre = proj_ref[:, :, 3 * ret_w + lo:3 * ret_w + lo + dh]
        mixed_ref[:, :, lo:lo + dh] = _group_norm_gate(o, gate_pre, gn_ref[:, lo:lo + dh]).astype(BF16)

    u = proj_ref[:, :, o_a:o_a + conv_w] * jax.nn.sigmoid(proj_ref[:, :, o_b:o_b + conv_w])

    def window(s):
        return cin_ref[s] if s < n_buf else u[s - n_buf]

    for t in range(n_tok):
        acc = None
        for j in range(n_taps):
            term = window(t + j) * dww_ref[j:j + 1, :]
            acc = term if acc is None else acc + term
        y = acc + dwb_ref[...]
        mixed_ref[t, :, ret_w:ret_w + conv_w] = _layer_norm_silu(y, lng_ref[...], lnb_ref[...]).astype(BF16)
    for s in range(n_buf):
        cout_ref[s] = window(s + n_tok)


def _mix_sample(x, mods, norm_gain, w_in, gn_gain, dw_w, dw_b, ln_g, ln_b, w_out, state_ret, conv_tm):
    n_tok, ns, d = x.shape
    tb = SAMPLE_MIX_SEQ_TILE
    heads, dh = state_ret.shape[1], state_ret.shape[2]
    ret_w = heads * dh
    n_taps, conv_w = dw_w.shape
    n_buf = n_taps - 1
    n_cols = w_in.shape[1]
    pc = SAMPLE_PROJ_COLS
    params = pltpu.CompilerParams(dimension_semantics=("arbitrary",), vmem_limit_bytes=VMEM_LIMIT_BYTES)

    proj, w_in_bf = pl.pallas_call(
        _sample_proj_kernel,
        grid=(n_cols // pc,),
        in_specs=[_resident(x.shape), _resident(mods.shape), _resident((1, d)),
                  pl.BlockSpec((d, pc), lambda j: (0, j))],
        out_specs=[pl.BlockSpec((n_tok, ns, pc), lambda j: (0, 0, j)),
                   pl.BlockSpec((d, pc), lambda j: (0, j))],
        out_shape=[jax.ShapeDtypeStruct((n_tok, ns, n_cols), F32),
                   jax.ShapeDtypeStruct(w_in.shape, BF16)],
        scratch_shapes=[pltpu.VMEM((n_tok * ns, d), BF16)],
        compiler_params=params,
        name="sample_proj",
    )(x, mods, norm_gain.reshape(1, d), w_in)

    cos2, sin2 = (_const(t) for t in _rotary_tables(PAST_LEN, n_tok, dh // 2))
    dmask, dq, dk, gc = _decay_tables(heads, n_tok)
    dm_b = _const(dmask[:, :, :, None, None], (heads, n_tok, n_tok, 1, dh))
    dq_b = _const(dq[:, :, None, None], (heads, n_tok, 1, dh))
    dk_b = _const(dk[:, :, None, None], (heads, n_tok, 1, dh))
    gc_b = _const(gc[:, None, None], (heads, 1, dh))
    mixed, state_new, conv_new = pl.pallas_call(
        functools.partial(_sample_core_kernel, heads=heads),
        grid=(ns // tb,),
        in_specs=[
            pl.BlockSpec((n_tok, tb, n_cols), lambda i: (0, i, 0)),
            pl.BlockSpec((tb, heads, dh, dh), lambda i: (i, 0, 0, 0)),
            pl.BlockSpec((n_buf, tb, conv_w), lambda i: (0, i, 0)),
            _resident((n_tok, 1, dh)),
            _resident((n_tok, 1, dh)),
            _resident((heads, n_tok, n_tok, 1, dh)),
            _resident((heads, n_tok, 1, dh)),
            _resident((heads, n_tok, 1, dh)),
            _resident((heads, 1, dh)),
            _resident((1, ret_w)),
            _resident((n_taps, conv_w)),
            _resident((1, conv_w)),
            _resident((1, conv_w)),
            _resident((1, conv_w)),
        ],
        out_specs=[
            pl.BlockSpec((n_tok, tb, ret_w + conv_w), lambda i: (0, i, 0)),
            pl.BlockSpec((tb, heads, dh, dh), lambda i: (i, 0, 0, 0)),
            pl.BlockSpec((n_buf, tb, conv_w), lambda i: (0, i, 0)),
        ],
        out_shape=[
            jax.ShapeDtypeStruct((n_tok, ns, ret_w + conv_w), BF16),
            jax.ShapeDtypeStruct(state_ret.shape, F32),
            jax.ShapeDtypeStruct(conv_tm.shape, F32),
        ],
        scratch_shapes=[pltpu.VMEM((n_tok, tb, ret_w), F32)] * 4,
        compiler_params=pltpu.CompilerParams(dimension_semantics=("arbitrary",),
                                             vmem_limit_bytes=VMEM_LIMIT_BYTES),
        name="sample_core",
    )(proj, state_ret, conv_tm, cos2.reshape(n_tok, 1, dh), sin2.reshape(n_tok, 1, dh), dm_b, dq_b, dk_b,
      gc_b, gn_gain.reshape(1, -1), dw_w, dw_b.reshape(1, -1), ln_g.reshape(1, -1), ln_b.reshape(1, -1))

    x_new, w_out_bf = pl.pallas_call(
        _sample_out_kernel,
        grid=(d // pc,),
        in_specs=[pl.BlockSpec((n_tok, ns, pc), lambda j: (0, 0, j)),
                  pl.BlockSpec((3, 1, ns, pc), lambda j: (0, 0, 0, j)),
                  _resident(mixed.shape),
                  pl.BlockSpec((w_out.shape[0], pc), lambda j: (0, j))],
        out_specs=[pl.BlockSpec((n_tok, ns, pc), lambda j: (0, 0, j)),
                   pl.BlockSpec((w_out.shape[0], pc), lambda j: (0, j))],
        out_shape=[jax.ShapeDtypeStruct(x.shape, F32), jax.ShapeDtypeStruct(w_out.shape, BF16)],
        compiler_params=params,
        name="sample_out",
    )(x, mods, mixed, w_out)
    return x_new, state_new, conv_new, w_in_bf, w_out_bf


def _layer(xp, xs, mods_p, mods_s, sret, sconv, lw, final_gain, n_tok):
    (norm_ffn1, w1g, w1u, w1d, norm_mix, w_in, gn_gain, dw_w, dw_b, ln_g, ln_b, w_out,
     norm_ffn2, w2g, w2u, w2d) = lw
    nb, seq, d = xp.shape
    n_tok, ns, _ = xs.shape
    heads = sret.shape[1]
    mods_p = mods_p.reshape(N_MOD, nb, 1, d)
    mods_s_tm = mods_s.reshape(N_MOD, 1, ns, d)

    xp, xs = _ffn(xp, xs, mods_p[0:3], mods_s_tm[0:3], norm_ffn1, w1g, w1u, w1d, None)
    xs, ret_s, conv_s_tm, w_in_bf, w_out_bf = _mix_sample(
        xs, mods_s_tm[3:6], norm_mix, w_in, gn_gain, dw_w, dw_b, ln_g, ln_b, w_out, sret,
        sconv.transpose(1, 0, 2))
    xp, ret_p, conv_p = _mix_prompt(xp, mods_p[3:6], norm_mix, w_in_bf, gn_gain, dw_w, dw_b, ln_g, ln_b,
                                    w_out_bf, heads)
    xp, xs = _ffn(xp, xs, mods_p[6:9], mods_s_tm[6:9], norm_ffn2, w2g, w2u, w2d, final_gain)
    return xp, xs, ret_p, conv_p, ret_s, conv_s_tm.transpose(1, 0, 2)


def kernel(x_prompt, x_sample, c_prompt, c_sample, state_ret, state_conv, norm_ffn1, ffn1_w_gate,
           ffn1_w_up, ffn1_w_down, norm_mix, w_in, ret_gn_gain, dw_w, dw_b, conv_ln_gain, conv_ln_bias,
           w_out, norm_ffn2, ffn2_w_gate, ffn2_w_up, ffn2_w_down, w_ada, b_ada, norm_final):
    depth = w_in.shape[0]
    nb = x_prompt.shape[0]
    ns, n_tok, d = x_sample.shape
    assert n_tok <= SUBLANES and x_prompt.shape[1] % RET_CHUNK == 0

    xp = x_prompt
    xs = x_sample.transpose(1, 0, 2)
    c_all = jnp.concatenate([c_prompt, c_sample], axis=0)

    ret_p, conv_p, ret_s, conv_s = [], [], [], []
    for l in range(depth):
        ada_p, ada_s = _ada(c_all, w_ada[l], b_ada[l], nb)
        lw = (norm_ffn1[l], ffn1_w_gate[l], ffn1_w_up[l], ffn1_w_down[l], norm_mix[l],
              w_in[l], ret_gn_gain[l], dw_w[l], dw_b[l], conv_ln_gain[l], conv_ln_bias[l],
              w_out[l], norm_ffn2[l], ffn2_w_gate[l], ffn2_w_up[l], ffn2_w_down[l])
        final_gain = norm_final if l == depth - 1 else None
        xp, xs, rp, cp, rs, cs = _layer(xp, xs, ada_p, ada_s, state_ret[l], state_conv[l], lw,
                                        final_gain, n_tok)
        ret_p.append(rp)
        conv_p.append(cp)
        ret_s.append(rs)
        conv_s.append(cs)

    return (xp, xs.transpose(1, 0, 2), jnp.stack(ret_p), jnp.stack(conv_p), jnp.stack(ret_s),
            jnp.stack(conv_s))
```

```python
import functools

import jax
import jax.numpy as jnp
import numpy as np
from jax import lax
from jax.experimental import pallas as pl
from jax.experimental.pallas import tpu as pltpu

F32 = jnp.float32
BF16 = jnp.bfloat16

PAST_LEN = 16384
RET_CHUNK = 256
ROPE_BASE = 10000.0
EPS = 1e-6
N_MOD = 9
MIX_SUB_LAYER = 1

SUBLANES = 8
LANES = 128
VMEM_LIMIT_BYTES = 56 * 1024 * 1024

FFN_TOKEN_TILE = 512
MIX_TOKEN_TILE = 512
SAMPLE_MIX_SEQ_TILE = 32
SAMPLE_PROJ_COLS = 512
ADA_ROWS_PER_STEP = 128
FF_CHUNK = 256


def _resident(shape):
    n = len(shape)
    return pl.BlockSpec(shape, lambda *_: (0,) * n, pipeline_mode=pl.Buffered(1))


def _rmsnorm(x, gain):
    ms = jnp.mean(x * x, axis=-1, keepdims=True)
    return x * lax.rsqrt(ms + EPS) * gain


def _dot(a, b):
    return jnp.dot(a, b, preferred_element_type=F32)


def _ada_kernel(cp_ref, cs_ref, w_ref, b_ref, op_ref, os_ref):
    k = pl.program_id(0)
    n_prompt = op_ref.shape[1]
    d = op_ref.shape[2]
    c = jnp.concatenate([cp_ref[...], cs_ref[...]], axis=0)
    h = (c * jax.nn.sigmoid(c)).astype(BF16)

    @pl.when(k == 0)
    def _():
        for m in range(op_ref.shape[0]):
            bias = b_ref[:, m * d:(m + 1) * d]
            op_ref[m] = jnp.broadcast_to(bias, op_ref.shape[1:])
            os_ref[m] = jnp.broadcast_to(bias, os_ref.shape[1:])

    for m in range(op_ref.shape[0]):
        part = _dot(h, w_ref[:, m * d:(m + 1) * d].astype(BF16))
        op_ref[m] += part[0:n_prompt]
        os_ref[m] += part[n_prompt:]


def _ada(c_prompt, c_sample, w_ada, b_ada):
    n_prompt, d = c_prompt.shape
    n_sample = c_sample.shape[0]
    tk = ADA_ROWS_PER_STEP
    return pl.pallas_call(
        _ada_kernel,
        grid=(d // tk,),
        in_specs=[
            pl.BlockSpec((n_prompt, tk), lambda k: (0, k)),
            pl.BlockSpec((n_sample, tk), lambda k: (0, k)),
            pl.BlockSpec((tk, N_MOD * d), lambda k: (k, 0)),
            _resident((1, N_MOD * d)),
        ],
        out_specs=[pl.BlockSpec((N_MOD, n_prompt, d), lambda k: (0, 0, 0)),
                   pl.BlockSpec((N_MOD, n_sample, d), lambda k: (0, 0, 0))],
        out_shape=[jax.ShapeDtypeStruct((N_MOD, n_prompt, d), F32),
                   jax.ShapeDtypeStruct((N_MOD, n_sample, d), F32)],
        compiler_params=pltpu.CompilerParams(dimension_semantics=("arbitrary",),
                                             vmem_limit_bytes=VMEM_LIMIT_BYTES),
        name="ada",
    )(c_prompt, c_sample, w_ada, b_ada.reshape(1, -1))


def _ffn_kernel(xp_ref, xs_ref, modp_ref, mods_ref, ng_ref, wg_ref, wu_ref, wd_ref, *rest,
                n_stage, n_prompt, final_norm):
    if final_norm:
        fg_ref, op_ref, os_ref, wg_scr, wu_scr, wd_scr, act_scr, hb_scr = rest
    else:
        op_ref, os_ref, wg_scr, wu_scr, wd_scr, act_scr, hb_scr = rest
    i = pl.program_id(0)
    n_chunks, _, ffc = wg_scr.shape

    @pl.when(i < n_stage)
    def _():
        wg_scr[i] = wg_ref[...].astype(BF16)
        wu_scr[i] = wu_ref[...].astype(BF16)
        wd_scr[pl.ds(pl.multiple_of(i * ffc, ffc), ffc), :] = wd_ref[...].astype(BF16)

    def tile(x_ref, mod_ref, o_ref):
        x = x_ref[...]
        g_, r_, d_ = x.shape
        shift, scale, gate = mod_ref[0], mod_ref[1], mod_ref[2]
        h = _rmsnorm(x, ng_ref[...]) * (1.0 + scale) + shift
        rows = g_ * r_
        hb_scr[0:rows, :] = h.astype(BF16).reshape(rows, d_)
        for c in range(n_chunks):
            g = _dot(hb_scr[0:rows, :], wg_scr[c])
            u = _dot(hb_scr[0:rows, :], wu_scr[c])
            act_scr[0:rows, c * ffc:(c + 1) * ffc] = (g * jax.nn.sigmoid(g) * u).astype(BF16)
        y = _dot(act_scr[0:rows, :], wd_scr[...]).reshape(g_, r_, d_)
        out = x + 0.5 * gate * y
        if final_norm:
            out = _rmsnorm(out, fg_ref[...])
        o_ref[...] = out

    @pl.when(jnp.logical_and(i >= n_stage, i < n_stage + n_prompt))
    def _():
        tile(xp_ref, modp_ref, op_ref)

    @pl.when(i == n_stage + n_prompt)
    def _():
        tile(xs_ref, mods_ref, os_ref)


def _ffn(xp, xs, mods_p, mods_s, sub_layer, norm_gain, wg, wu, wd, final_gain):
    nb, seq, d = xp.shape
    d_ff = wg.shape[1]
    tm = FFN_TOKEN_TILE
    ffc = FF_CHUNK
    assert seq % tm == 0 and d_ff % ffc == 0 and xs.shape[0] * xs.shape[1] <= tm
    per_seq = seq // tm
    n_stage, n_prompt = d_ff // ffc, nb * per_seq
    final_norm = final_gain is not None

    def prompt_tile(i):
        return jnp.clip(i - n_stage, 0, n_prompt - 1)

    def stage(i):
        return jnp.minimum(i, n_stage - 1)

    x_spec = pl.BlockSpec((1, tm, d), lambda i: (prompt_tile(i) // per_seq, prompt_tile(i) % per_seq, 0))
    in_specs = [
        x_spec,
        _resident(xs.shape),
        pl.BlockSpec((3, 1, 1, d), lambda i: (sub_layer, prompt_tile(i) // per_seq, 0, 0)),
        pl.BlockSpec((3,) + mods_s.shape[1:], lambda i: (sub_layer, 0, 0, 0), pipeline_mode=pl.Buffered(1)),
        _resident((1, d)),
        pl.BlockSpec((d, ffc), lambda i: (0, stage(i))),
        pl.BlockSpec((d, ffc), lambda i: (0, stage(i))),
        pl.BlockSpec((ffc, d), lambda i: (stage(i), 0)),
    ]
    args = [xp, xs, mods_p, mods_s, norm_gain.reshape(1, d), wg, wu, wd]
    if final_norm:
        in_specs.append(_resident((1, d)))
        args.append(final_gain.reshape(1, d))
    return pl.pallas_call(
        functools.partial(_ffn_kernel, n_stage=n_stage, n_prompt=n_prompt, final_norm=final_norm),
        grid=(n_stage + n_prompt + 1,),
        in_specs=in_specs,
        out_specs=[x_spec, pl.BlockSpec(xs.shape, lambda i: (0, 0, 0))],
        out_shape=[jax.ShapeDtypeStruct(xp.shape, F32), jax.ShapeDtypeStruct(xs.shape, F32)],
        scratch_shapes=[
            pltpu.VMEM((n_stage, d, ffc), BF16),
            pltpu.VMEM((n_stage, d, ffc), BF16),
            pltpu.VMEM((d_ff, d), BF16),
            pltpu.VMEM((tm, d_ff), BF16),
            pltpu.VMEM((tm, d), BF16),
        ],
        compiler_params=pltpu.CompilerParams(dimension_semantics=("arbitrary",),
                                             vmem_limit_bytes=VMEM_LIMIT_BYTES),
        name="ffn_final" if final_norm else "ffn",
    )(*args)


def _rotate(xh, cos2, sin2):
    return xh * cos2 + pltpu.roll(xh, xh.shape[-1] // 2, axis=xh.ndim - 1) * sin2


def _group_norm_gate(o, gate_pre, gain):
    mu = jnp.mean(o, axis=-1, keepdims=True)
    ctr = o - mu
    var = jnp.mean(ctr * ctr, axis=-1, keepdims=True)
    return gate_pre * jax.nn.sigmoid(gate_pre) * (ctr * lax.rsqrt(var + EPS) * gain)


def _layer_norm_silu(y, gain, bias):
    mu = jnp.mean(y, axis=-1, keepdims=True)
    ctr = y - mu
    var = jnp.mean(ctr * ctr, axis=-1, keepdims=True)
    yn = ctr * lax.rsqrt(var + EPS) * gain + bias
    return yn * jax.nn.sigmoid(yn)


def _conv_taps_slab(win_ref, dww_ref, s, n_out, lead):
    acc = None
    for j in range(dww_ref.shape[0]):
        term = win_ref[s, pl.ds(lead + j, n_out), :] * dww_ref[j:j + 1, s * LANES:(s + 1) * LANES]
        acc = term if acc is None else acc + term
    return acc


def _mix_prompt_kernel(xa_ref, xb_ref, moda_ref, modb_ref, ng_ref, win_ref, gn_ref, dww_ref, dwb_ref,
                       lng_ref, lnb_ref, wout_ref, cos_ref, sin_ref, dmask_ref, dq_ref, dk_ref, gc_ref,
                       o_ref, sout_ref, cout_ref, s_scr, u_scr, hb_scr, conv_scr, mix_scr,
                       *, heads, hist, per_seq, n_tiles):
    i = pl.program_id(0)
    tt, d_ = xa_ref.shape[1], xa_ref.shape[2]
    ret_w = gn_ref.shape[1]
    dh = ret_w // heads
    n_taps, conv_w = dww_ref.shape
    k_scale = dh ** -0.5
    o_a, o_b = 4 * ret_w, 4 * ret_w + conv_w
    n_slabs = u_scr.shape[0]

    slot_a = lax.rem(i, 2)
    slot_b = 1 - slot_a

    def stage_a_project():
        x = xa_ref[0]
        shift, scale = moda_ref[0, 0], moda_ref[1, 0]
        h = _rmsnorm(x, ng_ref[...]) * (1.0 + scale) + shift
        hb = h.astype(BF16)
        hb_scr[slot_a] = hb
        ab = _dot(hb, win_ref[:, o_a:o_b + conv_w])
        u = ab[:, :conv_w] * jax.nn.sigmoid(ab[:, conv_w:])
        for s in range(n_slabs):
            u_scr[s, hist:hist + tt, :] = u[:, s * LANES:(s + 1) * LANES]

    def stage_a_conv_slab(s):
        return _conv_taps_slab(u_scr, dww_ref, s, tt, hist - (n_taps - 1))

    def stage_a_finish(slabs):
        y = jnp.concatenate(slabs, axis=-1) + dwb_ref[...]
        conv_scr[slot_a] = _layer_norm_silu(y, lng_ref[...], lnb_ref[...]).astype(BF16)
        for s in range(n_slabs):
            u_scr[s, 0:hist, :] = u_scr[s, tt:tt + hist, :]

    def stage_b_project():
        return _dot(hb_scr[slot_b], win_ref[:, 0:o_a])

    def stage_b_head(proj, hd):
        lo = hd * dh
        cos2, sin2 = cos_ref[...], sin_ref[...]
        q = _rotate(proj[:, lo:lo + dh], cos2, sin2)
        k = _rotate(proj[:, ret_w + lo:ret_w + lo + dh], cos2, sin2) * k_scale
        v = proj[:, 2 * ret_w + lo:2 * ret_w + lo + dh]
        gate_pre = proj[:, 3 * ret_w + lo:3 * ret_w + lo + dh]
        dmask, dq, dk, gc = dmask_ref[hd], dq_ref[hd], dk_ref[hd], gc_ref[hd]
        state = s_scr[hd]
        outs = []
        for c0 in range(0, tt, RET_CHUNK):
            qc, kc = q[c0:c0 + RET_CHUNK], k[c0:c0 + RET_CHUNK]
            vb = v[c0:c0 + RET_CHUNK].astype(BF16)
            scores = lax.dot_general(qc.astype(BF16), kc.astype(BF16), (((1,), (1,)), ((), ())),
                                     preferred_element_type=F32)
            inner = _dot((scores * dmask).astype(BF16), vb)
            cross = _dot((qc * dq).astype(BF16), state.astype(BF16))
            outs.append(inner + cross)
            state = gc * state + lax.dot_general((kc * dk).astype(BF16), vb, (((0,), (0,)), ((), ())),
                                                 preferred_element_type=F32)
        s_scr[hd] = state
        o = jnp.concatenate(outs, axis=0) if len(outs) > 1 else outs[0]
        mix_scr[:, lo:lo + dh] = _group_norm_gate(o, gate_pre, gn_ref[:, lo:lo + dh]).astype(BF16)

    def stage_b_finish():
        mixed = (_dot(mix_scr[...], wout_ref[0:ret_w, :])
                 + _dot(conv_scr[slot_b], wout_ref[ret_w:ret_w + conv_w, :]))
        o_ref[0] = xb_ref[0] + modb_ref[2, 0] * mixed

    def run(do_a, do_b):
        if do_a:
            stage_a_project()
        if do_b:
            proj = stage_b_project()
            for hd in range(heads):
                stage_b_head(proj, hd)
            stage_b_finish()
        if do_a:
            stage_a_finish([stage_a_conv_slab(s) for s in range(n_slabs)])

    has_a = i < n_tiles
    has_b = i >= 1
    pos_a = lax.rem(i, per_seq)
    pos_b = lax.rem(i + per_seq - 1, per_seq)

    @pl.when(jnp.logical_and(has_a, pos_a == 0))
    def _():
        u_scr[:, 0:hist, :] = jnp.zeros((n_slabs, hist, LANES), F32)

    @pl.when(jnp.logical_and(has_b, pos_b == 0))
    def _():
        s_scr[...] = jnp.zeros_like(s_scr)

    @pl.when(i == 0)
    def _():
        run(True, False)

    @pl.when(jnp.logical_and(has_a, has_b))
    def _():
        run(True, True)

    @pl.when(i == n_tiles)
    def _():
        run(False, True)

    @pl.when(jnp.logical_and(has_a, pos_a == per_seq - 1))
    def _():
        for s in range(n_slabs):
            cout_ref[0, :, s * LANES:(s + 1) * LANES] = u_scr[s, hist - (n_taps - 1):hist, :]

    @pl.when(jnp.logical_and(has_b, pos_b == per_seq - 1))
    def _():
        sout_ref[0] = s_scr[...]


def _decay_tables(heads, chunk):
    lg = np.log(1.0 - 2.0 ** (-5.0 - np.arange(heads, dtype=np.float64)))
    idx = np.arange(chunk, dtype=np.float64)
    diff = idx[:, None] - idx[None, :]
    dmask = np.where(diff[None] >= 0, np.exp(lg[:, None, None] * np.maximum(diff, 0.0)[None]), 0.0)
    dq = np.exp(lg[:, None] * (idx[None, :] + 1.0))
    dk = np.exp(lg[:, None] * (chunk - 1.0 - idx[None, :]))
    gc = np.exp(lg * chunk)
    return dmask, dq, dk, gc


def _rotary_tables(pos0, n_pos, half):
    inv = ROPE_BASE ** (-np.arange(half, dtype=np.float64) / half)
    pos = (pos0 + np.arange(n_pos)).astype(np.float64)
    ang = pos[:, None] * inv[None, :]
    cos, sin = np.cos(ang), np.sin(ang)
    return np.concatenate([cos, cos], axis=-1), np.concatenate([-sin, sin], axis=-1)


def _const(a, shape=None):
    a = np.asarray(a, dtype=np.float32)
    if shape is not None:
        a = np.ascontiguousarray(np.broadcast_to(a, shape))
    return jnp.asarray(a)


def _mix_prompt(x, mods, norm_gain, w_in, gn_gain, dw_w, dw_b, ln_g, ln_b, w_out, heads):
    nb, seq, d = x.shape
    tt = MIX_TOKEN_TILE
    ret_w = gn_gain.shape[0]
    dh = ret_w // heads
    n_taps, conv_w = dw_w.shape
    hist = -(-(n_taps - 1) // SUBLANES) * SUBLANES
    chunk = RET_CHUNK
    cos2, sin2 = (_const(t) for t in _rotary_tables(0, seq, dh // 2))
    dmask, dq, dk, gc = _decay_tables(heads, chunk)
    dmask = _const(dmask)
    dq_b = _const(dq[:, :, None], (heads, chunk, dh))
    dk_b = _const(dk[:, :, None], (heads, chunk, dh))
    gc_b = _const(gc[:, None, None], (heads, 1, dh))
    per_seq = seq // tt
    n_tiles = nb * per_seq

    def tile_a(i):
        return jnp.minimum(i, n_tiles - 1)

    def tile_b(i):
        return jnp.maximum(i - 1, 0)

    kern = functools.partial(_mix_prompt_kernel, heads=heads, hist=hist, per_seq=per_seq, n_tiles=n_tiles)
    return pl.pallas_call(
        kern,
        grid=(n_tiles + 1,),
        in_specs=[
            pl.BlockSpec((1, tt, d), lambda i: (tile_a(i) // per_seq, tile_a(i) % per_seq, 0)),
            pl.BlockSpec((1, tt, d), lambda i: (tile_b(i) // per_seq, tile_b(i) % per_seq, 0)),
            pl.BlockSpec((3, 1, 1, d), lambda i: (MIX_SUB_LAYER, tile_a(i) // per_seq, 0, 0)),
            pl.BlockSpec((3, 1, 1, d), lambda i: (MIX_SUB_LAYER, tile_b(i) // per_seq, 0, 0)),
            _resident((1, d)),
            _resident(w_in.shape),
            _resident((1, ret_w)),
            _resident((n_taps, conv_w)),
            _resident((1, conv_w)),
            _resident((1, conv_w)),
            _resident((1, conv_w)),
            _resident(w_out.shape),
            pl.BlockSpec((tt, dh), lambda i: (tile_b(i) % per_seq, 0)),
            pl.BlockSpec((tt, dh), lambda i: (tile_b(i) % per_seq, 0)),
            _resident((heads, chunk, chunk)),
            _resident((heads, chunk, dh)),
            _resident((heads, chunk, dh)),
            _resident((heads, 1, dh)),
        ],
        out_specs=[
            pl.BlockSpec((1, tt, d), lambda i: (tile_b(i) // per_seq, tile_b(i) % per_seq, 0)),
            pl.BlockSpec((1, heads, dh, dh), lambda i: (tile_b(i) // per_seq, 0, 0, 0)),
            pl.BlockSpec((1, n_taps - 1, conv_w), lambda i: (tile_a(i) // per_seq, 0, 0)),
        ],
        out_shape=[
            jax.ShapeDtypeStruct(x.shape, F32),
            jax.ShapeDtypeStruct((nb, heads, dh, dh), F32),
            jax.ShapeDtypeStruct((nb, n_taps - 1, conv_w), F32),
        ],
        scratch_shapes=[
            pltpu.VMEM((heads, dh, dh), F32),
            pltpu.VMEM((conv_w // LANES, hist + tt, LANES), F32),
            pltpu.VMEM((2, tt, d), BF16),
            pltpu.VMEM((2, tt, conv_w), BF16),
            pltpu.VMEM((tt, ret_w), BF16),
        ],
        compiler_params=pltpu.CompilerParams(dimension_semantics=("arbitrary",),
                                             vmem_limit_bytes=VMEM_LIMIT_BYTES),
        name="mix_prompt",
    )(x, x, mods, mods, norm_gain.reshape(1, d), w_in, gn_gain.reshape(1, -1), dw_w, dw_b.reshape(1, -1),
      ln_g.reshape(1, -1), ln_b.reshape(1, -1), w_out, cos2, sin2, dmask, dq_b, dk_b, gc_b)


def _sample_proj_kernel(x_ref, mod_ref, ng_ref, win_ref, o_ref, wbf_ref, hb_scr):
    n_tok, ns, d_ = x_ref.shape

    @pl.when(pl.program_id(0) == 0)
    def _():
        h = _rmsnorm(x_ref[...], ng_ref[...]) * (1.0 + mod_ref[1]) + mod_ref[0]
        hb_scr[...] = h.astype(BF16).reshape(n_tok * ns, d_)

    w = win_ref[...].astype(BF16)
    wbf_ref[...] = w
    o_ref[...] = _dot(hb_scr[...], w).reshape(n_tok, ns, -1)


def _sample_out_kernel(x_ref, mod_ref, mixed_ref, wout_ref, o_ref, wbf_ref):
    n_tok, ns, _ = x_ref.shape
    w = wout_ref[...].astype(BF16)
    wbf_ref[...] = w
    y = _dot(mixed_ref[...].reshape(n_tok * ns, -1), w).reshape(n_tok, ns, -1)
    o_ref[...] = x_ref[...] + mod_ref[2] * y


def _sample_core_kernel(proj_ref, state_ref, cin_ref, cos_ref, sin_ref, dm_ref, dq_ref, dk_ref, gc_ref,
                        gn_ref, dww_ref, dwb_ref, lng_ref, lnb_ref, mixed_ref, sout_ref, cout_ref,
                        qd_scr, kd_scr, v_scr, cross_scr, *, heads):
    n_tok, tb, _ = proj_ref.shape
    ret_w = gn_ref.shape[1]
    dh = ret_w // heads
    n_taps, conv_w = dww_ref.shape
    n_buf = n_taps - 1
    o_a, o_b = 4 * ret_w, 4 * ret_w + conv_w
    k_scale = dh ** -0.5
    cos2, sin2 = cos_ref[...], sin_ref[...]

    inner_heads = []
    for hd in range(heads):
        lo = hd * dh
        q = _rotate(proj_ref[:, :, lo:lo + dh], cos2, sin2)
        k = _rotate(proj_ref[:, :, ret_w + lo:ret_w + lo + dh], cos2, sin2) * k_scale
        v = proj_ref[:, :, 2 * ret_w + lo:2 * ret_w + lo + dh]
        qd_scr[:, :, lo:lo + dh] = q * dq_ref[hd]
        kd_scr[:, :, lo:lo + dh] = k * dk_ref[hd]
        v_scr[:, :, lo:lo + dh] = v
        rows_out = []
        for i in range(n_tok):
            acc = None
            for j in range(i + 1):
                s_ij = jnp.sum(q[i] * k[j], axis=-1, keepdims=True)
                term = (s_ij * dm_ref[hd, i, j]) * v[j]
                acc = term if acc is None else acc + term
            rows_out.append(acc)
        inner_heads.append(jnp.stack(rows_out, axis=0))

    row = lax.broadcasted_iota(jnp.int32, (n_tok, SUBLANES, dh), 1)

    def group(g, carry):
        rows = pl.ds(pl.multiple_of(g * SUBLANES, SUBLANES), SUBLANES)
        for hd in range(heads):
            lo = hd * dh
            q_tile = qd_scr[:, rows, lo:lo + dh].reshape(n_tok * SUBLANES, dh).astype(BF16)
            k_tile = kd_scr[:, rows, lo:lo + dh].reshape(n_tok * SUBLANES, dh).astype(BF16)
            v_group = v_scr[:, rows, lo:lo + dh]
            cross = jnp.zeros((n_tok, SUBLANES, dh), F32)
            for r in range(SUBLANES):
                b = g * SUBLANES + r
                state = state_ref[b, hd]
                out = _dot(q_tile, state.astype(BF16)).reshape(n_tok, SUBLANES, dh)
                cross = jnp.where(row == r, out, cross)
                v_own = jnp.where(row == r, v_group, 0.0).reshape(n_tok * SUBLANES, dh).astype(BF16)
                upd = lax.dot_general(k_tile, v_own, (((0,), (0,)), ((), ())), preferred_element_type=F32)
                sout_ref[b, hd] = gc_ref[hd] * state + upd
            cross_scr[:, rows, lo:lo + dh] = cross
        return carry

    lax.fori_loop(0, tb // SUBLANES, group, 0)

    for hd in range(heads):
        lo = hd * dh
        o = inner_heads[hd] + cross_scr[:, :, lo:lo + dh]
        gate_pre = proj_ref[:, :, 3 * ret_w + lo:3 * ret_w + lo + dh]
        mixed_ref[:, :, lo:lo + dh] = _group_norm_gate(o, gate_pre, gn_ref[:, lo:lo + dh]).astype(BF16)

    u = proj_ref[:, :, o_a:o_a + conv_w] * jax.nn.sigmoid(proj_ref[:, :, o_b:o_b + conv_w])

    def window(s):
        return cin_ref[s] if s < n_buf else u[s - n_buf]

    for t in range(n_tok):
        acc = None
        for j in range(n_taps):
            term = window(t + j) * dww_ref[j:j + 1, :]
            acc = term if acc is None else acc + term
        y = acc + dwb_ref[...]
        mixed_ref[t, :, ret_w:ret_w + conv_w] = _layer_norm_silu(y, lng_ref[...], lnb_ref[...]).astype(BF16)
    for s in range(n_buf):
        cout_ref[s] = window(s + n_tok)


def _mix_sample(x, mods, norm_gain, w_in, gn_gain, dw_w, dw_b, ln_g, ln_b, w_out, state_ret, conv_tm):
    n_tok, ns, d = x.shape
    tb = SAMPLE_MIX_SEQ_TILE
    heads, dh = state_ret.shape[1], state_ret.shape[2]
    ret_w = heads * dh
    n_taps, conv_w = dw_w.shape
    n_buf = n_taps - 1
    n_cols = w_in.shape[1]
    pc = SAMPLE_PROJ_COLS
    params = pltpu.CompilerParams(dimension_semantics=("arbitrary",), vmem_limit_bytes=VMEM_LIMIT_BYTES)

    proj, w_in_bf = pl.pallas_call(
        _sample_proj_kernel,
        grid=(n_cols // pc,),
        in_specs=[_resident(x.shape),
                  pl.BlockSpec((3, 1, ns, d), lambda j: (MIX_SUB_LAYER, 0, 0, 0), pipeline_mode=pl.Buffered(1)),
                  _resident((1, d)),
                  pl.BlockSpec((d, pc), lambda j: (0, j))],
        out_specs=[pl.BlockSpec((n_tok, ns, pc), lambda j: (0, 0, j)),
                   pl.BlockSpec((d, pc), lambda j: (0, j))],
        out_shape=[jax.ShapeDtypeStruct((n_tok, ns, n_cols), F32),
                   jax.ShapeDtypeStruct(w_in.shape, BF16)],
        scratch_shapes=[pltpu.VMEM((n_tok * ns, d), BF16)],
        compiler_params=params,
        name="sample_proj",
    )(x, mods, norm_gain.reshape(1, d), w_in)

    cos2, sin2 = (_const(t) for t in _rotary_tables(PAST_LEN, n_tok, dh // 2))
    dmask, dq, dk, gc = _decay_tables(heads, n_tok)
    dm_b = _const(dmask[:, :, :, None, None], (heads, n_tok, n_tok, 1, dh))
    dq_b = _const(dq[:, :, None, None], (heads, n_tok, 1, dh))
    dk_b = _const(dk[:, :, None, None], (heads, n_tok, 1, dh))
    gc_b = _const(gc[:, None, None], (heads, 1, dh))
    mixed, state_new, conv_new = pl.pallas_call(
        functools.partial(_sample_core_kernel, heads=heads),
        grid=(ns // tb,),
        in_specs=[
            pl.BlockSpec((n_tok, tb, n_cols), lambda i: (0, i, 0)),
            pl.BlockSpec((tb, heads, dh, dh), lambda i: (i, 0, 0, 0)),
            pl.BlockSpec((n_buf, tb, conv_w), lambda i: (0, i, 0)),
            _resident((n_tok, 1, dh)),
            _resident((n_tok, 1, dh)),
            _resident((heads, n_tok, n_tok, 1, dh)),
            _resident((heads, n_tok, 1, dh)),
            _resident((heads, n_tok, 1, dh)),
            _resident((heads, 1, dh)),
            _resident((1, ret_w)),
            _resident((n_taps, conv_w)),
            _resident((1, conv_w)),
            _resident((1, conv_w)),
            _resident((1, conv_w)),
        ],
        out_specs=[
            pl.BlockSpec((n_tok, tb, ret_w + conv_w), lambda i: (0, i, 0)),
            pl.BlockSpec((tb, heads, dh, dh), lambda i: (i, 0, 0, 0)),
            pl.BlockSpec((n_buf, tb, conv_w), lambda i: (0, i, 0)),
        ],
        out_shape=[
            jax.ShapeDtypeStruct((n_tok, ns, ret_w + conv_w), BF16),
            jax.ShapeDtypeStruct(state_ret.shape, F32),
            jax.ShapeDtypeStruct(conv_tm.shape, F32),
        ],
        scratch_shapes=[pltpu.VMEM((n_tok, tb, ret_w), F32)] * 4,
        compiler_params=pltpu.CompilerParams(dimension_semantics=("arbitrary",),
                                             vmem_limit_bytes=VMEM_LIMIT_BYTES),
        name="sample_core",
    )(proj, state_ret, conv_tm, cos2.reshape(n_tok, 1, dh), sin2.reshape(n_tok, 1, dh), dm_b, dq_b, dk_b,
      gc_b, gn_gain.reshape(1, -1), dw_w, dw_b.reshape(1, -1), ln_g.reshape(1, -1), ln_b.reshape(1, -1))

    x_new, w_out_bf = pl.pallas_call(
        _sample_out_kernel,
        grid=(d // pc,),
        in_specs=[pl.BlockSpec((n_tok, ns, pc), lambda j: (0, 0, j)),
                  pl.BlockSpec((3, 1, ns, pc), lambda j: (MIX_SUB_LAYER, 0, 0, j)),
                  _resident(mixed.shape),
                  pl.BlockSpec((w_out.shape[0], pc), lambda j: (0, j))],
        out_specs=[pl.BlockSpec((n_tok, ns, pc), lambda j: (0, 0, j)),
                   pl.BlockSpec((w_out.shape[0], pc), lambda j: (0, j))],
        out_shape=[jax.ShapeDtypeStruct(x.shape, F32), jax.ShapeDtypeStruct(w_out.shape, BF16)],
        compiler_params=params,
        name="sample_out",
    )(x, mods, mixed, w_out)
    return x_new, state_new, conv_new, w_in_bf, w_out_bf


def _layer(xp, xs, mods_p, mods_s, sret, sconv, lw, final_gain, n_tok):
    (norm_ffn1, w1g, w1u, w1d, norm_mix, w_in, gn_gain, dw_w, dw_b, ln_g, ln_b, w_out,
     norm_ffn2, w2g, w2u, w2d) = lw
    nb, seq, d = xp.shape
    n_tok, ns, _ = xs.shape
    heads = sret.shape[1]
    mods_p = mods_p.reshape(N_MOD, nb, 1, d)
    mods_s_tm = mods_s.reshape(N_MOD, 1, ns, d)

    xp, xs = _ffn(xp, xs, mods_p, mods_s_tm, 0, norm_ffn1, w1g, w1u, w1d, None)
    xs, ret_s, conv_s_tm, w_in_bf, w_out_bf = _mix_sample(
        xs, mods_s_tm, norm_mix, w_in, gn_gain, dw_w, dw_b, ln_g, ln_b, w_out, sret,
        sconv.transpose(1, 0, 2))
    xp, ret_p, conv_p = _mix_prompt(xp, mods_p, norm_mix, w_in_bf, gn_gain, dw_w, dw_b, ln_g, ln_b,
                                    w_out_bf, heads)
    xp, xs = _ffn(xp, xs, mods_p, mods_s_tm, 2, norm_ffn2, w2g, w2u, w2d, final_gain)
    return xp, xs, ret_p, conv_p, ret_s, conv_s_tm.transpose(1, 0, 2)


def kernel(x_prompt, x_sample, c_prompt, c_sample, state_ret, state_conv, norm_ffn1, ffn1_w_gate,
           ffn1_w_up, ffn1_w_down, norm_mix, w_in, ret_gn_gain, dw_w, dw_b, conv_ln_gain, conv_ln_bias,
           w_out, norm_ffn2, ffn2_w_gate, ffn2_w_up, ffn2_w_down, w_ada, b_ada, norm_final):
    depth = w_in.shape[0]
    nb = x_prompt.shape[0]
    ns, n_tok, d = x_sample.shape
    assert n_tok <= SUBLANES and x_prompt.shape[1] % RET_CHUNK == 0

    xp = x_prompt
    xs = x_sample.transpose(1, 0, 2)

    ret_p, conv_p, ret_s, conv_s = [], [], [], []
    for l in range(depth):
        ada_p, ada_s = _ada(c_prompt, c_sample, w_ada[l], b_ada[l])
        lw = (norm_ffn1[l], ffn1_w_gate[l], ffn1_w_up[l], ffn1_w_down[l], norm_mix[l],
              w_in[l], ret_gn_gain[l], dw_w[l], dw_b[l], conv_ln_gain[l], conv_ln_bias[l],
              w_out[l], norm_ffn2[l], ffn2_w_gate[l], ffn2_w_up[l], ffn2_w_down[l])
        final_gain = norm_final if l == depth - 1 else None
        xp, xs, rp, cp, rs, cs = _layer(xp, xs, ada_p, ada_s, state_ret[l], state_conv[l], lw,
                                        final_gain, n_tok)
        ret_p.append(rp)
        conv_p.append(cp)
        ret_s.append(rs)
        conv_s.append(cs)

    return (xp, xs.transpose(1, 0, 2), jnp.stack(ret_p), jnp.stack(conv_p), jnp.stack(ret_s),
            jnp.stack(conv_s))
```

```python
import functools

import jax
import jax.numpy as jnp
import numpy as np
from jax import lax
from jax.experimental import pallas as pl
from jax.experimental.pallas import tpu as pltpu

F32 = jnp.float32
BF16 = jnp.bfloat16

PAST_LEN = 16384
RET_CHUNK = 256
ROPE_BASE = 10000.0
EPS = 1e-6
N_MOD = 9
MIX_SUB_LAYER = 1

SUBLANES = 8
LANES = 128
VMEM_LIMIT_BYTES = 56 * 1024 * 1024

FFN_TOKEN_TILE = 1024
MIX_TOKEN_TILE = 512
SAMPLE_MIX_SEQ_TILE = 32
SAMPLE_PROJ_COLS = 512
ADA_ROWS_PER_STEP = 128
FF_CHUNK = 256


def _resident(shape):
    n = len(shape)
    return pl.BlockSpec(shape, lambda *_: (0,) * n, pipeline_mode=pl.Buffered(1))


def _rmsnorm(x, gain):
    ms = jnp.mean(x * x, axis=-1, keepdims=True)
    return x * lax.rsqrt(ms + EPS) * gain


def _dot(a, b):
    return jnp.dot(a, b, preferred_element_type=F32)


def _ada_kernel(cp_ref, cs_ref, w_ref, b_ref, op_ref, os_ref):
    k = pl.program_id(0)
    n_prompt = op_ref.shape[1]
    d = op_ref.shape[2]
    c = jnp.concatenate([cp_ref[...], cs_ref[...]], axis=0)
    h = (c * jax.nn.sigmoid(c)).astype(BF16)

    @pl.when(k == 0)
    def _():
        for m in range(op_ref.shape[0]):
            bias = b_ref[:, m * d:(m + 1) * d]
            op_ref[m] = jnp.broadcast_to(bias, op_ref.shape[1:])
            os_ref[m] = jnp.broadcast_to(bias, os_ref.shape[1:])

    for m in range(op_ref.shape[0]):
        part = _dot(h, w_ref[:, m * d:(m + 1) * d].astype(BF16))
        op_ref[m] += part[0:n_prompt]
        os_ref[m] += part[n_prompt:]


def _ada(c_prompt, c_sample, w_ada, b_ada):
    n_prompt, d = c_prompt.shape
    n_sample = c_sample.shape[0]
    tk = ADA_ROWS_PER_STEP
    return pl.pallas_call(
        _ada_kernel,
        grid=(d // tk,),
        in_specs=[
            pl.BlockSpec((n_prompt, tk), lambda k: (0, k)),
            pl.BlockSpec((n_sample, tk), lambda k: (0, k)),
            pl.BlockSpec((tk, N_MOD * d), lambda k: (k, 0)),
            _resident((1, N_MOD * d)),
        ],
        out_specs=[pl.BlockSpec((N_MOD, n_prompt, d), lambda k: (0, 0, 0)),
                   pl.BlockSpec((N_MOD, n_sample, d), lambda k: (0, 0, 0))],
        out_shape=[jax.ShapeDtypeStruct((N_MOD, n_prompt, d), F32),
                   jax.ShapeDtypeStruct((N_MOD, n_sample, d), F32)],
        compiler_params=pltpu.CompilerParams(dimension_semantics=("arbitrary",),
                                             vmem_limit_bytes=VMEM_LIMIT_BYTES),
        name="ada",
    )(c_prompt, c_sample, w_ada, b_ada.reshape(1, -1))


def _ffn_kernel(xp_ref, xs_ref, modp_ref, mods_ref, ng_ref, wg_ref, wu_ref, wd_ref, *rest,
                n_stage, n_prompt, final_norm):
    if final_norm:
        fg_ref, op_ref, os_ref, wg_scr, wu_scr, wd_scr, act_scr, hb_scr = rest
    else:
        op_ref, os_ref, wg_scr, wu_scr, wd_scr, act_scr, hb_scr = rest
    i = pl.program_id(0)
    n_chunks, _, ffc = wg_scr.shape

    @pl.when(i < n_stage)
    def _():
        wg_scr[i] = wg_ref[...].astype(BF16)
        wu_scr[i] = wu_ref[...].astype(BF16)
        wd_scr[pl.ds(pl.multiple_of(i * ffc, ffc), ffc), :] = wd_ref[...].astype(BF16)

    def tile(x_ref, mod_ref, o_ref):
        x = x_ref[...]
        g_, r_, d_ = x.shape
        shift, scale, gate = mod_ref[0], mod_ref[1], mod_ref[2]
        h = _rmsnorm(x, ng_ref[...]) * (1.0 + scale) + shift
        rows = g_ * r_
        hb_scr[0:rows, :] = h.astype(BF16).reshape(rows, d_)
        for c in range(n_chunks):
            g = _dot(hb_scr[0:rows, :], wg_scr[c])
            u = _dot(hb_scr[0:rows, :], wu_scr[c])
            act_scr[0:rows, c * ffc:(c + 1) * ffc] = (g * jax.nn.sigmoid(g) * u).astype(BF16)
        y = _dot(act_scr[0:rows, :], wd_scr[...]).reshape(g_, r_, d_)
        out = x + 0.5 * gate * y
        if final_norm:
            out = _rmsnorm(out, fg_ref[...])
        o_ref[...] = out

    @pl.when(jnp.logical_and(i >= n_stage, i < n_stage + n_prompt))
    def _():
        tile(xp_ref, modp_ref, op_ref)

    @pl.when(i == n_stage + n_prompt)
    def _():
        tile(xs_ref, mods_ref, os_ref)


def _ffn(xp, xs, mods_p, mods_s, sub_layer, norm_gain, wg, wu, wd, final_gain):
    nb, seq, d = xp.shape
    d_ff = wg.shape[1]
    tm = FFN_TOKEN_TILE
    ffc = FF_CHUNK
    assert seq % tm == 0 and d_ff % ffc == 0 and xs.shape[0] * xs.shape[1] <= tm
    per_seq = seq // tm
    n_stage, n_prompt = d_ff // ffc, nb * per_seq
    final_norm = final_gain is not None

    def prompt_tile(i):
        return jnp.clip(i - n_stage, 0, n_prompt - 1)

    def stage(i):
        return jnp.minimum(i, n_stage - 1)

    x_spec = pl.BlockSpec((1, tm, d), lambda i: (prompt_tile(i) // per_seq, prompt_tile(i) % per_seq, 0))
    in_specs = [
        x_spec,
        _resident(xs.shape),
        pl.BlockSpec((3, 1, 1, d), lambda i: (sub_layer, prompt_tile(i) // per_seq, 0, 0)),
        pl.BlockSpec((3,) + mods_s.shape[1:], lambda i: (sub_layer, 0, 0, 0), pipeline_mode=pl.Buffered(1)),
        _resident((1, d)),
        pl.BlockSpec((d, ffc), lambda i: (0, stage(i))),
        pl.BlockSpec((d, ffc), lambda i: (0, stage(i))),
        pl.BlockSpec((ffc, d), lambda i: (stage(i), 0)),
    ]
    args = [xp, xs, mods_p, mods_s, norm_gain.reshape(1, d), wg, wu, wd]
    if final_norm:
        in_specs.append(_resident((1, d)))
        args.append(final_gain.reshape(1, d))
    return pl.pallas_call(
        functools.partial(_ffn_kernel, n_stage=n_stage, n_prompt=n_prompt, final_norm=final_norm),
        grid=(n_stage + n_prompt + 1,),
        in_specs=in_specs,
        out_specs=[x_spec, pl.BlockSpec(xs.shape, lambda i: (0, 0, 0))],
        out_shape=[jax.ShapeDtypeStruct(xp.shape, F32), jax.ShapeDtypeStruct(xs.shape, F32)],
        scratch_shapes=[
            pltpu.VMEM((n_stage, d, ffc), BF16),
            pltpu.VMEM((n_stage, d, ffc), BF16),
            pltpu.VMEM((d_ff, d), BF16),
            pltpu.VMEM((tm, d_ff), BF16),
            pltpu.VMEM((tm, d), BF16),
        ],
        compiler_params=pltpu.CompilerParams(dimension_semantics=("arbitrary",),
                                             vmem_limit_bytes=VMEM_LIMIT_BYTES),
        name="ffn_final" if final_norm else "ffn",
    )(*args)


def _rotate(xh, cos2, sin2):
    return xh * cos2 + pltpu.roll(xh, xh.shape[-1] // 2, axis=xh.ndim - 1) * sin2


def _group_norm_gate(o, gate_pre, gain):
    mu = jnp.mean(o, axis=-1, keepdims=True)
    ctr = o - mu
    var = jnp.mean(ctr * ctr, axis=-1, keepdims=True)
    return gate_pre * jax.nn.sigmoid(gate_pre) * (ctr * lax.rsqrt(var + EPS) * gain)


def _layer_norm_silu(y, gain, bias):
    mu = jnp.mean(y, axis=-1, keepdims=True)
    ctr = y - mu
    var = jnp.mean(ctr * ctr, axis=-1, keepdims=True)
    yn = ctr * lax.rsqrt(var + EPS) * gain + bias
    return yn * jax.nn.sigmoid(yn)


def _conv_taps_slab(win_ref, dww_ref, s, n_out, lead):
    acc = None
    for j in range(dww_ref.shape[0]):
        term = win_ref[s, pl.ds(lead + j, n_out), :] * dww_ref[j:j + 1, s * LANES:(s + 1) * LANES]
        acc = term if acc is None else acc + term
    return acc


def _mix_prompt_kernel(xa_ref, xb_ref, moda_ref, modb_ref, ng_ref, win_ref, gn_ref, dww_ref, dwb_ref,
                       lng_ref, lnb_ref, wout_ref, cos_ref, sin_ref, dmask_ref, dq_ref, dk_ref, gc_ref,
                       o_ref, sout_ref, cout_ref, s_scr, u_scr, hb_scr, conv_scr, mix_scr,
                       *, heads, hist, per_seq, n_tiles):
    i = pl.program_id(0)
    tt, d_ = xa_ref.shape[1], xa_ref.shape[2]
    ret_w = gn_ref.shape[1]
    dh = ret_w // heads
    n_taps, conv_w = dww_ref.shape
    k_scale = dh ** -0.5
    o_a, o_b = 4 * ret_w, 4 * ret_w + conv_w
    n_slabs = u_scr.shape[0]

    slot_a = lax.rem(i, 2)
    slot_b = 1 - slot_a

    def stage_a_project():
        x = xa_ref[0]
        shift, scale = moda_ref[0, 0], moda_ref[1, 0]
        h = _rmsnorm(x, ng_ref[...]) * (1.0 + scale) + shift
        hb = h.astype(BF16)
        hb_scr[slot_a] = hb
        ab = _dot(hb, win_ref[:, o_a:o_b + conv_w])
        u = ab[:, :conv_w] * jax.nn.sigmoid(ab[:, conv_w:])
        for s in range(n_slabs):
            u_scr[s, hist:hist + tt, :] = u[:, s * LANES:(s + 1) * LANES]

    def stage_a_conv_slab(s):
        return _conv_taps_slab(u_scr, dww_ref, s, tt, hist - (n_taps - 1))

    def stage_a_finish(slabs):
        y = jnp.concatenate(slabs, axis=-1) + dwb_ref[...]
        conv_scr[slot_a] = _layer_norm_silu(y, lng_ref[...], lnb_ref[...]).astype(BF16)
        for s in range(n_slabs):
            u_scr[s, 0:hist, :] = u_scr[s, tt:tt + hist, :]

    def stage_b_project():
        return _dot(hb_scr[slot_b], win_ref[:, 0:o_a])

    def stage_b_head(proj, hd):
        lo = hd * dh
        cos2, sin2 = cos_ref[...], sin_ref[...]
        q = _rotate(proj[:, lo:lo + dh], cos2, sin2)
        k = _rotate(proj[:, ret_w + lo:ret_w + lo + dh], cos2, sin2) * k_scale
        v = proj[:, 2 * ret_w + lo:2 * ret_w + lo + dh]
        gate_pre = proj[:, 3 * ret_w + lo:3 * ret_w + lo + dh]
        dmask, dq, dk, gc = dmask_ref[hd], dq_ref[hd], dk_ref[hd], gc_ref[hd]
        state = s_scr[hd]
        outs = []
        for c0 in range(0, tt, RET_CHUNK):
            qc, kc = q[c0:c0 + RET_CHUNK], k[c0:c0 + RET_CHUNK]
            vb = v[c0:c0 + RET_CHUNK].astype(BF16)
            scores = lax.dot_general(qc.astype(BF16), kc.astype(BF16), (((1,), (1,)), ((), ())),
                                     preferred_element_type=F32)
            inner = _dot((scores * dmask).astype(BF16), vb)
            cross = _dot((qc * dq).astype(BF16), state.astype(BF16))
            outs.append(inner + cross)
            state = gc * state + lax.dot_general((kc * dk).astype(BF16), vb, (((0,), (0,)), ((), ())),
                                                 preferred_element_type=F32)
        s_scr[hd] = state
        o = jnp.concatenate(outs, axis=0) if len(outs) > 1 else outs[0]
        mix_scr[:, lo:lo + dh] = _group_norm_gate(o, gate_pre, gn_ref[:, lo:lo + dh]).astype(BF16)

    def stage_b_finish():
        mixed = (_dot(mix_scr[...], wout_ref[0:ret_w, :])
                 + _dot(conv_scr[slot_b], wout_ref[ret_w:ret_w + conv_w, :]))
        o_ref[0] = xb_ref[0] + modb_ref[2, 0] * mixed

    def run(do_a, do_b):
        if do_a:
            stage_a_project()
        if do_b:
            proj = stage_b_project()
            for hd in range(heads):
                stage_b_head(proj, hd)
            stage_b_finish()
        if do_a:
            stage_a_finish([stage_a_conv_slab(s) for s in range(n_slabs)])

    has_a = i < n_tiles
    has_b = i >= 1
    pos_a = lax.rem(i, per_seq)
    pos_b = lax.rem(i + per_seq - 1, per_seq)

    @pl.when(jnp.logical_and(has_a, pos_a == 0))
    def _():
        u_scr[:, 0:hist, :] = jnp.zeros((n_slabs, hist, LANES), F32)

    @pl.when(jnp.logical_and(has_b, pos_b == 0))
    def _():
        s_scr[...] = jnp.zeros_like(s_scr)

    @pl.when(i == 0)
    def _():
        run(True, False)

    @pl.when(jnp.logical_and(has_a, has_b))
    def _():
        run(True, True)

    @pl.when(i == n_tiles)
    def _():
        run(False, True)

    @pl.when(jnp.logical_and(has_a, pos_a == per_seq - 1))
    def _():
        for s in range(n_slabs):
            cout_ref[0, :, s * LANES:(s + 1) * LANES] = u_scr[s, hist - (n_taps - 1):hist, :]

    @pl.when(jnp.logical_and(has_b, pos_b == per_seq - 1))
    def _():
        sout_ref[0] = s_scr[...]


def _decay_tables(heads, chunk):
    lg = np.log(1.0 - 2.0 ** (-5.0 - np.arange(heads, dtype=np.float64)))
    idx = np.arange(chunk, dtype=np.float64)
    diff = idx[:, None] - idx[None, :]
    dmask = np.where(diff[None] >= 0, np.exp(lg[:, None, None] * np.maximum(diff, 0.0)[None]), 0.0)
    dq = np.exp(lg[:, None] * (idx[None, :] + 1.0))
    dk = np.exp(lg[:, None] * (chunk - 1.0 - idx[None, :]))
    gc = np.exp(lg * chunk)
    return dmask, dq, dk, gc


def _rotary_tables(pos0, n_pos, half):
    inv = ROPE_BASE ** (-np.arange(half, dtype=np.float64) / half)
    pos = (pos0 + np.arange(n_pos)).astype(np.float64)
    ang = pos[:, None] * inv[None, :]
    cos, sin = np.cos(ang), np.sin(ang)
    return np.concatenate([cos, cos], axis=-1), np.concatenate([-sin, sin], axis=-1)


def _const(a, shape=None):
    a = np.asarray(a, dtype=np.float32)
    if shape is not None:
        a = np.ascontiguousarray(np.broadcast_to(a, shape))
    return jnp.asarray(a)


def _mix_prompt(x, mods, norm_gain, w_in, gn_gain, dw_w, dw_b, ln_g, ln_b, w_out, heads):
    nb, seq, d = x.shape
    tt = MIX_TOKEN_TILE
    ret_w = gn_gain.shape[0]
    dh = ret_w // heads
    n_taps, conv_w = dw_w.shape
    hist = -(-(n_taps - 1) // SUBLANES) * SUBLANES
    chunk = RET_CHUNK
    cos2, sin2 = (_const(t) for t in _rotary_tables(0, seq, dh // 2))
    dmask, dq, dk, gc = _decay_tables(heads, chunk)
    dmask = _const(dmask)
    dq_b = _const(dq[:, :, None], (heads, chunk, dh))
    dk_b = _const(dk[:, :, None], (heads, chunk, dh))
    gc_b = _const(gc[:, None, None], (heads, 1, dh))
    per_seq = seq // tt
    n_tiles = nb * per_seq

    def tile_a(i):
        return jnp.minimum(i, n_tiles - 1)

    def tile_b(i):
        return jnp.maximum(i - 1, 0)

    kern = functools.partial(_mix_prompt_kernel, heads=heads, hist=hist, per_seq=per_seq, n_tiles=n_tiles)
    return pl.pallas_call(
        kern,
        grid=(n_tiles + 1,),
        in_specs=[
            pl.BlockSpec((1, tt, d), lambda i: (tile_a(i) // per_seq, tile_a(i) % per_seq, 0)),
            pl.BlockSpec((1, tt, d), lambda i: (tile_b(i) // per_seq, tile_b(i) % per_seq, 0)),
            pl.BlockSpec((3, 1, 1, d), lambda i: (MIX_SUB_LAYER, tile_a(i) // per_seq, 0, 0)),
            pl.BlockSpec((3, 1, 1, d), lambda i: (MIX_SUB_LAYER, tile_b(i) // per_seq, 0, 0)),
            _resident((1, d)),
            _resident(w_in.shape),
            _resident((1, ret_w)),
            _resident((n_taps, conv_w)),
            _resident((1, conv_w)),
            _resident((1, conv_w)),
            _resident((1, conv_w)),
            _resident(w_out.shape),
            pl.BlockSpec((tt, dh), lambda i: (tile_b(i) % per_seq, 0)),
            pl.BlockSpec((tt, dh), lambda i: (tile_b(i) % per_seq, 0)),
            _resident((heads, chunk, chunk)),
            _resident((heads, chunk, dh)),
            _resident((heads, chunk, dh)),
            _resident((heads, 1, dh)),
        ],
        out_specs=[
            pl.BlockSpec((1, tt, d), lambda i: (tile_b(i) // per_seq, tile_b(i) % per_seq, 0)),
            pl.BlockSpec((1, heads, dh, dh), lambda i: (tile_b(i) // per_seq, 0, 0, 0)),
            pl.BlockSpec((1, n_taps - 1, conv_w), lambda i: (tile_a(i) // per_seq, 0, 0)),
        ],
        out_shape=[
            jax.ShapeDtypeStruct(x.shape, F32),
            jax.ShapeDtypeStruct((nb, heads, dh, dh), F32),
            jax.ShapeDtypeStruct((nb, n_taps - 1, conv_w), F32),
        ],
        scratch_shapes=[
            pltpu.VMEM((heads, dh, dh), F32),
            pltpu.VMEM((conv_w // LANES, hist + tt, LANES), F32),
            pltpu.VMEM((2, tt, d), BF16),
            pltpu.VMEM((2, tt, conv_w), BF16),
            pltpu.VMEM((tt, ret_w), BF16),
        ],
        compiler_params=pltpu.CompilerParams(dimension_semantics=("arbitrary",),
                                             vmem_limit_bytes=VMEM_LIMIT_BYTES),
        name="mix_prompt",
    )(x, x, mods, mods, norm_gain.reshape(1, d), w_in, gn_gain.reshape(1, -1), dw_w, dw_b.reshape(1, -1),
      ln_g.reshape(1, -1), ln_b.reshape(1, -1), w_out, cos2, sin2, dmask, dq_b, dk_b, gc_b)


def _sample_proj_kernel(x_ref, mod_ref, ng_ref, win_ref, o_ref, wbf_ref, hb_scr):
    n_tok, ns, d_ = x_ref.shape

    @pl.when(pl.program_id(0) == 0)
    def _():
        h = _rmsnorm(x_ref[...], ng_ref[...]) * (1.0 + mod_ref[1]) + mod_ref[0]
        hb_scr[...] = h.astype(BF16).reshape(n_tok * ns, d_)

    w = win_ref[...].astype(BF16)
    wbf_ref[...] = w
    o_ref[...] = _dot(hb_scr[...], w).reshape(n_tok, ns, -1)


def _sample_out_kernel(x_ref, mod_ref, mixed_ref, wout_ref, o_ref, wbf_ref):
    n_tok, ns, _ = x_ref.shape
    w = wout_ref[...].astype(BF16)
    wbf_ref[...] = w
    y = _dot(mixed_ref[...].reshape(n_tok * ns, -1), w).reshape(n_tok, ns, -1)
    o_ref[...] = x_ref[...] + mod_ref[2] * y


def _sample_core_kernel(proj_ref, state_ref, cin_ref, cos_ref, sin_ref, dm_ref, dq_ref, dk_ref, gc_ref,
                        gn_ref, dww_ref, dwb_ref, lng_ref, lnb_ref, mixed_ref, sout_ref, cout_ref,
                        qd_scr, kd_scr, v_scr, cross_scr, *, heads):
    n_tok, tb, _ = proj_ref.shape
    ret_w = gn_ref.shape[1]
    dh = ret_w // heads
    n_taps, conv_w = dww_ref.shape
    n_buf = n_taps - 1
    o_a, o_b = 4 * ret_w, 4 * ret_w + conv_w
    k_scale = dh ** -0.5
    cos2, sin2 = cos_ref[...], sin_ref[...]

    inner_heads = []
    for hd in range(heads):
        lo = hd * dh
        q = _rotate(proj_ref[:, :, lo:lo + dh], cos2, sin2)
        k = _rotate(proj_ref[:, :, ret_w + lo:ret_w + lo + dh], cos2, sin2) * k_scale
        v = proj_ref[:, :, 2 * ret_w + lo:2 * ret_w + lo + dh]
        qd_scr[:, :, lo:lo + dh] = q * dq_ref[hd]
        kd_scr[:, :, lo:lo + dh] = k * dk_ref[hd]
        v_scr[:, :, lo:lo + dh] = v
        rows_out = []
        for i in range(n_tok):
            acc = None
            for j in range(i + 1):
                s_ij = jnp.sum(q[i] * k[j], axis=-1, keepdims=True)
                term = (s_ij * dm_ref[hd, i, j]) * v[j]
                acc = term if acc is None else acc + term
            rows_out.append(acc)
        inner_heads.append(jnp.stack(rows_out, axis=0))

    row = lax.broadcasted_iota(jnp.int32, (n_tok, SUBLANES, dh), 1)

    def group(g, carry):
        rows = pl.ds(pl.multiple_of(g * SUBLANES, SUBLANES), SUBLANES)
        for hd in range(heads):
            lo = hd * dh
            q_tile = qd_scr[:, rows, lo:lo + dh].reshape(n_tok * SUBLANES, dh).astype(BF16)
            k_tile = kd_scr[:, rows, lo:lo + dh].reshape(n_tok * SUBLANES, dh).astype(BF16)
            v_group = v_scr[:, rows, lo:lo + dh]
            cross = jnp.zeros((n_tok, SUBLANES, dh), F32)
            for r in range(SUBLANES):
                b = g * SUBLANES + r
                state = state_ref[b, hd]
                out = _dot(q_tile, state.astype(BF16)).reshape(n_tok, SUBLANES, dh)
                cross = jnp.where(row == r, out, cross)
                v_own = jnp.where(row == r, v_group, 0.0).reshape(n_tok * SUBLANES, dh).astype(BF16)
                upd = lax.dot_general(k_tile, v_own, (((0,), (0,)), ((), ())), preferred_element_type=F32)
                sout_ref[b, hd] = gc_ref[hd] * state + upd
            cross_scr[:, rows, lo:lo + dh] = cross
        return carry

    lax.fori_loop(0, tb // SUBLANES, group, 0)

    for hd in range(heads):
        lo = hd * dh
        o = inner_heads[hd] + cross_scr[:, :, lo:lo + dh]
        gate_pre = proj_ref[:, :, 3 * ret_w + lo:3 * ret_w + lo + dh]
        mixed_ref[:, :, lo:lo + dh] = _group_norm_gate(o, gate_pre, gn_ref[:, lo:lo + dh]).astype(BF16)

    u = proj_ref[:, :, o_a:o_a + conv_w] * jax.nn.sigmoid(proj_ref[:, :, o_b:o_b + conv_w])

    def window(s):
        return cin_ref[s] if s < n_buf else u[s - n_buf]

    for t in range(n_tok):
        acc = None
        for j in range(n_taps):
            term = window(t + j) * dww_ref[j:j + 1, :]
            acc = term if acc is None else acc + term
        y = acc + dwb_ref[...]
        mixed_ref[t, :, ret_w:ret_w + conv_w] = _layer_norm_silu(y, lng_ref[...], lnb_ref[...]).astype(BF16)
    for s in range(n_buf):
        cout_ref[s] = window(s + n_tok)


def _mix_sample(x, mods, norm_gain, w_in, gn_gain, dw_w, dw_b, ln_g, ln_b, w_out, state_ret, conv_tm):
    n_tok, ns, d = x.shape
    tb = SAMPLE_MIX_SEQ_TILE
    heads, dh = state_ret.shape[1], state_ret.shape[2]
    ret_w = heads * dh
    n_taps, conv_w = dw_w.shape
    n_buf = n_taps - 1
    n_cols = w_in.shape[1]
    pc = SAMPLE_PROJ_COLS
    params = pltpu.CompilerParams(dimension_semantics=("arbitrary",), vmem_limit_bytes=VMEM_LIMIT_BYTES)

    proj, w_in_bf = pl.pallas_call(
        _sample_proj_kernel,
        grid=(n_cols // pc,),
        in_specs=[_resident(x.shape),
                  pl.BlockSpec((3, 1, ns, d), lambda j: (MIX_SUB_LAYER, 0, 0, 0), pipeline_mode=pl.Buffered(1)),
                  _resident((1, d)),
                  pl.BlockSpec((d, pc), lambda j: (0, j))],
        out_specs=[pl.BlockSpec((n_tok, ns, pc), lambda j: (0, 0, j)),
                   pl.BlockSpec((d, pc), lambda j: (0, j))],
        out_shape=[jax.ShapeDtypeStruct((n_tok, ns, n_cols), F32),
                   jax.ShapeDtypeStruct(w_in.shape, BF16)],
        scratch_shapes=[pltpu.VMEM((n_tok * ns, d), BF16)],
        compiler_params=params,
        name="sample_proj",
    )(x, mods, norm_gain.reshape(1, d), w_in)

    cos2, sin2 = (_const(t) for t in _rotary_tables(PAST_LEN, n_tok, dh // 2))
    dmask, dq, dk, gc = _decay_tables(heads, n_tok)
    dm_b = _const(dmask[:, :, :, None, None], (heads, n_tok, n_tok, 1, dh))
    dq_b = _const(dq[:, :, None, None], (heads, n_tok, 1, dh))
    dk_b = _const(dk[:, :, None, None], (heads, n_tok, 1, dh))
    gc_b = _const(gc[:, None, None], (heads, 1, dh))
    mixed, state_new, conv_new = pl.pallas_call(
        functools.partial(_sample_core_kernel, heads=heads),
        grid=(ns // tb,),
        in_specs=[
            pl.BlockSpec((n_tok, tb, n_cols), lambda i: (0, i, 0)),
            pl.BlockSpec((tb, heads, dh, dh), lambda i: (i, 0, 0, 0)),
            pl.BlockSpec((n_buf, tb, conv_w), lambda i: (0, i, 0)),
            _resident((n_tok, 1, dh)),
            _resident((n_tok, 1, dh)),
            _resident((heads, n_tok, n_tok, 1, dh)),
            _resident((heads, n_tok, 1, dh)),
            _resident((heads, n_tok, 1, dh)),
            _resident((heads, 1, dh)),
            _resident((1, ret_w)),
            _resident((n_taps, conv_w)),
            _resident((1, conv_w)),
            _resident((1, conv_w)),
            _resident((1, conv_w)),
        ],
        out_specs=[
            pl.BlockSpec((n_tok, tb, ret_w + conv_w), lambda i: (0, i, 0)),
            pl.BlockSpec((tb, heads, dh, dh), lambda i: (i, 0, 0, 0)),
            pl.BlockSpec((n_buf, tb, conv_w), lambda i: (0, i, 0)),
        ],
        out_shape=[
            jax.ShapeDtypeStruct((n_tok, ns, ret_w + conv_w), BF16),
            jax.ShapeDtypeStruct(state_ret.shape, F32),
            jax.ShapeDtypeStruct(conv_tm.shape, F32),
        ],
        scratch_shapes=[pltpu.VMEM((n_tok, tb, ret_w), F32)] * 4,
        compiler_params=pltpu.CompilerParams(dimension_semantics=("arbitrary",),
                                             vmem_limit_bytes=VMEM_LIMIT_BYTES),
        name="sample_core",
    )(proj, state_ret, conv_tm, cos2.reshape(n_tok, 1, dh), sin2.reshape(n_tok, 1, dh), dm_b, dq_b, dk_b,
      gc_b, gn_gain.reshape(1, -1), dw_w, dw_b.reshape(1, -1), ln_g.reshape(1, -1), ln_b.reshape(1, -1))

    x_new, w_out_bf = pl.pallas_call(
        _sample_out_kernel,
        grid=(d // pc,),
        in_specs=[pl.BlockSpec((n_tok, ns, pc), lambda j: (0, 0, j)),
                  pl.BlockSpec((3, 1, ns, pc), lambda j: (MIX_SUB_LAYER, 0, 0, j)),
                  _resident(mixed.shape),
                  pl.BlockSpec((w_out.shape[0], pc), lambda j: (0, j))],
        out_specs=[pl.BlockSpec((n_tok, ns, pc), lambda j: (0, 0, j)),
                   pl.BlockSpec((w_out.shape[0], pc), lambda j: (0, j))],
        out_shape=[jax.ShapeDtypeStruct(x.shape, F32), jax.ShapeDtypeStruct(w_out.shape, BF16)],
        compiler_params=params,
        name="sample_out",
    )(x, mods, mixed, w_out)
    return x_new, state_new, conv_new, w_in_bf, w_out_bf


def _layer(xp, xs, mods_p, mods_s, sret, sconv, lw, final_gain, n_tok):
    (norm_ffn1, w1g, w1u, w1d, norm_mix, w_in, gn_gain, dw_w, dw_b, ln_g, ln_b, w_out,
     norm_ffn2, w2g, w2u, w2d) = lw
    nb, seq, d = xp.shape
    n_tok, ns, _ = xs.shape
    heads = sret.shape[1]
    mods_p = mods_p.reshape(N_MOD, nb, 1, d)
    mods_s_tm = mods_s.reshape(N_MOD, 1, ns, d)

    xp, xs = _ffn(xp, xs, mods_p, mods_s_tm, 0, norm_ffn1, w1g, w1u, w1d, None)
    xs, ret_s, conv_s_tm, w_in_bf, w_out_bf = _mix_sample(
        xs, mods_s_tm, norm_mix, w_in, gn_gain, dw_w, dw_b, ln_g, ln_b, w_out, sret,
        sconv.transpose(1, 0, 2))
    xp, ret_p, conv_p = _mix_prompt(xp, mods_p, norm_mix, w_in_bf, gn_gain, dw_w, dw_b, ln_g, ln_b,
                                    w_out_bf, heads)
    xp, xs = _ffn(xp, xs, mods_p, mods_s_tm, 2, norm_ffn2, w2g, w2u, w2d, final_gain)
    return xp, xs, ret_p, conv_p, ret_s, conv_s_tm.transpose(1, 0, 2)


def kernel(x_prompt, x_sample, c_prompt, c_sample, state_ret, state_conv, norm_ffn1, ffn1_w_gate,
           ffn1_w_up, ffn1_w_down, norm_mix, w_in, ret_gn_gain, dw_w, dw_b, conv_ln_gain, conv_ln_bias,
           w_out, norm_ffn2, ffn2_w_gate, ffn2_w_up, ffn2_w_down, w_ada, b_ada, norm_final):
    depth = w_in.shape[0]
    nb = x_prompt.shape[0]
    ns, n_tok, d = x_sample.shape
    assert n_tok <= SUBLANES and x_prompt.shape[1] % RET_CHUNK == 0

    xp = x_prompt
    xs = x_sample.transpose(1, 0, 2)

    ret_p, conv_p, ret_s, conv_s = [], [], [], []
    for l in range(depth):
        ada_p, ada_s = _ada(c_prompt, c_sample, w_ada[l], b_ada[l])
        lw = (norm_ffn1[l], ffn1_w_gate[l], ffn1_w_up[l], ffn1_w_down[l], norm_mix[l],
              w_in[l], ret_gn_gain[l], dw_w[l], dw_b[l], conv_ln_gain[l], conv_ln_bias[l],
              w_out[l], norm_ffn2[l], ffn2_w_gate[l], ffn2_w_up[l], ffn2_w_down[l])
        final_gain = norm_final if l == depth - 1 else None
        xp, xs, rp, cp, rs, cs = _layer(xp, xs, ada_p, ada_s, state_ret[l], state_conv[l], lw,
                                        final_gain, n_tok)
        ret_p.append(rp)
        conv_p.append(cp)
        ret_s.append(rs)
        conv_s.append(cs)

    return (xp, xs.transpose(1, 0, 2), jnp.stack(ret_p), jnp.stack(conv_p), jnp.stack(ret_s),
            jnp.stack(conv_s))
```

```python
import functools

import jax
import jax.numpy as jnp
import numpy as np
from jax import lax
from jax.experimental import pallas as pl
from jax.experimental.pallas import tpu as pltpu

F32 = jnp.float32
BF16 = jnp.bfloat16

PAST_LEN = 16384
RET_CHUNK = 256
ROPE_BASE = 10000.0
EPS = 1e-6
N_MOD = 9
MIX_SUB_LAYER = 1

SUBLANES = 8
LANES = 128
VMEM_LIMIT_BYTES = 56 * 1024 * 1024

FFN_TOKEN_TILE = 1024
FFN_FINAL_TOKEN_TILE = 512
MIX_TOKEN_TILE = 512
SAMPLE_MIX_SEQ_TILE = 32
SAMPLE_PROJ_COLS = 512
ADA_ROWS_PER_STEP = 128
FF_CHUNK = 256
FFN_STAGE_STEPS = 8


def _resident(shape):
    n = len(shape)
    return pl.BlockSpec(shape, lambda *_: (0,) * n, pipeline_mode=pl.Buffered(1))


def _rmsnorm(x, gain):
    ms = jnp.mean(x * x, axis=-1, keepdims=True)
    return x * lax.rsqrt(ms + EPS) * gain


def _dot(a, b):
    return jnp.dot(a, b, preferred_element_type=F32)


def _ada_kernel(cp_ref, cs_ref, w_ref, b_ref, op_ref, os_ref):
    k = pl.program_id(0)
    n_prompt = op_ref.shape[1]
    d = op_ref.shape[2]
    c = jnp.concatenate([cp_ref[...], cs_ref[...]], axis=0)
    h = (c * jax.nn.sigmoid(c)).astype(BF16)

    @pl.when(k == 0)
    def _():
        for m in range(op_ref.shape[0]):
            bias = b_ref[:, m * d:(m + 1) * d]
            op_ref[m] = jnp.broadcast_to(bias, op_ref.shape[1:])
            os_ref[m] = jnp.broadcast_to(bias, os_ref.shape[1:])

    for m in range(op_ref.shape[0]):
        part = _dot(h, w_ref[:, m * d:(m + 1) * d].astype(BF16))
        op_ref[m] += part[0:n_prompt]
        os_ref[m] += part[n_prompt:]


def _ada(c_prompt, c_sample, w_ada, b_ada):
    n_prompt, d = c_prompt.shape
    n_sample = c_sample.shape[0]
    tk = ADA_ROWS_PER_STEP
    return pl.pallas_call(
        _ada_kernel,
        grid=(d // tk,),
        in_specs=[
            pl.BlockSpec((n_prompt, tk), lambda k: (0, k)),
            pl.BlockSpec((n_sample, tk), lambda k: (0, k)),
            pl.BlockSpec((tk, N_MOD * d), lambda k: (k, 0)),
            _resident((1, N_MOD * d)),
        ],
        out_specs=[pl.BlockSpec((N_MOD, n_prompt, d), lambda k: (0, 0, 0)),
                   pl.BlockSpec((N_MOD, n_sample, d), lambda k: (0, 0, 0))],
        out_shape=[jax.ShapeDtypeStruct((N_MOD, n_prompt, d), F32),
                   jax.ShapeDtypeStruct((N_MOD, n_sample, d), F32)],
        compiler_params=pltpu.CompilerParams(dimension_semantics=("arbitrary",),
                                             vmem_limit_bytes=VMEM_LIMIT_BYTES),
        name="ada",
    )(c_prompt, c_sample, w_ada, b_ada.reshape(1, -1))


def _ffn_kernel(xp_ref, xs_ref, modp_ref, mods_ref, ng_ref, wg_ref, wu_ref, wd_ref, *rest,
                n_stage, n_prompt, final_norm):
    if final_norm:
        fg_ref, op_ref, os_ref, wg_scr, wu_scr, wd_scr, act_scr, hb_scr = rest
    else:
        op_ref, os_ref, wg_scr, wu_scr, wd_scr, act_scr, hb_scr = rest
    i = pl.program_id(0)
    n_chunks, _, ffc = wg_scr.shape

    @pl.when(i < n_stage)
    def _():
        rows_in, rows_dn = wg_ref.shape[0], wd_ref.shape[0]
        r_in = pl.ds(pl.multiple_of(i * rows_in, rows_in), rows_in)
        for c in range(n_chunks):
            wg_scr[c, r_in, :] = wg_ref[:, c * ffc:(c + 1) * ffc].astype(BF16)
            wu_scr[c, r_in, :] = wu_ref[:, c * ffc:(c + 1) * ffc].astype(BF16)
        wd_scr[pl.ds(pl.multiple_of(i * rows_dn, 2 * SUBLANES), rows_dn), :] = wd_ref[...].astype(BF16)

    def tile(x_ref, mod_ref, o_ref):
        x = x_ref[...]
        g_, r_, d_ = x.shape
        shift, scale, gate = mod_ref[0], mod_ref[1], mod_ref[2]
        h = _rmsnorm(x, ng_ref[...]) * (1.0 + scale) + shift
        rows = g_ * r_
        hb_scr[0:rows, :] = h.astype(BF16).reshape(rows, d_)
        for c in range(n_chunks):
            g = _dot(hb_scr[0:rows, :], wg_scr[c])
            u = _dot(hb_scr[0:rows, :], wu_scr[c])
            act_scr[0:rows, c * ffc:(c + 1) * ffc] = (g * jax.nn.sigmoid(g) * u).astype(BF16)
        y = _dot(act_scr[0:rows, :], wd_scr[...]).reshape(g_, r_, d_)
        out = x + 0.5 * gate * y
        if final_norm:
            out = _rmsnorm(out, fg_ref[...])
        o_ref[...] = out

    @pl.when(jnp.logical_and(i >= n_stage, i < n_stage + n_prompt))
    def _():
        tile(xp_ref, modp_ref, op_ref)

    @pl.when(i == n_stage + n_prompt)
    def _():
        tile(xs_ref, mods_ref, os_ref)


def _ffn(xp, xs, mods_p, mods_s, sub_layer, norm_gain, wg, wu, wd, final_gain):
    nb, seq, d = xp.shape
    d_ff = wg.shape[1]
    final_norm = final_gain is not None
    tm = FFN_FINAL_TOKEN_TILE if final_norm else FFN_TOKEN_TILE
    ffc = FF_CHUNK
    assert seq % tm == 0 and d_ff % ffc == 0 and xs.shape[0] * xs.shape[1] <= tm
    per_seq = seq // tm
    n_stage, n_prompt, n_chunks = FFN_STAGE_STEPS, nb * per_seq, d_ff // ffc
    rows_in, rows_dn = d // n_stage, d_ff // n_stage
    assert rows_in * n_stage == d and rows_dn * n_stage == d_ff
    assert rows_in % (2 * SUBLANES) == 0 and rows_dn % (2 * SUBLANES) == 0

    def prompt_tile(i):
        return jnp.clip(i - n_stage, 0, n_prompt - 1)

    def stage(i):
        return jnp.minimum(i, n_stage - 1)

    x_spec = pl.BlockSpec((1, tm, d), lambda i: (prompt_tile(i) // per_seq, prompt_tile(i) % per_seq, 0))
    in_specs = [
        x_spec,
        _resident(xs.shape),
        pl.BlockSpec((3, 1, 1, d), lambda i: (sub_layer, prompt_tile(i) // per_seq, 0, 0)),
        pl.BlockSpec((3,) + mods_s.shape[1:], lambda i: (sub_layer, 0, 0, 0), pipeline_mode=pl.Buffered(1)),
        _resident((1, d)),
        pl.BlockSpec((rows_in, d_ff), lambda i: (stage(i), 0)),
        pl.BlockSpec((rows_in, d_ff), lambda i: (stage(i), 0)),
        pl.BlockSpec((rows_dn, d), lambda i: (stage(i), 0)),
    ]
    args = [xp, xs, mods_p, mods_s, norm_gain.reshape(1, d), wg, wu, wd]
    if final_norm:
        in_specs.append(_resident((1, d)))
        args.append(final_gain.reshape(1, d))
    return pl.pallas_call(
        functools.partial(_ffn_kernel, n_stage=n_stage, n_prompt=n_prompt, final_norm=final_norm),
        grid=(n_stage + n_prompt + 1,),
        in_specs=in_specs,
        out_specs=[x_spec, pl.BlockSpec(xs.shape, lambda i: (0, 0, 0))],
        out_shape=[jax.ShapeDtypeStruct(xp.shape, F32), jax.ShapeDtypeStruct(xs.shape, F32)],
        scratch_shapes=[
            pltpu.VMEM((n_chunks, d, ffc), BF16),
            pltpu.VMEM((n_chunks, d, ffc), BF16),
            pltpu.VMEM((d_ff, d), BF16),
            pltpu.VMEM((tm, d_ff), BF16),
            pltpu.VMEM((tm, d), BF16),
        ],
        compiler_params=pltpu.CompilerParams(dimension_semantics=("arbitrary",),
                                             vmem_limit_bytes=VMEM_LIMIT_BYTES),
        name="ffn_final" if final_norm else "ffn",
    )(*args)


def _rotate(xh, cos2, sin2):
    return xh * cos2 + pltpu.roll(xh, xh.shape[-1] // 2, axis=xh.ndim - 1) * sin2


def _group_norm_gate(o, gate_pre, gain):
    mu = jnp.mean(o, axis=-1, keepdims=True)
    ctr = o - mu
    var = jnp.mean(ctr * ctr, axis=-1, keepdims=True)
    return gate_pre * jax.nn.sigmoid(gate_pre) * (ctr * lax.rsqrt(var + EPS) * gain)


def _layer_norm_silu(y, gain, bias):
    mu = jnp.mean(y, axis=-1, keepdims=True)
    ctr = y - mu
    var = jnp.mean(ctr * ctr, axis=-1, keepdims=True)
    yn = ctr * lax.rsqrt(var + EPS) * gain + bias
    return yn * jax.nn.sigmoid(yn)


def _conv_taps_slab(win_ref, dww_ref, s, n_out, lead):
    acc = None
    for j in range(dww_ref.shape[0]):
        term = win_ref[s, pl.ds(lead + j, n_out), :] * dww_ref[j:j + 1, s * LANES:(s + 1) * LANES]
        acc = term if acc is None else acc + term
    return acc


def _mix_prompt_kernel(xa_ref, xb_ref, moda_ref, modb_ref, ng_ref, win_ref, gn_ref, dww_ref, dwb_ref,
                       lng_ref, lnb_ref, wout_ref, cos_ref, sin_ref, dmask_ref, dq_ref, dk_ref, gc_ref,
                       o_ref, sout_ref, cout_ref, s_scr, u_scr, hb_scr, conv_scr, mix_scr,
                       *, heads, hist, per_seq, n_tiles):
    i = pl.program_id(0)
    tt, d_ = xa_ref.shape[1], xa_ref.shape[2]
    ret_w = gn_ref.shape[1]
    dh = ret_w // heads
    n_taps, conv_w = dww_ref.shape
    k_scale = dh ** -0.5
    o_a, o_b = 4 * ret_w, 4 * ret_w + conv_w
    n_slabs = u_scr.shape[0]

    slot_a = lax.rem(i, 2)
    slot_b = 1 - slot_a

    def stage_a_project():
        x = xa_ref[0]
        shift, scale = moda_ref[0, 0], moda_ref[1, 0]
        h = _rmsnorm(x, ng_ref[...]) * (1.0 + scale) + shift
        hb = h.astype(BF16)
        hb_scr[slot_a] = hb
        ab = _dot(hb, win_ref[:, o_a:o_b + conv_w])
        u = ab[:, :conv_w] * jax.nn.sigmoid(ab[:, conv_w:])
        for s in range(n_slabs):
            u_scr[s, hist:hist + tt, :] = u[:, s * LANES:(s + 1) * LANES]

    def stage_a_conv_slab(s):
        return _conv_taps_slab(u_scr, dww_ref, s, tt, hist - (n_taps - 1))

    def stage_a_finish(slabs):
        y = jnp.concatenate(slabs, axis=-1) + dwb_ref[...]
        conv_scr[slot_a] = _layer_norm_silu(y, lng_ref[...], lnb_ref[...]).astype(BF16)
        for s in range(n_slabs):
            u_scr[s, 0:hist, :] = u_scr[s, tt:tt + hist, :]

    def stage_b_project():
        return _dot(hb_scr[slot_b], win_ref[:, 0:o_a])

    def stage_b_head(proj, hd):
        lo = hd * dh
        cos2, sin2 = cos_ref[...], sin_ref[...]
        q = _rotate(proj[:, lo:lo + dh], cos2, sin2)
        k = _rotate(proj[:, ret_w + lo:ret_w + lo + dh], cos2, sin2) * k_scale
        v = proj[:, 2 * ret_w + lo:2 * ret_w + lo + dh]
        gate_pre = proj[:, 3 * ret_w + lo:3 * ret_w + lo + dh]
        dmask, dq, dk, gc = dmask_ref[hd], dq_ref[hd], dk_ref[hd], gc_ref[hd]
        state = s_scr[hd]
        outs = []
        for c0 in range(0, tt, RET_CHUNK):
            qc, kc = q[c0:c0 + RET_CHUNK], k[c0:c0 + RET_CHUNK]
            vb = v[c0:c0 + RET_CHUNK].astype(BF16)
            scores = lax.dot_general(qc.astype(BF16), kc.astype(BF16), (((1,), (1,)), ((), ())),
                                     preferred_element_type=F32)
            inner = _dot((scores * dmask).astype(BF16), vb)
            cross = _dot((qc * dq).astype(BF16), state.astype(BF16))
            outs.append(inner + cross)
            state = gc * state + lax.dot_general((kc * dk).astype(BF16), vb, (((0,), (0,)), ((), ())),
                                                 preferred_element_type=F32)
        s_scr[hd] = state
        o = jnp.concatenate(outs, axis=0) if len(outs) > 1 else outs[0]
        mix_scr[:, lo:lo + dh] = _group_norm_gate(o, gate_pre, gn_ref[:, lo:lo + dh]).astype(BF16)

    def stage_b_finish():
        mixed = (_dot(mix_scr[...], wout_ref[0:ret_w, :])
                 + _dot(conv_scr[slot_b], wout_ref[ret_w:ret_w + conv_w, :]))
        o_ref[0] = xb_ref[0] + modb_ref[2, 0] * mixed

    def run(do_a, do_b):
        if do_a:
            stage_a_project()
        if do_b:
            proj = stage_b_project()
            for hd in range(heads):
                stage_b_head(proj, hd)
            stage_b_finish()
        if do_a:
            stage_a_finish([stage_a_conv_slab(s) for s in range(n_slabs)])

    has_a = i < n_tiles
    has_b = i >= 1
    pos_a = lax.rem(i, per_seq)
    pos_b = lax.rem(i + per_seq - 1, per_seq)

    @pl.when(jnp.logical_and(has_a, pos_a == 0))
    def _():
        u_scr[:, 0:hist, :] = jnp.zeros((n_slabs, hist, LANES), F32)

    @pl.when(jnp.logical_and(has_b, pos_b == 0))
    def _():
        s_scr[...] = jnp.zeros_like(s_scr)

    @pl.when(i == 0)
    def _():
        run(True, False)

    @pl.when(jnp.logical_and(has_a, has_b))
    def _():
        run(True, True)

    @pl.when(i == n_tiles)
    def _():
        run(False, True)

    @pl.when(jnp.logical_and(has_a, pos_a == per_seq - 1))
    def _():
        for s in range(n_slabs):
            cout_ref[0, :, s * LANES:(s + 1) * LANES] = u_scr[s, hist - (n_taps - 1):hist, :]

    @pl.when(jnp.logical_and(has_b, pos_b == per_seq - 1))
    def _():
        sout_ref[0] = s_scr[...]


def _decay_tables(heads, chunk):
    lg = np.log(1.0 - 2.0 ** (-5.0 - np.arange(heads, dtype=np.float64)))
    idx = np.arange(chunk, dtype=np.float64)
    diff = idx[:, None] - idx[None, :]
    dmask = np.where(diff[None] >= 0, np.exp(lg[:, None, None] * np.maximum(diff, 0.0)[None]), 0.0)
    dq = np.exp(lg[:, None] * (idx[None, :] + 1.0))
    dk = np.exp(lg[:, None] * (chunk - 1.0 - idx[None, :]))
    gc = np.exp(lg * chunk)
    return dmask, dq, dk, gc


def _rotary_tables(pos0, n_pos, half):
    inv = ROPE_BASE ** (-np.arange(half, dtype=np.float64) / half)
    pos = (pos0 + np.arange(n_pos)).astype(np.float64)
    ang = pos[:, None] * inv[None, :]
    cos, sin = np.cos(ang), np.sin(ang)
    return np.concatenate([cos, cos], axis=-1), np.concatenate([-sin, sin], axis=-1)


def _const(a, shape=None):
    a = np.asarray(a, dtype=np.float32)
    if shape is not None:
        a = np.ascontiguousarray(np.broadcast_to(a, shape))
    return jnp.asarray(a)


def _mix_prompt(x, mods, norm_gain, w_in, gn_gain, dw_w, dw_b, ln_g, ln_b, w_out, heads):
    nb, seq, d = x.shape
    tt = MIX_TOKEN_TILE
    ret_w = gn_gain.shape[0]
    dh = ret_w // heads
    n_taps, conv_w = dw_w.shape
    hist = -(-(n_taps - 1) // SUBLANES) * SUBLANES
    chunk = RET_CHUNK
    cos2, sin2 = (_const(t) for t in _rotary_tables(0, seq, dh // 2))
    dmask, dq, dk, gc = _decay_tables(heads, chunk)
    dmask = _const(dmask)
    dq_b = _const(dq[:, :, None], (heads, chunk, dh))
    dk_b = _const(dk[:, :, None], (heads, chunk, dh))
    gc_b = _const(gc[:, None, None], (heads, 1, dh))
    per_seq = seq // tt
    n_tiles = nb * per_seq

    def tile_a(i):
        return jnp.minimum(i, n_tiles - 1)

    def tile_b(i):
        return jnp.maximum(i - 1, 0)

    kern = functools.partial(_mix_prompt_kernel, heads=heads, hist=hist, per_seq=per_seq, n_tiles=n_tiles)
    return pl.pallas_call(
        kern,
        grid=(n_tiles + 1,),
        in_specs=[
            pl.BlockSpec((1, tt, d), lambda i: (tile_a(i) // per_seq, tile_a(i) % per_seq, 0)),
            pl.BlockSpec((1, tt, d), lambda i: (tile_b(i) // per_seq, tile_b(i) % per_seq, 0)),
            pl.BlockSpec((3, 1, 1, d), lambda i: (MIX_SUB_LAYER, tile_a(i) // per_seq, 0, 0)),
            pl.BlockSpec((3, 1, 1, d), lambda i: (MIX_SUB_LAYER, tile_b(i) // per_seq, 0, 0)),
            _resident((1, d)),
            _resident(w_in.shape),
            _resident((1, ret_w)),
            _resident((n_taps, conv_w)),
            _resident((1, conv_w)),
            _resident((1, conv_w)),
            _resident((1, conv_w)),
            _resident(w_out.shape),
            pl.BlockSpec((tt, dh), lambda i: (tile_b(i) % per_seq, 0)),
            pl.BlockSpec((tt, dh), lambda i: (tile_b(i) % per_seq, 0)),
            _resident((heads, chunk, chunk)),
            _resident((heads, chunk, dh)),
            _resident((heads, chunk, dh)),
            _resident((heads, 1, dh)),
        ],
        out_specs=[
            pl.BlockSpec((1, tt, d), lambda i: (tile_b(i) // per_seq, tile_b(i) % per_seq, 0)),
            pl.BlockSpec((1, heads, dh, dh), lambda i: (tile_b(i) // per_seq, 0, 0, 0)),
            pl.BlockSpec((1, n_taps - 1, conv_w), lambda i: (tile_a(i) // per_seq, 0, 0)),
        ],
        out_shape=[
            jax.ShapeDtypeStruct(x.shape, F32),
            jax.ShapeDtypeStruct((nb, heads, dh, dh), F32),
            jax.ShapeDtypeStruct((nb, n_taps - 1, conv_w), F32),
        ],
        scratch_shapes=[
            pltpu.VMEM((heads, dh, dh), F32),
            pltpu.VMEM((conv_w // LANES, hist + tt, LANES), F32),
            pltpu.VMEM((2, tt, d), BF16),
            pltpu.VMEM((2, tt, conv_w), BF16),
            pltpu.VMEM((tt, ret_w), BF16),
        ],
        compiler_params=pltpu.CompilerParams(dimension_semantics=("arbitrary",),
                                             vmem_limit_bytes=VMEM_LIMIT_BYTES),
        name="mix_prompt",
    )(x, x, mods, mods, norm_gain.reshape(1, d), w_in, gn_gain.reshape(1, -1), dw_w, dw_b.reshape(1, -1),
      ln_g.reshape(1, -1), ln_b.reshape(1, -1), w_out, cos2, sin2, dmask, dq_b, dk_b, gc_b)


def _sample_proj_kernel(x_ref, mod_ref, ng_ref, win_ref, o_ref, wbf_ref, hb_scr):
    n_tok, ns, d_ = x_ref.shape

    @pl.when(pl.program_id(0) == 0)
    def _():
        h = _rmsnorm(x_ref[...], ng_ref[...]) * (1.0 + mod_ref[1]) + mod_ref[0]
        hb_scr[...] = h.astype(BF16).reshape(n_tok * ns, d_)

    w = win_ref[...].astype(BF16)
    wbf_ref[...] = w
    o_ref[...] = _dot(hb_scr[...], w).reshape(n_tok, ns, -1)


def _sample_out_kernel(x_ref, mod_ref, mixed_ref, wout_ref, o_ref, wbf_ref):
    n_tok, ns, _ = x_ref.shape
    w = wout_ref[...].astype(BF16)
    wbf_ref[...] = w
    y = _dot(mixed_ref[...].reshape(n_tok * ns, -1), w).reshape(n_tok, ns, -1)
    o_ref[...] = x_ref[...] + mod_ref[2] * y


def _sample_core_kernel(proj_ref, state_ref, cin_ref, cos_ref, sin_ref, dm_ref, dq_ref, dk_ref, gc_ref,
                        gn_ref, dww_ref, dwb_ref, lng_ref, lnb_ref, mixed_ref, sout_ref, cout_ref,
                        qd_scr, kd_scr, v_scr, cross_scr, *, heads):
    n_tok, tb, _ = proj_ref.shape
    ret_w = gn_ref.shape[1]
    dh = ret_w // heads
    n_taps, conv_w = dww_ref.shape
    n_buf = n_taps - 1
    o_a, o_b = 4 * ret_w, 4 * ret_w + conv_w
    k_scale = dh ** -0.5
    cos2, sin2 = cos_ref[...], sin_ref[...]

    inner_heads = []
    for hd in range(heads):
        lo = hd * dh
        q = _rotate(proj_ref[:, :, lo:lo + dh], cos2, sin2)
        k = _rotate(proj_ref[:, :, ret_w + lo:ret_w + lo + dh], cos2, sin2) * k_scale
        v = proj_ref[:, :, 2 * ret_w + lo:2 * ret_w + lo + dh]
        qd_scr[:, :, lo:lo + dh] = q * dq_ref[hd]
        kd_scr[:, :, lo:lo + dh] = k * dk_ref[hd]
        v_scr[:, :, lo:lo + dh] = v
        rows_out = []
        for i in range(n_tok):
            acc = None
            for j in range(i + 1):
                s_ij = jnp.sum(q[i] * k[j], axis=-1, keepdims=True)
                term = (s_ij * dm_ref[hd, i, j]) * v[j]
                acc = term if acc is None else acc + term
            rows_out.append(acc)
        inner_heads.append(jnp.stack(rows_out, axis=0))

    row = lax.broadcasted_iota(jnp.int32, (n_tok, SUBLANES, dh), 1)

    def group(g, carry):
        rows = pl.ds(pl.multiple_of(g * SUBLANES, SUBLANES), SUBLANES)
        for hd in range(heads):
            lo = hd * dh
            q_tile = qd_scr[:, rows, lo:lo + dh].reshape(n_tok * SUBLANES, dh).astype(BF16)
            k_tile = kd_scr[:, rows, lo:lo + dh].reshape(n_tok * SUBLANES, dh).astype(BF16)
            v_group = v_scr[:, rows, lo:lo + dh]
            cross = jnp.zeros((n_tok, SUBLANES, dh), F32)
            for r in range(SUBLANES):
                b = g * SUBLANES + r
                state = state_ref[b, hd]
                out = _dot(q_tile, state.astype(BF16)).reshape(n_tok, SUBLANES, dh)
                cross = jnp.where(row == r, out, cross)
                v_own = jnp.where(row == r, v_group, 0.0).reshape(n_tok * SUBLANES, dh).astype(BF16)
                upd = lax.dot_general(k_tile, v_own, (((0,), (0,)), ((), ())), preferred_element_type=F32)
                sout_ref[b, hd] = gc_ref[hd] * state + upd
            cross_scr[:, rows, lo:lo + dh] = cross
        return carry

    lax.fori_loop(0, tb // SUBLANES, group, 0)

    for hd in range(heads):
        lo = hd * dh
        o = inner_heads[hd] + cross_scr[:, :, lo:lo + dh]
        gate_pre = proj_ref[:, :, 3 * ret_w + lo:3 * ret_w + lo + dh]
        mixed_ref[:, :, lo:lo + dh] = _group_norm_gate(o, gate_pre, gn_ref[:, lo:lo + dh]).astype(BF16)

    u = proj_ref[:, :, o_a:o_a + conv_w] * jax.nn.sigmoid(proj_ref[:, :, o_b:o_b + conv_w])

    def window(s):
        return cin_ref[s] if s < n_buf else u[s - n_buf]

    for t in range(n_tok):
        acc = None
        for j in range(n_taps):
            term = window(t + j) * dww_ref[j:j + 1, :]
            acc = term if acc is None else acc + term
        y = acc + dwb_ref[...]
        mixed_ref[t, :, ret_w:ret_w + conv_w] = _layer_norm_silu(y, lng_ref[...], lnb_ref[...]).astype(BF16)
    for s in range(n_buf):
        cout_ref[s] = window(s + n_tok)


def _mix_sample(x, mods, norm_gain, w_in, gn_gain, dw_w, dw_b, ln_g, ln_b, w_out, state_ret, conv_tm):
    n_tok, ns, d = x.shape
    tb = SAMPLE_MIX_SEQ_TILE
    heads, dh = state_ret.shape[1], state_ret.shape[2]
    ret_w = heads * dh
    n_taps, conv_w = dw_w.shape
    n_buf = n_taps - 1
    n_cols = w_in.shape[1]
    pc = SAMPLE_PROJ_COLS
    params = pltpu.CompilerParams(dimension_semantics=("arbitrary",), vmem_limit_bytes=VMEM_LIMIT_BYTES)

    proj, w_in_bf = pl.pallas_call(
        _sample_proj_kernel,
        grid=(n_cols // pc,),
        in_specs=[_resident(x.shape),
                  pl.BlockSpec((3, 1, ns, d), lambda j: (MIX_SUB_LAYER, 0, 0, 0), pipeline_mode=pl.Buffered(1)),
                  _resident((1, d)),
                  pl.BlockSpec((d, pc), lambda j: (0, j))],
        out_specs=[pl.BlockSpec((n_tok, ns, pc), lambda j: (0, 0, j)),
                   pl.BlockSpec((d, pc), lambda j: (0, j))],
        out_shape=[jax.ShapeDtypeStruct((n_tok, ns, n_cols), F32),
                   jax.ShapeDtypeStruct(w_in.shape, BF16)],
        scratch_shapes=[pltpu.VMEM((n_tok * ns, d), BF16)],
        compiler_params=params,
        name="sample_proj",
    )(x, mods, norm_gain.reshape(1, d), w_in)

    cos2, sin2 = (_const(t) for t in _rotary_tables(PAST_LEN, n_tok, dh // 2))
    dmask, dq, dk, gc = _decay_tables(heads, n_tok)
    dm_b = _const(dmask[:, :, :, None, None], (heads, n_tok, n_tok, 1, dh))
    dq_b = _const(dq[:, :, None, None], (heads, n_tok, 1, dh))
    dk_b = _const(dk[:, :, None, None], (heads, n_tok, 1, dh))
    gc_b = _const(gc[:, None, None], (heads, 1, dh))
    mixed, state_new, conv_new = pl.pallas_call(
        functools.partial(_sample_core_kernel, heads=heads),
        grid=(ns // tb,),
        in_specs=[
            pl.BlockSpec((n_tok, tb, n_cols), lambda i: (0, i, 0)),
            pl.BlockSpec((tb, heads, dh, dh), lambda i: (i, 0, 0, 0)),
            pl.BlockSpec((n_buf, tb, conv_w), lambda i: (0, i, 0)),
            _resident((n_tok, 1, dh)),
            _resident((n_tok, 1, dh)),
            _resident((heads, n_tok, n_tok, 1, dh)),
            _resident((heads, n_tok, 1, dh)),
            _resident((heads, n_tok, 1, dh)),
            _resident((heads, 1, dh)),
            _resident((1, ret_w)),
            _resident((n_taps, conv_w)),
            _resident((1, conv_w)),
            _resident((1, conv_w)),
            _resident((1, conv_w)),
        ],
        out_specs=[
            pl.BlockSpec((n_tok, tb, ret_w + conv_w), lambda i: (0, i, 0)),
            pl.BlockSpec((tb, heads, dh, dh), lambda i: (i, 0, 0, 0)),
            pl.BlockSpec((n_buf, tb, conv_w), lambda i: (0, i, 0)),
        ],
        out_shape=[
            jax.ShapeDtypeStruct((n_tok, ns, ret_w + conv_w), BF16),
            jax.ShapeDtypeStruct(state_ret.shape, F32),
            jax.ShapeDtypeStruct(conv_tm.shape, F32),
        ],
        scratch_shapes=[pltpu.VMEM((n_tok, tb, ret_w), F32)] * 4,
        compiler_params=pltpu.CompilerParams(dimension_semantics=("arbitrary",),
                                             vmem_limit_bytes=VMEM_LIMIT_BYTES),
        name="sample_core",
    )(proj, state_ret, conv_tm, cos2.reshape(n_tok, 1, dh), sin2.reshape(n_tok, 1, dh), dm_b, dq_b, dk_b,
      gc_b, gn_gain.reshape(1, -1), dw_w, dw_b.reshape(1, -1), ln_g.reshape(1, -1), ln_b.reshape(1, -1))

    x_new, w_out_bf = pl.pallas_call(
        _sample_out_kernel,
        grid=(d // pc,),
        in_specs=[pl.BlockSpec((n_tok, ns, pc), lambda j: (0, 0, j)),
                  pl.BlockSpec((3, 1, ns, pc), lambda j: (MIX_SUB_LAYER, 0, 0, j)),
                  _resident(mixed.shape),
                  pl.BlockSpec((w_out.shape[0], pc), lambda j: (0, j))],
        out_specs=[pl.BlockSpec((n_tok, ns, pc), lambda j: (0, 0, j)),
                   pl.BlockSpec((w_out.shape[0], pc), lambda j: (0, j))],
        out_shape=[jax.ShapeDtypeStruct(x.shape, F32), jax.ShapeDtypeStruct(w_out.shape, BF16)],
        compiler_params=params,
        name="sample_out",
    )(x, mods, mixed, w_out)
    return x_new, state_new, conv_new, w_in_bf, w_out_bf


def _layer(xp, xs, mods_p, mods_s, sret, sconv, lw, final_gain, n_tok):
    (norm_ffn1, w1g, w1u, w1d, norm_mix, w_in, gn_gain, dw_w, dw_b, ln_g, ln_b, w_out,
     norm_ffn2, w2g, w2u, w2d) = lw
    nb, seq, d = xp.shape
    n_tok, ns, _ = xs.shape
    heads = sret.shape[1]
    mods_p = mods_p.reshape(N_MOD, nb, 1, d)
    mods_s_tm = mods_s.reshape(N_MOD, 1, ns, d)

    xp, xs = _ffn(xp, xs, mods_p, mods_s_tm, 0, norm_ffn1, w1g, w1u, w1d, None)
    xs, ret_s, conv_s_tm, w_in_bf, w_out_bf = _mix_sample(
        xs, mods_s_tm, norm_mix, w_in, gn_gain, dw_w, dw_b, ln_g, ln_b, w_out, sret,
        sconv.transpose(1, 0, 2))
    xp, ret_p, conv_p = _mix_prompt(xp, mods_p, norm_mix, w_in_bf, gn_gain, dw_w, dw_b, ln_g, ln_b,
                                    w_out_bf, heads)
    xp, xs = _ffn(xp, xs, mods_p, mods_s_tm, 2, norm_ffn2, w2g, w2u, w2d, final_gain)
    return xp, xs, ret_p, conv_p, ret_s, conv_s_tm.transpose(1, 0, 2)


def kernel(x_prompt, x_sample, c_prompt, c_sample, state_ret, state_conv, norm_ffn1, ffn1_w_gate,
           ffn1_w_up, ffn1_w_down, norm_mix, w_in, ret_gn_gain, dw_w, dw_b, conv_ln_gain, conv_ln_bias,
           w_out, norm_ffn2, ffn2_w_gate, ffn2_w_up, ffn2_w_down, w_ada, b_ada, norm_final):
    depth = w_in.shape[0]
    nb = x_prompt.shape[0]
    ns, n_tok, d = x_sample.shape
    assert n_tok <= SUBLANES and x_prompt.shape[1] % RET_CHUNK == 0

    xp = x_prompt
    xs = x_sample.transpose(1, 0, 2)

    ret_p, conv_p, ret_s, conv_s = [], [], [], []
    for l in range(depth):
        ada_p, ada_s = _ada(c_prompt, c_sample, w_ada[l], b_ada[l])
        lw = (norm_ffn1[l], ffn1_w_gate[l], ffn1_w_up[l], ffn1_w_down[l], norm_mix[l],
              w_in[l], ret_gn_gain[l], dw_w[l], dw_b[l], conv_ln_gain[l], conv_ln_bias[l],
              w_out[l], norm_ffn2[l], ffn2_w_gate[l], ffn2_w_up[l], ffn2_w_down[l])
        final_gain = norm_final if l == depth - 1 else None
        xp, xs, rp, cp, rs, cs = _layer(xp, xs, ada_p, ada_s, state_ret[l], state_conv[l], lw,
                                        final_gain, n_tok)
        ret_p.append(rp)
        conv_p.append(cp)
        ret_s.append(rs)
        conv_s.append(cs)

    return (xp, xs.transpose(1, 0, 2), jnp.stack(ret_p), jnp.stack(conv_p), jnp.stack(ret_s),
            jnp.stack(conv_s))
```

```python
import functools

import jax
import jax.numpy as jnp
import numpy as np
from jax import lax
from jax.experimental import pallas as pl
from jax.experimental.pallas import tpu as pltpu

F32 = jnp.float32
BF16 = jnp.bfloat16

PAST_LEN = 16384
RET_CHUNK = 256
ROPE_BASE = 10000.0
EPS = 1e-6
N_MOD = 9
MIX_SUB_LAYER = 1

SUBLANES = 8
LANES = 128
VMEM_LIMIT_BYTES = 56 * 1024 * 1024
SMALL_CALL_VMEM_LIMIT_BYTES = 32 * 1024 * 1024

FFN_TOKEN_TILE = 1024
FFN_FINAL_TOKEN_TILE = 512
MIX_TOKEN_TILE = 512
SAMPLE_MIX_SEQ_TILE = 32
SAMPLE_PROJ_COLS = 512
ADA_ROWS_PER_STEP = 128
FF_CHUNK = 256
FFN_STAGE_STEPS = 8


def _resident(shape):
    n = len(shape)
    return pl.BlockSpec(shape, lambda *_: (0,) * n, pipeline_mode=pl.Buffered(1))


def _rmsnorm(x, gain):
    ms = jnp.mean(x * x, axis=-1, keepdims=True)
    return x * lax.rsqrt(ms + EPS) * gain


def _dot(a, b):
    return jnp.dot(a, b, preferred_element_type=F32)


def _ada_kernel(cp_ref, cs_ref, w_ref, b_ref, op_ref, os_ref):
    k = pl.program_id(0)
    n_prompt = op_ref.shape[1]
    d = op_ref.shape[2]
    c = jnp.concatenate([cp_ref[...], cs_ref[...]], axis=0)
    h = (c * jax.nn.sigmoid(c)).astype(BF16)

    @pl.when(k == 0)
    def _():
        for m in range(op_ref.shape[0]):
            bias = b_ref[:, m * d:(m + 1) * d]
            op_ref[m] = jnp.broadcast_to(bias, op_ref.shape[1:])
            os_ref[m] = jnp.broadcast_to(bias, os_ref.shape[1:])

    for m in range(op_ref.shape[0]):
        part = _dot(h, w_ref[:, m * d:(m + 1) * d].astype(BF16))
        op_ref[m] += part[0:n_prompt]
        os_ref[m] += part[n_prompt:]


def _ada(c_prompt, c_sample, w_ada, b_ada):
    n_prompt, d = c_prompt.shape
    n_sample = c_sample.shape[0]
    tk = ADA_ROWS_PER_STEP
    return pl.pallas_call(
        _ada_kernel,
        grid=(d // tk,),
        in_specs=[
            pl.BlockSpec((n_prompt, tk), lambda k: (0, k)),
            pl.BlockSpec((n_sample, tk), lambda k: (0, k)),
            pl.BlockSpec((tk, N_MOD * d), lambda k: (k, 0)),
            _resident((1, N_MOD * d)),
        ],
        out_specs=[pl.BlockSpec((N_MOD, n_prompt, d), lambda k: (0, 0, 0)),
                   pl.BlockSpec((N_MOD, n_sample, d), lambda k: (0, 0, 0))],
        out_shape=[jax.ShapeDtypeStruct((N_MOD, n_prompt, d), F32),
                   jax.ShapeDtypeStruct((N_MOD, n_sample, d), F32)],
        compiler_params=pltpu.CompilerParams(dimension_semantics=("arbitrary",),
                                             vmem_limit_bytes=SMALL_CALL_VMEM_LIMIT_BYTES),
        name="ada",
    )(c_prompt, c_sample, w_ada, b_ada.reshape(1, -1))


def _ffn_kernel(xp_ref, xs_ref, modp_ref, mods_ref, ng_ref, wg_ref, wu_ref, wd_ref, *rest,
                n_stage, n_prompt, final_norm):
    if final_norm:
        fg_ref, op_ref, os_ref, wg_scr, wu_scr, wd_scr, act_scr, hb_scr = rest
    else:
        op_ref, os_ref, wg_scr, wu_scr, wd_scr, act_scr, hb_scr = rest
    i = pl.program_id(0)
    n_chunks, _, ffc = wg_scr.shape

    @pl.when(i < n_stage)
    def _():
        rows_in, rows_dn = wg_ref.shape[0], wd_ref.shape[0]
        r_in = pl.ds(pl.multiple_of(i * rows_in, rows_in), rows_in)
        for c in range(n_chunks):
            wg_scr[c, r_in, :] = wg_ref[:, c * ffc:(c + 1) * ffc].astype(BF16)
            wu_scr[c, r_in, :] = wu_ref[:, c * ffc:(c + 1) * ffc].astype(BF16)
        wd_scr[pl.ds(pl.multiple_of(i * rows_dn, 2 * SUBLANES), rows_dn), :] = wd_ref[...].astype(BF16)

    def tile(x_ref, mod_ref, o_ref):
        x = x_ref[...]
        g_, r_, d_ = x.shape
        shift, scale, gate = mod_ref[0], mod_ref[1], mod_ref[2]
        h = _rmsnorm(x, ng_ref[...]) * (1.0 + scale) + shift
        rows = g_ * r_
        hb_scr[0:rows, :] = h.astype(BF16).reshape(rows, d_)
        for c in range(n_chunks):
            g = _dot(hb_scr[0:rows, :], wg_scr[c])
            u = _dot(hb_scr[0:rows, :], wu_scr[c])
            act_scr[0:rows, c * ffc:(c + 1) * ffc] = (g * jax.nn.sigmoid(g) * u).astype(BF16)
        y = _dot(act_scr[0:rows, :], wd_scr[...]).reshape(g_, r_, d_)
        out = x + 0.5 * gate * y
        if final_norm:
            out = _rmsnorm(out, fg_ref[...])
        o_ref[...] = out

    @pl.when(jnp.logical_and(i >= n_stage, i < n_stage + n_prompt))
    def _():
        tile(xp_ref, modp_ref, op_ref)

    @pl.when(i == n_stage + n_prompt)
    def _():
        tile(xs_ref, mods_ref, os_ref)


def _ffn(xp, xs, mods_p, mods_s, sub_layer, norm_gain, wg, wu, wd, final_gain):
    nb, seq, d = xp.shape
    d_ff = wg.shape[1]
    final_norm = final_gain is not None
    tm = FFN_FINAL_TOKEN_TILE if final_norm else FFN_TOKEN_TILE
    ffc = FF_CHUNK
    assert seq % tm == 0 and d_ff % ffc == 0 and xs.shape[0] * xs.shape[1] <= tm
    per_seq = seq // tm
    n_stage, n_prompt, n_chunks = FFN_STAGE_STEPS, nb * per_seq, d_ff // ffc
    rows_in, rows_dn = d // n_stage, d_ff // n_stage
    assert rows_in * n_stage == d and rows_dn * n_stage == d_ff
    assert rows_in % (2 * SUBLANES) == 0 and rows_dn % (2 * SUBLANES) == 0

    def prompt_tile(i):
        return jnp.clip(i - n_stage, 0, n_prompt - 1)

    def stage(i):
        return jnp.minimum(i, n_stage - 1)

    x_spec = pl.BlockSpec((1, tm, d), lambda i: (prompt_tile(i) // per_seq, prompt_tile(i) % per_seq, 0))
    in_specs = [
        x_spec,
        _resident(xs.shape),
        pl.BlockSpec((3, 1, 1, d), lambda i: (sub_layer, prompt_tile(i) // per_seq, 0, 0)),
        pl.BlockSpec((3,) + mods_s.shape[1:], lambda i: (sub_layer, 0, 0, 0), pipeline_mode=pl.Buffered(1)),
        _resident((1, d)),
        pl.BlockSpec((rows_in, d_ff), lambda i: (stage(i), 0)),
        pl.BlockSpec((rows_in, d_ff), lambda i: (stage(i), 0)),
        pl.BlockSpec((rows_dn, d), lambda i: (stage(i), 0)),
    ]
    args = [xp, xs, mods_p, mods_s, norm_gain.reshape(1, d), wg, wu, wd]
    if final_norm:
        in_specs.append(_resident((1, d)))
        args.append(final_gain.reshape(1, d))
    return pl.pallas_call(
        functools.partial(_ffn_kernel, n_stage=n_stage, n_prompt=n_prompt, final_norm=final_norm),
        grid=(n_stage + n_prompt + 1,),
        in_specs=in_specs,
        out_specs=[x_spec, pl.BlockSpec(xs.shape, lambda i: (0, 0, 0))],
        out_shape=[jax.ShapeDtypeStruct(xp.shape, F32), jax.ShapeDtypeStruct(xs.shape, F32)],
        scratch_shapes=[
            pltpu.VMEM((n_chunks, d, ffc), BF16),
            pltpu.VMEM((n_chunks, d, ffc), BF16),
            pltpu.VMEM((d_ff, d), BF16),
            pltpu.VMEM((tm, d_ff), BF16),
            pltpu.VMEM((tm, d), BF16),
        ],
        compiler_params=pltpu.CompilerParams(dimension_semantics=("arbitrary",),
                                             vmem_limit_bytes=VMEM_LIMIT_BYTES),
        name="ffn_final" if final_norm else "ffn",
    )(*args)


def _rotate(xh, cos2, sin2):
    return xh * cos2 + pltpu.roll(xh, xh.shape[-1] // 2, axis=xh.ndim - 1) * sin2


def _group_norm_gate(o, gate_pre, gain):
    mu = jnp.mean(o, axis=-1, keepdims=True)
    ctr = o - mu
    var = jnp.mean(ctr * ctr, axis=-1, keepdims=True)
    return gate_pre * jax.nn.sigmoid(gate_pre) * (ctr * lax.rsqrt(var + EPS) * gain)


def _layer_norm_silu(y, gain, bias):
    mu = jnp.mean(y, axis=-1, keepdims=True)
    ctr = y - mu
    var = jnp.mean(ctr * ctr, axis=-1, keepdims=True)
    yn = ctr * lax.rsqrt(var + EPS) * gain + bias
    return yn * jax.nn.sigmoid(yn)


def _conv_taps_slab(win_ref, dww_ref, s, n_out, lead):
    acc = None
    for j in range(dww_ref.shape[0]):
        term = win_ref[s, pl.ds(lead + j, n_out), :] * dww_ref[j:j + 1, s * LANES:(s + 1) * LANES]
        acc = term if acc is None else acc + term
    return acc


def _mix_prompt_kernel(xa_ref, xb_ref, moda_ref, modb_ref, ng_ref, win_ref, gn_ref, dww_ref, dwb_ref,
                       lng_ref, lnb_ref, wout_ref, cos_ref, sin_ref, dmask_ref, dq_ref, dk_ref, gc_ref,
                       o_ref, sout_ref, cout_ref, s_scr, u_scr, hb_scr, conv_scr, mix_scr,
                       *, heads, hist, per_seq, n_tiles):
    i = pl.program_id(0)
    tt, d_ = xa_ref.shape[1], xa_ref.shape[2]
    ret_w = gn_ref.shape[1]
    dh = ret_w // heads
    n_taps, conv_w = dww_ref.shape
    k_scale = dh ** -0.5
    o_a, o_b = 4 * ret_w, 4 * ret_w + conv_w
    n_slabs = u_scr.shape[0]

    slot_a = lax.rem(i, 2)
    slot_b = 1 - slot_a

    def stage_a_project():
        x = xa_ref[0]
        shift, scale = moda_ref[0, 0], moda_ref[1, 0]
        h = _rmsnorm(x, ng_ref[...]) * (1.0 + scale) + shift
        hb = h.astype(BF16)
        hb_scr[slot_a] = hb
        ab = _dot(hb, win_ref[:, o_a:o_b + conv_w])
        u = ab[:, :conv_w] * jax.nn.sigmoid(ab[:, conv_w:])
        for s in range(n_slabs):
            u_scr[s, hist:hist + tt, :] = u[:, s * LANES:(s + 1) * LANES]

    def stage_a_conv_slab(s):
        return _conv_taps_slab(u_scr, dww_ref, s, tt, hist - (n_taps - 1))

    def stage_a_finish(slabs):
        y = jnp.concatenate(slabs, axis=-1) + dwb_ref[...]
        conv_scr[slot_a] = _layer_norm_silu(y, lng_ref[...], lnb_ref[...]).astype(BF16)
        for s in range(n_slabs):
            u_scr[s, 0:hist, :] = u_scr[s, tt:tt + hist, :]

    def stage_b_project():
        return _dot(hb_scr[slot_b], win_ref[:, 0:o_a])

    def stage_b_head(proj, hd):
        lo = hd * dh
        cos2, sin2 = cos_ref[...], sin_ref[...]
        q = _rotate(proj[:, lo:lo + dh], cos2, sin2)
        k = _rotate(proj[:, ret_w + lo:ret_w + lo + dh], cos2, sin2) * k_scale
        v = proj[:, 2 * ret_w + lo:2 * ret_w + lo + dh]
        gate_pre = proj[:, 3 * ret_w + lo:3 * ret_w + lo + dh]
        dmask, dq, dk, gc = dmask_ref[hd], dq_ref[hd], dk_ref[hd], gc_ref[hd]
        state = s_scr[hd]
        outs = []
        for c0 in range(0, tt, RET_CHUNK):
            qc, kc = q[c0:c0 + RET_CHUNK], k[c0:c0 + RET_CHUNK]
            vb = v[c0:c0 + RET_CHUNK].astype(BF16)
            scores = lax.dot_general(qc.astype(BF16), kc.astype(BF16), (((1,), (1,)), ((), ())),
                                     preferred_element_type=F32)
            inner = _dot((scores * dmask).astype(BF16), vb)
            cross = _dot((qc * dq).astype(BF16), state.astype(BF16))
            outs.append(inner + cross)
            state = gc * state + lax.dot_general((kc * dk).astype(BF16), vb, (((0,), (0,)), ((), ())),
                                                 preferred_element_type=F32)
        s_scr[hd] = state
        o = jnp.concatenate(outs, axis=0) if len(outs) > 1 else outs[0]
        mix_scr[:, lo:lo + dh] = _group_norm_gate(o, gate_pre, gn_ref[:, lo:lo + dh]).astype(BF16)

    def stage_b_finish():
        mixed = (_dot(mix_scr[...], wout_ref[0:ret_w, :])
                 + _dot(conv_scr[slot_b], wout_ref[ret_w:ret_w + conv_w, :]))
        o_ref[0] = xb_ref[0] + modb_ref[2, 0] * mixed

    def run(do_a, do_b):
        if do_a:
            stage_a_project()
        if do_b:
            proj = stage_b_project()
            for hd in range(heads):
                stage_b_head(proj, hd)
            stage_b_finish()
        if do_a:
            stage_a_finish([stage_a_conv_slab(s) for s in range(n_slabs)])

    has_a = i < n_tiles
    has_b = i >= 1
    pos_a = lax.rem(i, per_seq)
    pos_b = lax.rem(i + per_seq - 1, per_seq)

    @pl.when(jnp.logical_and(has_a, pos_a == 0))
    def _():
        u_scr[:, 0:hist, :] = jnp.zeros((n_slabs, hist, LANES), F32)

    @pl.when(jnp.logical_and(has_b, pos_b == 0))
    def _():
        s_scr[...] = jnp.zeros_like(s_scr)

    @pl.when(i == 0)
    def _():
        run(True, False)

    @pl.when(jnp.logical_and(has_a, has_b))
    def _():
        run(True, True)

    @pl.when(i == n_tiles)
    def _():
        run(False, True)

    @pl.when(jnp.logical_and(has_a, pos_a == per_seq - 1))
    def _():
        for s in range(n_slabs):
            cout_ref[0, :, s * LANES:(s + 1) * LANES] = u_scr[s, hist - (n_taps - 1):hist, :]

    @pl.when(jnp.logical_and(has_b, pos_b == per_seq - 1))
    def _():
        sout_ref[0] = s_scr[...]


def _decay_tables(heads, chunk):
    lg = np.log(1.0 - 2.0 ** (-5.0 - np.arange(heads, dtype=np.float64)))
    idx = np.arange(chunk, dtype=np.float64)
    diff = idx[:, None] - idx[None, :]
    dmask = np.where(diff[None] >= 0, np.exp(lg[:, None, None] * np.maximum(diff, 0.0)[None]), 0.0)
    dq = np.exp(lg[:, None] * (idx[None, :] + 1.0))
    dk = np.exp(lg[:, None] * (chunk - 1.0 - idx[None, :]))
    gc = np.exp(lg * chunk)
    return dmask, dq, dk, gc


def _rotary_tables(pos0, n_pos, half):
    inv = ROPE_BASE ** (-np.arange(half, dtype=np.float64) / half)
    pos = (pos0 + np.arange(n_pos)).astype(np.float64)
    ang = pos[:, None] * inv[None, :]
    cos, sin = np.cos(ang), np.sin(ang)
    return np.concatenate([cos, cos], axis=-1), np.concatenate([-sin, sin], axis=-1)


def _const(a, shape=None):
    a = np.asarray(a, dtype=np.float32)
    if shape is not None:
        a = np.ascontiguousarray(np.broadcast_to(a, shape))
    return jnp.asarray(a)


def _mix_prompt(x, mods, norm_gain, w_in, gn_gain, dw_w, dw_b, ln_g, ln_b, w_out, heads):
    nb, seq, d = x.shape
    tt = MIX_TOKEN_TILE
    ret_w = gn_gain.shape[0]
    dh = ret_w // heads
    n_taps, conv_w = dw_w.shape
    hist = -(-(n_taps - 1) // SUBLANES) * SUBLANES
    chunk = RET_CHUNK
    cos2, sin2 = (_const(t) for t in _rotary_tables(0, seq, dh // 2))
    dmask, dq, dk, gc = _decay_tables(heads, chunk)
    dmask = _const(dmask)
    dq_b = _const(dq[:, :, None], (heads, chunk, dh))
    dk_b = _const(dk[:, :, None], (heads, chunk, dh))
    gc_b = _const(gc[:, None, None], (heads, 1, dh))
    per_seq = seq // tt
    n_tiles = nb * per_seq

    def tile_a(i):
        return jnp.minimum(i, n_tiles - 1)

    def tile_b(i):
        return jnp.maximum(i - 1, 0)

    kern = functools.partial(_mix_prompt_kernel, heads=heads, hist=hist, per_seq=per_seq, n_tiles=n_tiles)
    return pl.pallas_call(
        kern,
        grid=(n_tiles + 1,),
        in_specs=[
            pl.BlockSpec((1, tt, d), lambda i: (tile_a(i) // per_seq, tile_a(i) % per_seq, 0)),
            pl.BlockSpec((1, tt, d), lambda i: (tile_b(i) // per_seq, tile_b(i) % per_seq, 0)),
            pl.BlockSpec((3, 1, 1, d), lambda i: (MIX_SUB_LAYER, tile_a(i) // per_seq, 0, 0)),
            pl.BlockSpec((3, 1, 1, d), lambda i: (MIX_SUB_LAYER, tile_b(i) // per_seq, 0, 0)),
            _resident((1, d)),
            _resident(w_in.shape),
            _resident((1, ret_w)),
            _resident((n_taps, conv_w)),
            _resident((1, conv_w)),
            _resident((1, conv_w)),
            _resident((1, conv_w)),
            _resident(w_out.shape),
            pl.BlockSpec((tt, dh), lambda i: (tile_b(i) % per_seq, 0)),
            pl.BlockSpec((tt, dh), lambda i: (tile_b(i) % per_seq, 0)),
            _resident((heads, chunk, chunk)),
            _resident((heads, chunk, dh)),
            _resident((heads, chunk, dh)),
            _resident((heads, 1, dh)),
        ],
        out_specs=[
            pl.BlockSpec((1, tt, d), lambda i: (tile_b(i) // per_seq, tile_b(i) % per_seq, 0)),
            pl.BlockSpec((1, heads, dh, dh), lambda i: (tile_b(i) // per_seq, 0, 0, 0)),
            pl.BlockSpec((1, n_taps - 1, conv_w), lambda i: (tile_a(i) // per_seq, 0, 0)),
        ],
        out_shape=[
            jax.ShapeDtypeStruct(x.shape, F32),
            jax.ShapeDtypeStruct((nb, heads, dh, dh), F32),
            jax.ShapeDtypeStruct((nb, n_taps - 1, conv_w), F32),
        ],
        scratch_shapes=[
            pltpu.VMEM((heads, dh, dh), F32),
            pltpu.VMEM((conv_w // LANES, hist + tt, LANES), F32),
            pltpu.VMEM((2, tt, d), BF16),
            pltpu.VMEM((2, tt, conv_w), BF16),
            pltpu.VMEM((tt, ret_w), BF16),
        ],
        compiler_params=pltpu.CompilerParams(dimension_semantics=("arbitrary",),
                                             vmem_limit_bytes=VMEM_LIMIT_BYTES),
        name="mix_prompt",
    )(x, x, mods, mods, norm_gain.reshape(1, d), w_in, gn_gain.reshape(1, -1), dw_w, dw_b.reshape(1, -1),
      ln_g.reshape(1, -1), ln_b.reshape(1, -1), w_out, cos2, sin2, dmask, dq_b, dk_b, gc_b)


def _sample_proj_kernel(x_ref, mod_ref, ng_ref, win_ref, o_ref, wbf_ref, hb_scr):
    n_tok, ns, d_ = x_ref.shape

    @pl.when(pl.program_id(0) == 0)
    def _():
        h = _rmsnorm(x_ref[...], ng_ref[...]) * (1.0 + mod_ref[1]) + mod_ref[0]
        hb_scr[...] = h.astype(BF16).reshape(n_tok * ns, d_)

    w = win_ref[...].astype(BF16)
    wbf_ref[...] = w
    o_ref[...] = _dot(hb_scr[...], w).reshape(n_tok, ns, -1)


def _sample_out_kernel(x_ref, mod_ref, mixed_ref, wout_ref, o_ref, wbf_ref):
    n_tok, ns, _ = x_ref.shape
    w = wout_ref[...].astype(BF16)
    wbf_ref[...] = w
    y = _dot(mixed_ref[...].reshape(n_tok * ns, -1), w).reshape(n_tok, ns, -1)
    o_ref[...] = x_ref[...] + mod_ref[2] * y


def _sample_core_kernel(proj_ref, state_ref, cin_ref, cos_ref, sin_ref, dm_ref, dq_ref, dk_ref, gc_ref,
                        gn_ref, dww_ref, dwb_ref, lng_ref, lnb_ref, mixed_ref, sout_ref, cout_ref,
                        qd_scr, kd_scr, v_scr, cross_scr, *, heads):
    n_tok, tb, _ = proj_ref.shape
    ret_w = gn_ref.shape[1]
    dh = ret_w // heads
    n_taps, conv_w = dww_ref.shape
    n_buf = n_taps - 1
    o_a, o_b = 4 * ret_w, 4 * ret_w + conv_w
    k_scale = dh ** -0.5
    cos2, sin2 = cos_ref[...], sin_ref[...]

    inner_heads = []
    for hd in range(heads):
        lo = hd * dh
        q = _rotate(proj_ref[:, :, lo:lo + dh], cos2, sin2)
        k = _rotate(proj_ref[:, :, ret_w + lo:ret_w + lo + dh], cos2, sin2) * k_scale
        v = proj_ref[:, :, 2 * ret_w + lo:2 * ret_w + lo + dh]
        qd_scr[:, :, lo:lo + dh] = q * dq_ref[hd]
        kd_scr[:, :, lo:lo + dh] = k * dk_ref[hd]
        v_scr[:, :, lo:lo + dh] = v
        rows_out = []
        for i in range(n_tok):
            acc = None
            for j in range(i + 1):
                s_ij = jnp.sum(q[i] * k[j], axis=-1, keepdims=True)
                term = (s_ij * dm_ref[hd, i, j]) * v[j]
                acc = term if acc is None else acc + term
            rows_out.append(acc)
        inner_heads.append(jnp.stack(rows_out, axis=0))

    row = lax.broadcasted_iota(jnp.int32, (n_tok, SUBLANES, dh), 1)

    def group(g, carry):
        rows = pl.ds(pl.multiple_of(g * SUBLANES, SUBLANES), SUBLANES)
        for hd in range(heads):
            lo = hd * dh
            q_tile = qd_scr[:, rows, lo:lo + dh].reshape(n_tok * SUBLANES, dh).astype(BF16)
            k_tile = kd_scr[:, rows, lo:lo + dh].reshape(n_tok * SUBLANES, dh).astype(BF16)
            v_group = v_scr[:, rows, lo:lo + dh]
            cross = jnp.zeros((n_tok, SUBLANES, dh), F32)
            for r in range(SUBLANES):
                b = g * SUBLANES + r
                state = state_ref[b, hd]
                out = _dot(q_tile, state.astype(BF16)).reshape(n_tok, SUBLANES, dh)
                cross = jnp.where(row == r, out, cross)
                v_own = jnp.where(row == r, v_group, 0.0).reshape(n_tok * SUBLANES, dh).astype(BF16)
                upd = lax.dot_general(k_tile, v_own, (((0,), (0,)), ((), ())), preferred_element_type=F32)
                sout_ref[b, hd] = gc_ref[hd] * state + upd
            cross_scr[:, rows, lo:lo + dh] = cross
        return carry

    lax.fori_loop(0, tb // SUBLANES, group, 0)

    for hd in range(heads):
        lo = hd * dh
        o = inner_heads[hd] + cross_scr[:, :, lo:lo + dh]
        gate_pre = proj_ref[:, :, 3 * ret_w + lo:3 * ret_w + lo + dh]
        mixed_ref[:, :, lo:lo + dh] = _group_norm_gate(o, gate_pre, gn_ref[:, lo:lo + dh]).astype(BF16)

    u = proj_ref[:, :, o_a:o_a + conv_w] * jax.nn.sigmoid(proj_ref[:, :, o_b:o_b + conv_w])

    def window(s):
        return cin_ref[s] if s < n_buf else u[s - n_buf]

    for t in range(n_tok):
        acc = None
        for j in range(n_taps):
            term = window(t + j) * dww_ref[j:j + 1, :]
            acc = term if acc is None else acc + term
        y = acc + dwb_ref[...]
        mixed_ref[t, :, ret_w:ret_w + conv_w] = _layer_norm_silu(y, lng_ref[...], lnb_ref[...]).astype(BF16)
    for s in range(n_buf):
        cout_ref[s] = window(s + n_tok)


def _mix_sample(x, mods, norm_gain, w_in, gn_gain, dw_w, dw_b, ln_g, ln_b, w_out, state_ret, conv_tm):
    n_tok, ns, d = x.shape
    tb = SAMPLE_MIX_SEQ_TILE
    heads, dh = state_ret.shape[1], state_ret.shape[2]
    ret_w = heads * dh
    n_taps, conv_w = dw_w.shape
    n_buf = n_taps - 1
    n_cols = w_in.shape[1]
    pc = SAMPLE_PROJ_COLS
    params = pltpu.CompilerParams(dimension_semantics=("arbitrary",),
                                  vmem_limit_bytes=SMALL_CALL_VMEM_LIMIT_BYTES)

    proj, w_in_bf = pl.pallas_call(
        _sample_proj_kernel,
        grid=(n_cols // pc,),
        in_specs=[_resident(x.shape),
                  pl.BlockSpec((3, 1, ns, d), lambda j: (MIX_SUB_LAYER, 0, 0, 0), pipeline_mode=pl.Buffered(1)),
                  _resident((1, d)),
                  pl.BlockSpec((d, pc), lambda j: (0, j))],
        out_specs=[pl.BlockSpec((n_tok, ns, pc), lambda j: (0, 0, j)),
                   pl.BlockSpec((d, pc), lambda j: (0, j))],
        out_shape=[jax.ShapeDtypeStruct((n_tok, ns, n_cols), F32),
                   jax.ShapeDtypeStruct(w_in.shape, BF16)],
        scratch_shapes=[pltpu.VMEM((n_tok * ns, d), BF16)],
        compiler_params=params,
        name="sample_proj",
    )(x, mods, norm_gain.reshape(1, d), w_in)

    cos2, sin2 = (_const(t) for t in _rotary_tables(PAST_LEN, n_tok, dh // 2))
    dmask, dq, dk, gc = _decay_tables(heads, n_tok)
    dm_b = _const(dmask[:, :, :, None, None], (heads, n_tok, n_tok, 1, dh))
    dq_b = _const(dq[:, :, None, None], (heads, n_tok, 1, dh))
    dk_b = _const(dk[:, :, None, None], (heads, n_tok, 1, dh))
    gc_b = _const(gc[:, None, None], (heads, 1, dh))
    mixed, state_new, conv_new = pl.pallas_call(
        functools.partial(_sample_core_kernel, heads=heads),
        grid=(ns // tb,),
        in_specs=[
            pl.BlockSpec((n_tok, tb, n_cols), lambda i: (0, i, 0)),
            pl.BlockSpec((tb, heads, dh, dh), lambda i: (i, 0, 0, 0)),
            pl.BlockSpec((n_buf, tb, conv_w), lambda i: (0, i, 0)),
            _resident((n_tok, 1, dh)),
            _resident((n_tok, 1, dh)),
            _resident((heads, n_tok, n_tok, 1, dh)),
            _resident((heads, n_tok, 1, dh)),
            _resident((heads, n_tok, 1, dh)),
            _resident((heads, 1, dh)),
            _resident((1, ret_w)),
            _resident((n_taps, conv_w)),
            _resident((1, conv_w)),
            _resident((1, conv_w)),
            _resident((1, conv_w)),
        ],
        out_specs=[
            pl.BlockSpec((n_tok, tb, ret_w + conv_w), lambda i: (0, i, 0)),
            pl.BlockSpec((tb, heads, dh, dh), lambda i: (i, 0, 0, 0)),
            pl.BlockSpec((n_buf, tb, conv_w), lambda i: (0, i, 0)),
        ],
        out_shape=[
            jax.ShapeDtypeStruct((n_tok, ns, ret_w + conv_w), BF16),
            jax.ShapeDtypeStruct(state_ret.shape, F32),
            jax.ShapeDtypeStruct(conv_tm.shape, F32),
        ],
        scratch_shapes=[pltpu.VMEM((n_tok, tb, ret_w), F32)] * 4,
        compiler_params=pltpu.CompilerParams(dimension_semantics=("arbitrary",),
                                             vmem_limit_bytes=VMEM_LIMIT_BYTES),
        name="sample_core",
    )(proj, state_ret, conv_tm, cos2.reshape(n_tok, 1, dh), sin2.reshape(n_tok, 1, dh), dm_b, dq_b, dk_b,
      gc_b, gn_gain.reshape(1, -1), dw_w, dw_b.reshape(1, -1), ln_g.reshape(1, -1), ln_b.reshape(1, -1))

    x_new, w_out_bf = pl.pallas_call(
        _sample_out_kernel,
        grid=(d // pc,),
        in_specs=[pl.BlockSpec((n_tok, ns, pc), lambda j: (0, 0, j)),
                  pl.BlockSpec((3, 1, ns, pc), lambda j: (MIX_SUB_LAYER, 0, 0, j)),
                  _resident(mixed.shape),
                  pl.BlockSpec((w_out.shape[0], pc), lambda j: (0, j))],
        out_specs=[pl.BlockSpec((n_tok, ns, pc), lambda j: (0, 0, j)),
                   pl.BlockSpec((w_out.shape[0], pc), lambda j: (0, j))],
        out_shape=[jax.ShapeDtypeStruct(x.shape, F32), jax.ShapeDtypeStruct(w_out.shape, BF16)],
        compiler_params=params,
        name="sample_out",
    )(x, mods, mixed, w_out)
    return x_new, state_new, conv_new, w_in_bf, w_out_bf


def _layer(xp, xs, mods_p, mods_s, sret, sconv, lw, final_gain, n_tok):
    (norm_ffn1, w1g, w1u, w1d, norm_mix, w_in, gn_gain, dw_w, dw_b, ln_g, ln_b, w_out,
     norm_ffn2, w2g, w2u, w2d) = lw
    nb, seq, d = xp.shape
    n_tok, ns, _ = xs.shape
    heads = sret.shape[1]
    mods_p = mods_p.reshape(N_MOD, nb, 1, d)
    mods_s_tm = mods_s.reshape(N_MOD, 1, ns, d)

    xp, xs = _ffn(xp, xs, mods_p, mods_s_tm, 0, norm_ffn1, w1g, w1u, w1d, None)
    xs, ret_s, conv_s_tm, w_in_bf, w_out_bf = _mix_sample(
        xs, mods_s_tm, norm_mix, w_in, gn_gain, dw_w, dw_b, ln_g, ln_b, w_out, sret,
        sconv.transpose(1, 0, 2))
    xp, ret_p, conv_p = _mix_prompt(xp, mods_p, norm_mix, w_in_bf, gn_gain, dw_w, dw_b, ln_g, ln_b,
                                    w_out_bf, heads)
    xp, xs = _ffn(xp, xs, mods_p, mods_s_tm, 2, norm_ffn2, w2g, w2u, w2d, final_gain)
    return xp, xs, ret_p, conv_p, ret_s, conv_s_tm.transpose(1, 0, 2)


def kernel(x_prompt, x_sample, c_prompt, c_sample, state_ret, state_conv, norm_ffn1, ffn1_w_gate,
           ffn1_w_up, ffn1_w_down, norm_mix, w_in, ret_gn_gain, dw_w, dw_b, conv_ln_gain, conv_ln_bias,
           w_out, norm_ffn2, ffn2_w_gate, ffn2_w_up, ffn2_w_down, w_ada, b_ada, norm_final):
    depth = w_in.shape[0]
    nb = x_prompt.shape[0]
    ns, n_tok, d = x_sample.shape
    assert n_tok <= SUBLANES and x_prompt.shape[1] % RET_CHUNK == 0

    xp = x_prompt
    xs = x_sample.transpose(1, 0, 2)

    ret_p, conv_p, ret_s, conv_s = [], [], [], []
    for l in range(depth):
        ada_p, ada_s = _ada(c_prompt, c_sample, w_ada[l], b_ada[l])
        lw = (norm_ffn1[l], ffn1_w_gate[l], ffn1_w_up[l], ffn1_w_down[l], norm_mix[l],
              w_in[l], ret_gn_gain[l], dw_w[l], dw_b[l], conv_ln_gain[l], conv_ln_bias[l],
              w_out[l], norm_ffn2[l], ffn2_w_gate[l], ffn2_w_up[l], ffn2_w_down[l])
        final_gain = norm_final if l == depth - 1 else None
        xp, xs, rp, cp, rs, cs = _layer(xp, xs, ada_p, ada_s, state_ret[l], state_conv[l], lw,
                                        final_gain, n_tok)
        ret_p.append(rp)
        conv_p.append(cp)
        ret_s.append(rs)
        conv_s.append(cs)

    return (xp, xs.transpose(1, 0, 2), jnp.stack(ret_p), jnp.stack(conv_p), jnp.stack(ret_s),
            jnp.stack(conv_s))
```

```python
import functools

import jax
import jax.numpy as jnp
import numpy as np
from jax import lax
from jax.experimental import pallas as pl
from jax.experimental.pallas import tpu as pltpu

F32 = jnp.float32
BF16 = jnp.bfloat16

PAST_LEN = 16384
RET_CHUNK = 256
ROPE_BASE = 10000.0
EPS = 1e-6
N_MOD = 9
MIX_SUB_LAYER = 1

SUBLANES = 8
LANES = 128
VMEM_LIMIT_BYTES = 56 * 1024 * 1024

FFN_TOKEN_TILE = 1024
FFN_FINAL_TOKEN_TILE = 512
MIX_TOKEN_TILE = 512
SAMPLE_MIX_SEQ_TILE = 32
SAMPLE_PROJ_COLS = 512
ADA_ROWS_PER_STEP = 128
FF_CHUNK = 256
FFN_STAGE_STEPS = 8


def _resident(shape):
    n = len(shape)
    return pl.BlockSpec(shape, lambda *_: (0,) * n, pipeline_mode=pl.Buffered(1))


def _rmsnorm(x, gain):
    ms = jnp.mean(x * x, axis=-1, keepdims=True)
    return x * lax.rsqrt(ms + EPS) * gain


def _dot(a, b):
    return jnp.dot(a, b, preferred_element_type=F32)


def _ada_kernel(cp_ref, cs_ref, w_ref, b_ref, op_ref, os_ref):
    k = pl.program_id(0)
    n_prompt = op_ref.shape[1]
    d = op_ref.shape[2]
    c = jnp.concatenate([cp_ref[...], cs_ref[...]], axis=0)
    h = (c * jax.nn.sigmoid(c)).astype(BF16)

    @pl.when(k == 0)
    def _():
        for m in range(op_ref.shape[0]):
            bias = b_ref[:, m * d:(m + 1) * d]
            op_ref[m] = jnp.broadcast_to(bias, op_ref.shape[1:])
            os_ref[m] = jnp.broadcast_to(bias, os_ref.shape[1:])

    for m in range(op_ref.shape[0]):
        part = _dot(h, w_ref[:, m * d:(m + 1) * d].astype(BF16))
        op_ref[m] += part[0:n_prompt]
        os_ref[m] += part[n_prompt:]


def _ada(c_prompt, c_sample, w_ada, b_ada):
    n_prompt, d = c_prompt.shape
    n_sample = c_sample.shape[0]
    tk = ADA_ROWS_PER_STEP
    return pl.pallas_call(
        _ada_kernel,
        grid=(d // tk,),
        in_specs=[
            pl.BlockSpec((n_prompt, tk), lambda k: (0, k)),
            pl.BlockSpec((n_sample, tk), lambda k: (0, k)),
            pl.BlockSpec((tk, N_MOD * d), lambda k: (k, 0)),
            _resident((1, N_MOD * d)),
        ],
        out_specs=[pl.BlockSpec((N_MOD, n_prompt, d), lambda k: (0, 0, 0)),
                   pl.BlockSpec((N_MOD, n_sample, d), lambda k: (0, 0, 0))],
        out_shape=[jax.ShapeDtypeStruct((N_MOD, n_prompt, d), F32),
                   jax.ShapeDtypeStruct((N_MOD, n_sample, d), F32)],
        compiler_params=pltpu.CompilerParams(dimension_semantics=("arbitrary",),
                                             vmem_limit_bytes=VMEM_LIMIT_BYTES),
        name="ada",
    )(c_prompt, c_sample, w_ada, b_ada.reshape(1, -1))


def _ffn_kernel(xp_ref, xs_ref, modp_ref, mods_ref, ng_ref, wg_ref, wu_ref, wd_ref, *rest,
                n_stage, n_prompt, final_norm, lookahead):
    rest = list(rest)
    fg_ref = rest.pop(0) if final_norm else None
    if lookahead:
        xn_ref, modn_ref = rest.pop(0), rest.pop(0)
    op_ref, os_ref, wg_scr, wu_scr, wd_scr, act_scr, hb_scr = rest
    i = pl.program_id(0)
    k = i - n_stage
    n_chunks, _, ffc = wg_scr.shape

    @pl.when(i < n_stage)
    def _():
        rows_in, rows_dn = wg_ref.shape[0], wd_ref.shape[0]
        r_in = pl.ds(pl.multiple_of(i * rows_in, rows_in), rows_in)
        for c in range(n_chunks):
            wg_scr[c, r_in, :] = wg_ref[:, c * ffc:(c + 1) * ffc].astype(BF16)
            wu_scr[c, r_in, :] = wu_ref[:, c * ffc:(c + 1) * ffc].astype(BF16)
        wd_scr[pl.ds(pl.multiple_of(i * rows_dn, 2 * SUBLANES), rows_dn), :] = wd_ref[...].astype(BF16)

    def norm_stage(x_ref, mod_ref, slot):
        x = x_ref[...]
        g_, r_, d_ = x.shape
        h = _rmsnorm(x, ng_ref[...]) * (1.0 + mod_ref[1]) + mod_ref[0]
        hb_scr[slot, 0:g_ * r_, :] = h.astype(BF16).reshape(g_ * r_, d_)

    def matmul_stage(x_ref, mod_ref, o_ref, slot):
        g_, r_, d_ = x_ref.shape
        rows = g_ * r_
        for c in range(n_chunks):
            g = _dot(hb_scr[slot, 0:rows, :], wg_scr[c])
            u = _dot(hb_scr[slot, 0:rows, :], wu_scr[c])
            act_scr[0:rows, c * ffc:(c + 1) * ffc] = (g * jax.nn.sigmoid(g) * u).astype(BF16)
        y = _dot(act_scr[0:rows, :], wd_scr[...]).reshape(g_, r_, d_)
        out = x_ref[...] + 0.5 * mod_ref[2] * y
        if final_norm:
            out = _rmsnorm(out, fg_ref[...])
        o_ref[...] = out

    if lookahead:
        @pl.when(i == n_stage - 1)
        def _():
            norm_stage(xn_ref, modn_ref, 0)

        @pl.when(jnp.logical_and(k >= 0, k < n_prompt))
        def _():
            norm_stage(xn_ref, modn_ref, lax.rem(k + 1, 2))
            matmul_stage(xp_ref, modp_ref, op_ref, lax.rem(k, 2))
    else:
        @pl.when(jnp.logical_and(k >= 0, k < n_prompt))
        def _():
            norm_stage(xp_ref, modp_ref, 0)
            matmul_stage(xp_ref, modp_ref, op_ref, 0)

    @pl.when(k == n_prompt)
    def _():
        sample_slot = n_prompt % 2 if lookahead else 0
        norm_stage(xs_ref, mods_ref, sample_slot)
        matmul_stage(xs_ref, mods_ref, os_ref, sample_slot)


def _ffn(xp, xs, mods_p, mods_s, sub_layer, norm_gain, wg, wu, wd, final_gain):
    nb, seq, d = xp.shape
    d_ff = wg.shape[1]
    final_norm = final_gain is not None
    tm = FFN_FINAL_TOKEN_TILE if final_norm else FFN_TOKEN_TILE
    ffc = FF_CHUNK
    assert seq % tm == 0 and d_ff % ffc == 0 and xs.shape[0] * xs.shape[1] <= tm
    per_seq = seq // tm
    n_stage, n_prompt, n_chunks = FFN_STAGE_STEPS, nb * per_seq, d_ff // ffc
    rows_in, rows_dn = d // n_stage, d_ff // n_stage
    assert rows_in * n_stage == d and rows_dn * n_stage == d_ff
    assert rows_in % (2 * SUBLANES) == 0 and rows_dn % (2 * SUBLANES) == 0

    def prompt_tile(i):
        return jnp.clip(i - n_stage, 0, n_prompt - 1)

    def stage(i):
        return jnp.minimum(i, n_stage - 1)

    x_spec = pl.BlockSpec((1, tm, d), lambda i: (prompt_tile(i) // per_seq, prompt_tile(i) % per_seq, 0))
    in_specs = [
        x_spec,
        _resident(xs.shape),
        pl.BlockSpec((3, 1, 1, d), lambda i: (sub_layer, prompt_tile(i) // per_seq, 0, 0)),
        pl.BlockSpec((3,) + mods_s.shape[1:], lambda i: (sub_layer, 0, 0, 0), pipeline_mode=pl.Buffered(1)),
        _resident((1, d)),
        pl.BlockSpec((rows_in, d_ff), lambda i: (stage(i), 0)),
        pl.BlockSpec((rows_in, d_ff), lambda i: (stage(i), 0)),
        pl.BlockSpec((rows_dn, d), lambda i: (stage(i), 0)),
    ]
    args = [xp, xs, mods_p, mods_s, norm_gain.reshape(1, d), wg, wu, wd]
    if final_norm:
        in_specs.append(_resident((1, d)))
        args.append(final_gain.reshape(1, d))
    lookahead = tm <= FFN_FINAL_TOKEN_TILE
    if lookahead:
        def next_tile(i):
            return jnp.clip(i - n_stage + 1, 0, n_prompt - 1)
        in_specs += [
            pl.BlockSpec((1, tm, d), lambda i: (next_tile(i) // per_seq, next_tile(i) % per_seq, 0)),
            pl.BlockSpec((3, 1, 1, d), lambda i: (sub_layer, next_tile(i) // per_seq, 0, 0)),
        ]
        args += [xp, mods_p]
    return pl.pallas_call(
        functools.partial(_ffn_kernel, n_stage=n_stage, n_prompt=n_prompt, final_norm=final_norm,
                          lookahead=lookahead),
        grid=(n_stage + n_prompt + 1,),
        in_specs=in_specs,
        out_specs=[x_spec, pl.BlockSpec(xs.shape, lambda i: (0, 0, 0))],
        out_shape=[jax.ShapeDtypeStruct(xp.shape, F32), jax.ShapeDtypeStruct(xs.shape, F32)],
        scratch_shapes=[
            pltpu.VMEM((n_chunks, d, ffc), BF16),
            pltpu.VMEM((n_chunks, d, ffc), BF16),
            pltpu.VMEM((d_ff, d), BF16),
            pltpu.VMEM((tm, d_ff), BF16),
            pltpu.VMEM((2 if lookahead else 1, tm, d), BF16),
        ],
        compiler_params=pltpu.CompilerParams(dimension_semantics=("arbitrary",),
                                             vmem_limit_bytes=VMEM_LIMIT_BYTES),
        name="ffn_final" if final_norm else "ffn",
    )(*args)


def _rotate(xh, cos2, sin2):
    return xh * cos2 + pltpu.roll(xh, xh.shape[-1] // 2, axis=xh.ndim - 1) * sin2


def _group_norm_gate(o, gate_pre, gain):
    mu = jnp.mean(o, axis=-1, keepdims=True)
    ctr = o - mu
    var = jnp.mean(ctr * ctr, axis=-1, keepdims=True)
    return gate_pre * jax.nn.sigmoid(gate_pre) * (ctr * lax.rsqrt(var + EPS) * gain)


def _layer_norm_silu(y, gain, bias):
    mu = jnp.mean(y, axis=-1, keepdims=True)
    ctr = y - mu
    var = jnp.mean(ctr * ctr, axis=-1, keepdims=True)
    yn = ctr * lax.rsqrt(var + EPS) * gain + bias
    return yn * jax.nn.sigmoid(yn)


def _conv_taps_slab(win_ref, dww_ref, s, n_out, lead):
    acc = None
    for j in range(dww_ref.shape[0]):
        term = win_ref[s, pl.ds(lead + j, n_out), :] * dww_ref[j:j + 1, s * LANES:(s + 1) * LANES]
        acc = term if acc is None else acc + term
    return acc


def _mix_prompt_kernel(xa_ref, xb_ref, moda_ref, modb_ref, ng_ref, win_ref, gn_ref, dww_ref, dwb_ref,
                       lng_ref, lnb_ref, wout_ref, cos_ref, sin_ref, dmask_ref, dq_ref, dk_ref, gc_ref,
                       o_ref, sout_ref, cout_ref, s_scr, u_scr, hb_scr, conv_scr, mix_scr,
                       *, heads, hist, per_seq, n_tiles):
    i = pl.program_id(0)
    tt, d_ = xa_ref.shape[1], xa_ref.shape[2]
    ret_w = gn_ref.shape[1]
    dh = ret_w // heads
    n_taps, conv_w = dww_ref.shape
    k_scale = dh ** -0.5
    o_a, o_b = 4 * ret_w, 4 * ret_w + conv_w
    n_slabs = u_scr.shape[0]

    slot_a = lax.rem(i, 2)
    slot_b = 1 - slot_a

    def stage_a_project():
        x = xa_ref[0]
        shift, scale = moda_ref[0, 0], moda_ref[1, 0]
        h = _rmsnorm(x, ng_ref[...]) * (1.0 + scale) + shift
        hb = h.astype(BF16)
        hb_scr[slot_a] = hb
        ab = _dot(hb, win_ref[:, o_a:o_b + conv_w])
        u = ab[:, :conv_w] * jax.nn.sigmoid(ab[:, conv_w:])
        for s in range(n_slabs):
            u_scr[s, hist:hist + tt, :] = u[:, s * LANES:(s + 1) * LANES]

    def stage_a_conv_slab(s):
        return _conv_taps_slab(u_scr, dww_ref, s, tt, hist - (n_taps - 1))

    def stage_a_finish(slabs):
        y = jnp.concatenate(slabs, axis=-1) + dwb_ref[...]
        conv_scr[slot_a] = _layer_norm_silu(y, lng_ref[...], lnb_ref[...]).astype(BF16)
        for s in range(n_slabs):
            u_scr[s, 0:hist, :] = u_scr[s, tt:tt + hist, :]

    def stage_b_project():
        return _dot(hb_scr[slot_b], win_ref[:, 0:o_a])

    def stage_b_head(proj, hd):
        lo = hd * dh
        cos2, sin2 = cos_ref[...], sin_ref[...]
        q = _rotate(proj[:, lo:lo + dh], cos2, sin2)
        k = _rotate(proj[:, ret_w + lo:ret_w + lo + dh], cos2, sin2) * k_scale
        v = proj[:, 2 * ret_w + lo:2 * ret_w + lo + dh]
        gate_pre = proj[:, 3 * ret_w + lo:3 * ret_w + lo + dh]
        dmask, dq, dk, gc = dmask_ref[hd], dq_ref[hd], dk_ref[hd], gc_ref[hd]
        state = s_scr[hd]
        outs = []
        for c0 in range(0, tt, RET_CHUNK):
            qc, kc = q[c0:c0 + RET_CHUNK], k[c0:c0 + RET_CHUNK]
            vb = v[c0:c0 + RET_CHUNK].astype(BF16)
            scores = lax.dot_general(qc.astype(BF16), kc.astype(BF16), (((1,), (1,)), ((), ())),
                                     preferred_element_type=F32)
            inner = _dot((scores * dmask).astype(BF16), vb)
            cross = _dot((qc * dq).astype(BF16), state.astype(BF16))
            outs.append(inner + cross)
            state = gc * state + lax.dot_general((kc * dk).astype(BF16), vb, (((0,), (0,)), ((), ())),
                                                 preferred_element_type=F32)
        s_scr[hd] = state
        o = jnp.concatenate(outs, axis=0) if len(outs) > 1 else outs[0]
        mix_scr[:, lo:lo + dh] = _group_norm_gate(o, gate_pre, gn_ref[:, lo:lo + dh]).astype(BF16)

    def stage_b_finish():
        mixed = (_dot(mix_scr[...], wout_ref[0:ret_w, :])
                 + _dot(conv_scr[slot_b], wout_ref[ret_w:ret_w + conv_w, :]))
        o_ref[0] = xb_ref[0] + modb_ref[2, 0] * mixed

    def run(do_a, do_b):
        if do_a:
            stage_a_project()
        if do_b:
            proj = stage_b_project()
            for hd in range(heads):
                stage_b_head(proj, hd)
            stage_b_finish()
        if do_a:
            stage_a_finish([stage_a_conv_slab(s) for s in range(n_slabs)])

    has_a = i < n_tiles
    has_b = i >= 1
    pos_a = lax.rem(i, per_seq)
    pos_b = lax.rem(i + per_seq - 1, per_seq)

    @pl.when(jnp.logical_and(has_a, pos_a == 0))
    def _():
        u_scr[:, 0:hist, :] = jnp.zeros((n_slabs, hist, LANES), F32)

    @pl.when(jnp.logical_and(has_b, pos_b == 0))
    def _():
        s_scr[...] = jnp.zeros_like(s_scr)

    @pl.when(i == 0)
    def _():
        run(True, False)

    @pl.when(jnp.logical_and(has_a, has_b))
    def _():
        run(True, True)

    @pl.when(i == n_tiles)
    def _():
        run(False, True)

    @pl.when(jnp.logical_and(has_a, pos_a == per_seq - 1))
    def _():
        for s in range(n_slabs):
            cout_ref[0, :, s * LANES:(s + 1) * LANES] = u_scr[s, hist - (n_taps - 1):hist, :]

    @pl.when(jnp.logical_and(has_b, pos_b == per_seq - 1))
    def _():
        sout_ref[0] = s_scr[...]


def _decay_tables(heads, chunk):
    lg = np.log(1.0 - 2.0 ** (-5.0 - np.arange(heads, dtype=np.float64)))
    idx = np.arange(chunk, dtype=np.float64)
    diff = idx[:, None] - idx[None, :]
    dmask = np.where(diff[None] >= 0, np.exp(lg[:, None, None] * np.maximum(diff, 0.0)[None]), 0.0)
    dq = np.exp(lg[:, None] * (idx[None, :] + 1.0))
    dk = np.exp(lg[:, None] * (chunk - 1.0 - idx[None, :]))
    gc = np.exp(lg * chunk)
    return dmask, dq, dk, gc


def _rotary_tables(pos0, n_pos, half):
    inv = ROPE_BASE ** (-np.arange(half, dtype=np.float64) / half)
    pos = (pos0 + np.arange(n_pos)).astype(np.float64)
    ang = pos[:, None] * inv[None, :]
    cos, sin = np.cos(ang), np.sin(ang)
    return np.concatenate([cos, cos], axis=-1), np.concatenate([-sin, sin], axis=-1)


def _const(a, shape=None):
    a = np.asarray(a, dtype=np.float32)
    if shape is not None:
        a = np.ascontiguousarray(np.broadcast_to(a, shape))
    return jnp.asarray(a)


def _mix_prompt(x, mods, norm_gain, w_in, gn_gain, dw_w, dw_b, ln_g, ln_b, w_out, heads):
    nb, seq, d = x.shape
    tt = MIX_TOKEN_TILE
    ret_w = gn_gain.shape[0]
    dh = ret_w // heads
    n_taps, conv_w = dw_w.shape
    hist = -(-(n_taps - 1) // SUBLANES) * SUBLANES
    chunk = RET_CHUNK
    cos2, sin2 = (_const(t) for t in _rotary_tables(0, seq, dh // 2))
    dmask, dq, dk, gc = _decay_tables(heads, chunk)
    dmask = _const(dmask)
    dq_b = _const(dq[:, :, None], (heads, chunk, dh))
    dk_b = _const(dk[:, :, None], (heads, chunk, dh))
    gc_b = _const(gc[:, None, None], (heads, 1, dh))
    per_seq = seq // tt
    n_tiles = nb * per_seq

    def tile_a(i):
        return jnp.minimum(i, n_tiles - 1)

    def tile_b(i):
        return jnp.maximum(i - 1, 0)

    kern = functools.partial(_mix_prompt_kernel, heads=heads, hist=hist, per_seq=per_seq, n_tiles=n_tiles)
    return pl.pallas_call(
        kern,
        grid=(n_tiles + 1,),
        in_specs=[
            pl.BlockSpec((1, tt, d), lambda i: (tile_a(i) // per_seq, tile_a(i) % per_seq, 0)),
            pl.BlockSpec((1, tt, d), lambda i: (tile_b(i) // per_seq, tile_b(i) % per_seq, 0)),
            pl.BlockSpec((3, 1, 1, d), lambda i: (MIX_SUB_LAYER, tile_a(i) // per_seq, 0, 0)),
            pl.BlockSpec((3, 1, 1, d), lambda i: (MIX_SUB_LAYER, tile_b(i) // per_seq, 0, 0)),
            _resident((1, d)),
            _resident(w_in.shape),
            _resident((1, ret_w)),
            _resident((n_taps, conv_w)),
            _resident((1, conv_w)),
            _resident((1, conv_w)),
            _resident((1, conv_w)),
            _resident(w_out.shape),
            pl.BlockSpec((tt, dh), lambda i: (tile_b(i) % per_seq, 0)),
            pl.BlockSpec((tt, dh), lambda i: (tile_b(i) % per_seq, 0)),
            _resident((heads, chunk, chunk)),
            _resident((heads, chunk, dh)),
            _resident((heads, chunk, dh)),
            _resident((heads, 1, dh)),
        ],
        out_specs=[
            pl.BlockSpec((1, tt, d), lambda i: (tile_b(i) // per_seq, tile_b(i) % per_seq, 0)),
            pl.BlockSpec((1, heads, dh, dh), lambda i: (tile_b(i) // per_seq, 0, 0, 0)),
            pl.BlockSpec((1, n_taps - 1, conv_w), lambda i: (tile_a(i) // per_seq, 0, 0)),
        ],
        out_shape=[
            jax.ShapeDtypeStruct(x.shape, F32),
            jax.ShapeDtypeStruct((nb, heads, dh, dh), F32),
            jax.ShapeDtypeStruct((nb, n_taps - 1, conv_w), F32),
        ],
        scratch_shapes=[
            pltpu.VMEM((heads, dh, dh), F32),
            pltpu.VMEM((conv_w // LANES, hist + tt, LANES), F32),
            pltpu.VMEM((2, tt, d), BF16),
            pltpu.VMEM((2, tt, conv_w), BF16),
            pltpu.VMEM((tt, ret_w), BF16),
        ],
        compiler_params=pltpu.CompilerParams(dimension_semantics=("arbitrary",),
                                             vmem_limit_bytes=VMEM_LIMIT_BYTES),
        name="mix_prompt",
    )(x, x, mods, mods, norm_gain.reshape(1, d), w_in, gn_gain.reshape(1, -1), dw_w, dw_b.reshape(1, -1),
      ln_g.reshape(1, -1), ln_b.reshape(1, -1), w_out, cos2, sin2, dmask, dq_b, dk_b, gc_b)


def _sample_proj_kernel(x_ref, mod_ref, ng_ref, win_ref, o_ref, wbf_ref, hb_scr):
    n_tok, ns, d_ = x_ref.shape

    @pl.when(pl.program_id(0) == 0)
    def _():
        h = _rmsnorm(x_ref[...], ng_ref[...]) * (1.0 + mod_ref[1]) + mod_ref[0]
        hb_scr[...] = h.astype(BF16).reshape(n_tok * ns, d_)

    w = win_ref[...].astype(BF16)
    wbf_ref[...] = w
    o_ref[...] = _dot(hb_scr[...], w).reshape(n_tok, ns, -1)


def _sample_out_kernel(x_ref, mod_ref, mixed_ref, wout_ref, o_ref, wbf_ref):
    n_tok, ns, _ = x_ref.shape
    w = wout_ref[...].astype(BF16)
    wbf_ref[...] = w
    y = _dot(mixed_ref[...].reshape(n_tok * ns, -1), w).reshape(n_tok, ns, -1)
    o_ref[...] = x_ref[...] + mod_ref[2] * y


def _sample_core_kernel(proj_ref, state_ref, cin_ref, cos_ref, sin_ref, dm_ref, dq_ref, dk_ref, gc_ref,
                        gn_ref, dww_ref, dwb_ref, lng_ref, lnb_ref, mixed_ref, sout_ref, cout_ref,
                        qd_scr, kd_scr, v_scr, cross_scr, *, heads):
    n_tok, tb, _ = proj_ref.shape
    ret_w = gn_ref.shape[1]
    dh = ret_w // heads
    n_taps, conv_w = dww_ref.shape
    n_buf = n_taps - 1
    o_a, o_b = 4 * ret_w, 4 * ret_w + conv_w
    k_scale = dh ** -0.5
    cos2, sin2 = cos_ref[...], sin_ref[...]

    inner_heads = []
    for hd in range(heads):
        lo = hd * dh
        q = _rotate(proj_ref[:, :, lo:lo + dh], cos2, sin2)
        k = _rotate(proj_ref[:, :, ret_w + lo:ret_w + lo + dh], cos2, sin2) * k_scale
        v = proj_ref[:, :, 2 * ret_w + lo:2 * ret_w + lo + dh]
        qd_scr[:, :, lo:lo + dh] = q * dq_ref[hd]
        kd_scr[:, :, lo:lo + dh] = k * dk_ref[hd]
        v_scr[:, :, lo:lo + dh] = v
        rows_out = []
        for i in range(n_tok):
            acc = None
            for j in range(i + 1):
                s_ij = jnp.sum(q[i] * k[j], axis=-1, keepdims=True)
                term = (s_ij * dm_ref[hd, i, j]) * v[j]
                acc = term if acc is None else acc + term
            rows_out.append(acc)
        inner_heads.append(jnp.stack(rows_out, axis=0))

    row = lax.broadcasted_iota(jnp.int32, (n_tok, SUBLANES, dh), 1)

    def group(g, carry):
        rows = pl.ds(pl.multiple_of(g * SUBLANES, SUBLANES), SUBLANES)
        for hd in range(heads):
            lo = hd * dh
            q_tile = qd_scr[:, rows, lo:lo + dh].reshape(n_tok * SUBLANES, dh).astype(BF16)
            k_tile = kd_scr[:, rows, lo:lo + dh].reshape(n_tok * SUBLANES, dh).astype(BF16)
            v_group = v_scr[:, rows, lo:lo + dh]
            cross = jnp.zeros((n_tok, SUBLANES, dh), F32)
            for r in range(SUBLANES):
                b = g * SUBLANES + r
                state = state_ref[b, hd]
                out = _dot(q_tile, state.astype(BF16)).reshape(n_tok, SUBLANES, dh)
                cross = jnp.where(row == r, out, cross)
                v_own = jnp.where(row == r, v_group, 0.0).reshape(n_tok * SUBLANES, dh).astype(BF16)
                upd = lax.dot_general(k_tile, v_own, (((0,), (0,)), ((), ())), preferred_element_type=F32)
                sout_ref[b, hd] = gc_ref[hd] * state + upd
            cross_scr[:, rows, lo:lo + dh] = cross
        return carry

    lax.fori_loop(0, tb // SUBLANES, group, 0)

    for hd in range(heads):
        lo = hd * dh
        o = inner_heads[hd] + cross_scr[:, :, lo:lo + dh]
        gate_pre = proj_ref[:, :, 3 * ret_w + lo:3 * ret_w + lo + dh]
        mixed_ref[:, :, lo:lo + dh] = _group_norm_gate(o, gate_pre, gn_ref[:, lo:lo + dh]).astype(BF16)

    u = proj_ref[:, :, o_a:o_a + conv_w] * jax.nn.sigmoid(proj_ref[:, :, o_b:o_b + conv_w])

    def window(s):
        return cin_ref[s] if s < n_buf else u[s - n_buf]

    for t in range(n_tok):
        acc = None
        for j in range(n_taps):
            term = window(t + j) * dww_ref[j:j + 1, :]
            acc = term if acc is None else acc + term
        y = acc + dwb_ref[...]
        mixed_ref[t, :, ret_w:ret_w + conv_w] = _layer_norm_silu(y, lng_ref[...], lnb_ref[...]).astype(BF16)
    for s in range(n_buf):
        cout_ref[s] = window(s + n_tok)


def _mix_sample(x, mods, norm_gain, w_in, gn_gain, dw_w, dw_b, ln_g, ln_b, w_out, state_ret, conv_tm):
    n_tok, ns, d = x.shape
    tb = SAMPLE_MIX_SEQ_TILE
    heads, dh = state_ret.shape[1], state_ret.shape[2]
    ret_w = heads * dh
    n_taps, conv_w = dw_w.shape
    n_buf = n_taps - 1
    n_cols = w_in.shape[1]
    pc = SAMPLE_PROJ_COLS
    params = pltpu.CompilerParams(dimension_semantics=("arbitrary",), vmem_limit_bytes=VMEM_LIMIT_BYTES)

    proj, w_in_bf = pl.pallas_call(
        _sample_proj_kernel,
        grid=(n_cols // pc,),
        in_specs=[_resident(x.shape),
                  pl.BlockSpec((3, 1, ns, d), lambda j: (MIX_SUB_LAYER, 0, 0, 0), pipeline_mode=pl.Buffered(1)),
                  _resident((1, d)),
                  pl.BlockSpec((d, pc), lambda j: (0, j))],
        out_specs=[pl.BlockSpec((n_tok, ns, pc), lambda j: (0, 0, j)),
                   pl.BlockSpec((d, pc), lambda j: (0, j))],
        out_shape=[jax.ShapeDtypeStruct((n_tok, ns, n_cols), F32),
                   jax.ShapeDtypeStruct(w_in.shape, BF16)],
        scratch_shapes=[pltpu.VMEM((n_tok * ns, d), BF16)],
        compiler_params=params,
        name="sample_proj",
    )(x, mods, norm_gain.reshape(1, d), w_in)

    cos2, sin2 = (_const(t) for t in _rotary_tables(PAST_LEN, n_tok, dh // 2))
    dmask, dq, dk, gc = _decay_tables(heads, n_tok)
    dm_b = _const(dmask[:, :, :, None, None], (heads, n_tok, n_tok, 1, dh))
    dq_b = _const(dq[:, :, None, None], (heads, n_tok, 1, dh))
    dk_b = _const(dk[:, :, None, None], (heads, n_tok, 1, dh))
    gc_b = _const(gc[:, None, None], (heads, 1, dh))
    mixed, state_new, conv_new = pl.pallas_call(
        functools.partial(_sample_core_kernel, heads=heads),
        grid=(ns // tb,),
        in_specs=[
            pl.BlockSpec((n_tok, tb, n_cols), lambda i: (0, i, 0)),
            pl.BlockSpec((tb, heads, dh, dh), lambda i: (i, 0, 0, 0)),
            pl.BlockSpec((n_buf, tb, conv_w), lambda i: (0, i, 0)),
            _resident((n_tok, 1, dh)),
            _resident((n_tok, 1, dh)),
            _resident((heads, n_tok, n_tok, 1, dh)),
            _resident((heads, n_tok, 1, dh)),
            _resident((heads, n_tok, 1, dh)),
            _resident((heads, 1, dh)),
            _resident((1, ret_w)),
            _resident((n_taps, conv_w)),
            _resident((1, conv_w)),
            _resident((1, conv_w)),
            _resident((1, conv_w)),
        ],
        out_specs=[
            pl.BlockSpec((n_tok, tb, ret_w + conv_w), lambda i: (0, i, 0)),
            pl.BlockSpec((tb, heads, dh, dh), lambda i: (i, 0, 0, 0)),
            pl.BlockSpec((n_buf, tb, conv_w), lambda i: (0, i, 0)),
        ],
        out_shape=[
            jax.ShapeDtypeStruct((n_tok, ns, ret_w + conv_w), BF16),
            jax.ShapeDtypeStruct(state_ret.shape, F32),
            jax.ShapeDtypeStruct(conv_tm.shape, F32),
        ],
        scratch_shapes=[pltpu.VMEM((n_tok, tb, ret_w), F32)] * 4,
        compiler_params=pltpu.CompilerParams(dimension_semantics=("arbitrary",),
                                             vmem_limit_bytes=VMEM_LIMIT_BYTES),
        name="sample_core",
    )(proj, state_ret, conv_tm, cos2.reshape(n_tok, 1, dh), sin2.reshape(n_tok, 1, dh), dm_b, dq_b, dk_b,
      gc_b, gn_gain.reshape(1, -1), dw_w, dw_b.reshape(1, -1), ln_g.reshape(1, -1), ln_b.reshape(1, -1))

    x_new, w_out_bf = pl.pallas_call(
        _sample_out_kernel,
        grid=(d // pc,),
        in_specs=[pl.BlockSpec((n_tok, ns, pc), lambda j: (0, 0, j)),
                  pl.BlockSpec((3, 1, ns, pc), lambda j: (MIX_SUB_LAYER, 0, 0, j)),
                  _resident(mixed.shape),
                  pl.BlockSpec((w_out.shape[0], pc), lambda j: (0, j))],
        out_specs=[pl.BlockSpec((n_tok, ns, pc), lambda j: (0, 0, j)),
                   pl.BlockSpec((w_out.shape[0], pc), lambda j: (0, j))],
        out_shape=[jax.ShapeDtypeStruct(x.shape, F32), jax.ShapeDtypeStruct(w_out.shape, BF16)],
        compiler_params=params,
        name="sample_out",
    )(x, mods, mixed, w_out)
    return x_new, state_new, conv_new, w_in_bf, w_out_bf


def _layer(xp, xs, mods_p, mods_s, sret, sconv, lw, final_gain, n_tok):
    (norm_ffn1, w1g, w1u, w1d, norm_mix, w_in, gn_gain, dw_w, dw_b, ln_g, ln_b, w_out,
     norm_ffn2, w2g, w2u, w2d) = lw
    nb, seq, d = xp.shape
    n_tok, ns, _ = xs.shape
    heads = sret.shape[1]
    mods_p = mods_p.reshape(N_MOD, nb, 1, d)
    mods_s_tm = mods_s.reshape(N_MOD, 1, ns, d)

    xp, xs = _ffn(xp, xs, mods_p, mods_s_tm, 0, norm_ffn1, w1g, w1u, w1d, None)
    xs, ret_s, conv_s_tm, w_in_bf, w_out_bf = _mix_sample(
        xs, mods_s_tm, norm_mix, w_in, gn_gain, dw_w, dw_b, ln_g, ln_b, w_out, sret,
        sconv.transpose(1, 0, 2))
    xp, ret_p, conv_p = _mix_prompt(xp, mods_p, norm_mix, w_in_bf, gn_gain, dw_w, dw_b, ln_g, ln_b,
                                    w_out_bf, heads)
    xp, xs = _ffn(xp, xs, mods_p, mods_s_tm, 2, norm_ffn2, w2g, w2u, w2d, final_gain)
    return xp, xs, ret_p, conv_p, ret_s, conv_s_tm.transpose(1, 0, 2)


def kernel(x_prompt, x_sample, c_prompt, c_sample, state_ret, state_conv, norm_ffn1, ffn1_w_gate,
           ffn1_w_up, ffn1_w_down, norm_mix, w_in, ret_gn_gain, dw_w, dw_b, conv_ln_gain, conv_ln_bias,
           w_out, norm_ffn2, ffn2_w_gate, ffn2_w_up, ffn2_w_down, w_ada, b_ada, norm_final):
    depth = w_in.shape[0]
    nb = x_prompt.shape[0]
    ns, n_tok, d = x_sample.shape
    assert n_tok <= SUBLANES and x_prompt.shape[1] % RET_CHUNK == 0

    xp = x_prompt
    xs = x_sample.transpose(1, 0, 2)

    ret_p, conv_p, ret_s, conv_s = [], [], [], []
    for l in range(depth):
        ada_p, ada_s = _ada(c_prompt, c_sample, w_ada[l], b_ada[l])
        lw = (norm_ffn1[l], ffn1_w_gate[l], ffn1_w_up[l], ffn1_w_down[l], norm_mix[l],
              w_in[l], ret_gn_gain[l], dw_w[l], dw_b[l], conv_ln_gain[l], conv_ln_bias[l],
              w_out[l], norm_ffn2[l], ffn2_w_gate[l], ffn2_w_up[l], ffn2_w_down[l])
        final_gain = norm_final if l == depth - 1 else None
        xp, xs, rp, cp, rs, cs = _layer(xp, xs, ada_p, ada_s, state_ret[l], state_conv[l], lw,
                                        final_gain, n_tok)
        ret_p.append(rp)
        conv_p.append(cp)
        ret_s.append(rs)
        conv_s.append(cs)

    return (xp, xs.transpose(1, 0, 2), jnp.stack(ret_p), jnp.stack(conv_p), jnp.stack(ret_s),
            jnp.stack(conv_s))
```
